```python
import jax, jax.numpy as jnp
from jax import lax
import numpy as np

D_MODEL = 1024
BATCH = 8
SEQ = 4096
DEPTH = 2

MEM_LEN = 256
EPS = 1e-6
CONV_WIDTH = 4
SSD_D_INNER = D_MODEL
SSD_HEAD_DIM = 64
SSD_HEADS = SSD_D_INNER // SSD_HEAD_DIM
SSD_GROUPS = 2
SSD_STATE = 128
SSD_XBC = SSD_D_INNER + 2 * SSD_GROUPS * SSD_STATE
SSD_CHUNK = 128
MLSTM_D_INNER = D_MODEL
MLSTM_HEADS = 4
MLSTM_HEAD_DIM = MLSTM_D_INNER // MLSTM_HEADS
MLSTM_CHUNK = 128
EVEN_IN = SSD_D_INNER + SSD_XBC + SSD_HEADS + 2 * MLSTM_D_INNER + 2 * MLSTM_HEADS
HGRN_D_INNER = D_MODEL
HGRN_HEAD_DIM = 128
HGRN_HEADS = HGRN_D_INNER // HGRN_HEAD_DIM
HGRN_CHUNK = 64
XATTN_HEADS = 4
XATTN_HEAD_DIM = D_MODEL // XATTN_HEADS
PEER_HEADS = 8
PEER_N_KEYS = 128
PEER_N_EXPERTS = PEER_N_KEYS * PEER_N_KEYS
PEER_TOPK = 16
PEER_QUERY_DIM = 256
PEER_HALF = PEER_QUERY_DIM // 2
PEER_BLOCK = 128
N_EVEN = (DEPTH + 1) // 2
N_ODD = DEPTH // 2

kernel_name = 'hybrid_ssd_mlstm_hgrn2_peer_trunk'

F32 = jnp.float32


def head_rms_norm(x, w, n_groups):
    shp = x.shape
    x32 = x.astype(F32).reshape(shp[:-1] + (n_groups, shp[-1] // n_groups))
    y = x32 * lax.rsqrt(jnp.mean(x32 * x32, axis=-1, keepdims=True) + EPS)
    return (y.reshape(shp) * w.astype(F32)).astype(x.dtype)


def rms_norm(x, w):
    return head_rms_norm(x, w, 1)


def causal_conv(x, w, b):
    y = lax.conv_general_dilated(x, w[:, None, :], window_strides=(1,), padding=[(w.shape[0] - 1, 0)],
                                 dimension_numbers=('NWC', 'WIO', 'NWC'), feature_group_count=x.shape[-1])
    return y + b


def ssd_mixer(z, xbc, dt_raw, conv_w, conv_b, dt_bias, a_log, d_skip, norm_w):
    bsz, seq, _ = z.shape
    L = SSD_CHUNK; nc = seq // L; G = SSD_GROUPS; E = SSD_HEADS // SSD_GROUPS
    P = SSD_HEAD_DIM; N = SSD_STATE
    xbc = jax.nn.silu(causal_conv(xbc, conv_w, conv_b))
    xs = xbc[..., :SSD_D_INNER].astype(F32).reshape(bsz, nc, L, G, E, P)
    bm = xbc[..., SSD_D_INNER:SSD_D_INNER + G * N].astype(F32).reshape(bsz, nc, L, G, N)
    cm = xbc[..., SSD_D_INNER + G * N:].astype(F32).reshape(bsz, nc, L, G, N)
    dt = jax.nn.softplus(dt_raw.astype(F32) + dt_bias.astype(F32)).reshape(bsz, nc, L, G, E)
    a = -jnp.exp(a_log.astype(F32)).reshape(G, E)
    cs = jnp.cumsum(jnp.transpose(dt * a, (0, 3, 4, 1, 2)), axis=-1)
    x_dt = xs * dt[..., None]
    causal = jnp.tril(jnp.ones((L, L), dtype=bool))
    decay = jnp.exp(jnp.where(causal, cs[..., :, None] - cs[..., None, :], -jnp.inf))
    cb = jnp.einsum('bclgn,bcsgn->bgcls', cm, bm)
    y_diag = jnp.einsum('bgcls,bgecls,bcsgep->bclgep', cb, decay, x_dt)
    decay_to_end = jnp.exp(cs[..., -1:] - cs)
    chunk_states = jnp.einsum('bcsgn,bgecs,bcsgep->bcgepn', bm, decay_to_end, x_dt)
    chunk_decay = jnp.exp(cs[..., -1])

    def step(state, inp):
        s_c, a_c = inp
        return a_c[..., None, None] * state + s_c, state

    init = jnp.zeros((bsz, G, E, P, N), F32)
    _, prev = lax.scan(step, init, (jnp.moveaxis(chunk_states, 1, 0), jnp.moveaxis(chunk_decay, 3, 0)))
    y_off = jnp.einsum('bclgn,cbgepn,bgecl->bclgep', cm, prev, jnp.exp(cs))
    y = y_diag + y_off + xs * d_skip.astype(F32).reshape(G, E, 1)
    y = y.reshape(bsz, seq, SSD_D_INNER) * jax.nn.silu(z.astype(F32))
    return head_rms_norm(y, norm_w, G).astype(z.dtype)


def mlstm_mixer(xm, og, ig, fg, conv_w, conv_b, wq, wk, wv, i_bias, f_bias, norm_w):
    bsz, seq, _ = xm.shape
    H = MLSTM_HEADS; Dh = MLSTM_HEAD_DIM; L = MLSTM_CHUNK; nc = seq // L
    xc = jax.nn.silu(causal_conv(xm, conv_w, conv_b))

    def heads(t, w):
        y = jnp.einsum('bshd,hde->bhse', t.reshape(bsz, seq, H, Dh), w)
        return y.astype(F32).reshape(bsz, H, nc, L, Dh)

    q = heads(xc, wq)
    k = heads(xc, wk) * (Dh ** -0.5)
    v = heads(xm, wv)
    i_t = jnp.transpose(ig.astype(F32) + i_bias.astype(F32), (0, 2, 1)).reshape(bsz, H, nc, L)
    logf = jnp.transpose(jax.nn.log_sigmoid(fg.astype(F32) + f_bias.astype(F32)), (0, 2, 1)).reshape(bsz, H, nc, L)
    bcum = jnp.cumsum(logf, axis=-1)
    causal = jnp.tril(jnp.ones((L, L), dtype=bool))
    dlog = jnp.where(causal, bcum[..., :, None] - bcum[..., None, :] + i_t[..., None, :], -jnp.inf)
    m_intra = jnp.max(dlog, axis=-1)
    g_end = bcum[..., -1:] - bcum + i_t
    m_chunk = jnp.max(g_end, axis=-1)
    w_end = jnp.exp(g_end - m_chunk[..., None])
    c_local = jnp.einsum('bhcs,bhcsd,bhcse->bhcde', w_end, k, v)
    n_local = jnp.einsum('bhcs,bhcsd->bhcd', w_end, k)
    b_end = bcum[..., -1]

    def step(carry, inp):
        c_st, n_st, m_st = carry
        be, mc, cl, nl = inp
        m_new = jnp.maximum(be + m_st, mc)
        a_old = jnp.exp(be + m_st - m_new)
        a_new = jnp.exp(mc - m_new)
        c_new = a_old[..., None, None] * c_st + a_new[..., None, None] * cl
        n_new = a_old[..., None] * n_st + a_new[..., None] * nl
        return (c_new, n_new, m_new), (c_st, n_st, m_st)

    init = (jnp.zeros((bsz, H, Dh, Dh), F32), jnp.zeros((bsz, H, Dh), F32), jnp.zeros((bsz, H), F32))
    mv = lambda t: jnp.moveaxis(t, 2, 0)
    _, (c_prev, n_prev, m_prev) = lax.scan(step, init, (mv(b_end), mv(m_chunk), mv(c_local), mv(n_local)))
    m_prev_b = jnp.moveaxis(m_prev, 0, 2)[..., None]
    m_t = jnp.maximum(bcum + m_prev_b, m_intra)
    a_t = jnp.exp(bcum + m_prev_b - m_t)
    qk = jnp.einsum('bhcld,bhcsd->bhcls', q, k) * jnp.exp(dlog - m_t[..., None])
    num = a_t[..., None] * jnp.einsum('bhcld,cbhde->bhcle', q, c_prev) + jnp.einsum('bhcls,bhcse->bhcle', qk, v)
    den = a_t * jnp.einsum('bhcld,cbhd->bhcl', q, n_prev) + jnp.sum(qk, axis=-1)
    hid = num / jnp.maximum(jnp.abs(den), jnp.exp(-m_t))[..., None]
    hid = jnp.transpose(hid, (0, 2, 3, 1, 4)).reshape(bsz, seq, H * Dh)
    hid = head_rms_norm(hid, norm_w, H) * jax.nn.sigmoid(og.astype(F32))
    return hid.astype(xm.dtype)


def hgrn2_mixer(uq, uf, ui, ug, lb, norm_w):
    bsz, seq, _ = uq.shape
    H = HGRN_HEADS; Dk = HGRN_HEAD_DIM; L = HGRN_CHUNK; nc = seq // L

    def heads(t):
        return jnp.transpose(t.reshape(bsz, nc, L, H, -1), (0, 3, 1, 2, 4))

    uf32 = uf.astype(F32)
    lb = lb.astype(F32)
    logf = heads(jnp.log(lb + (1.0 - lb) * jax.nn.sigmoid(uf32)))
    k = heads((1.0 - lb) * jax.nn.sigmoid(-uf32))
    q = heads(jax.nn.silu(uq.astype(F32)))
    v = heads(ui.astype(F32))
    gc = jnp.cumsum(logf, axis=3)
    g_ref = gc[:, :, :, L // 2:L // 2 + 1, :]
    causal = jnp.tril(jnp.ones((L, L), dtype=bool))
    att = jnp.einsum('bhcld,bhcsd->bhcls', q * jnp.exp(gc - g_ref), k * jnp.exp(g_ref - gc))
    att = jnp.where(causal, att, 0.0)
    o = jnp.einsum('bhcls,bhcse->bhcle', att, v)
    s_local = jnp.einsum('bhcsd,bhcse->bhcde', k * jnp.exp(gc[..., -1:, :] - gc), v)
    chunk_decay = jnp.exp(gc[..., -1, :])

    def step(state, inp):
        a_c, s_c = inp
        return a_c[..., :, None] * state + s_c, state

    init = jnp.zeros((bsz, H, Dk, Dk), F32)
    _, s_prev = lax.scan(step, init, (jnp.moveaxis(chunk_decay, 2, 0), jnp.moveaxis(s_local, 2, 0)))
    o = o + jnp.einsum('bhcld,cbhde->bhcle', q * jnp.exp(gc), s_prev)
    o = jnp.transpose(o, (0, 2, 3, 1, 4)).reshape(bsz, seq, HGRN_D_INNER)
    o = head_rms_norm(o, norm_w, H) * jax.nn.silu(ug.astype(F32))
    return o.astype(uq.dtype)


def even_mixer(hn, w_in, w_out, ssd_conv_w, ssd_conv_b, ssd_dt_bias, ssd_a_log, ssd_d_skip, ssd_norm,
               ml_conv_w, ml_conv_b, ml_wq, ml_wk, ml_wv, ml_i_bias, ml_f_bias, ml_norm):
    proj = hn @ w_in
    o1 = SSD_D_INNER
    o2 = o1 + SSD_XBC
    o3 = o2 + SSD_HEADS
    o4 = o3 + MLSTM_D_INNER
    o5 = o4 + MLSTM_D_INNER
    o6 = o5 + MLSTM_HEADS
    z, xbc, dt_raw, xm, og, ig, fg = jnp.split(proj, [o1, o2, o3, o4, o5, o6], axis=-1)
    y_a = ssd_mixer(z, xbc, dt_raw, ssd_conv_w, ssd_conv_b, ssd_dt_bias, ssd_a_log, ssd_d_skip, ssd_norm)
    y_b = mlstm_mixer(xm, og, ig, fg, ml_conv_w, ml_conv_b, ml_wq, ml_wk, ml_wv, ml_i_bias, ml_f_bias, ml_norm)
    return jnp.concatenate([y_a, y_b], axis=-1).astype(hn.dtype) @ w_out


def odd_mixer(hn, w_in, w_out, lb, norm_w):
    uq, uf, ui, ug = jnp.split(hn @ w_in, 4, axis=-1)
    return hgrn2_mixer(uq, uf, ui, ug, lb, norm_w).astype(hn.dtype) @ w_out


def mem_cross_attention(hn, mem_n, wq, wkv, wo):
    bsz, seq, _ = hn.shape
    q = (hn @ wq).reshape(bsz, seq, XATTN_HEADS, XATTN_HEAD_DIM)
    k, v = jnp.split(mem_n @ wkv, 2, axis=-1)
    k = k.reshape(bsz, -1, XATTN_HEADS, XATTN_HEAD_DIM)
    v = v.reshape(bsz, -1, XATTN_HEADS, XATTN_HEAD_DIM)
    scores = jnp.einsum('bshd,bmhd->bhsm', q, k).astype(F32) * (XATTN_HEAD_DIM ** -0.5)
    p = jax.nn.softmax(scores, axis=-1).astype(v.dtype)
    o = jnp.einsum('bhsm,bmhd->bshd', p, v).reshape(bsz, seq, D_MODEL)
    return o @ wo


def peer_ffn(hn, wq, subkeys, u_tab, v_tab):
    bsz, seq, _ = hn.shape
    T = bsz * seq
    xt = hn.reshape(T, D_MODEL)
    q = (xt @ wq).reshape(T, PEER_HEADS, 2, PEER_HALF)
    s = jnp.einsum('thpd,hpkd->thpk', q, subkeys).astype(F32)
    s1, i1 = lax.top_k(s[:, :, 0], PEER_TOPK)
    s2, i2 = lax.top_k(s[:, :, 1], PEER_TOPK)
    cand = (s1[..., :, None] + s2[..., None, :]).reshape(T, PEER_HEADS, PEER_TOPK * PEER_TOPK)
    cand_idx = (i1[..., :, None] * PEER_N_KEYS + i2[..., None, :]).reshape(T, PEER_HEADS, PEER_TOPK * PEER_TOPK)
    top_s, pos = lax.top_k(cand, PEER_TOPK)
    idx = jnp.take_along_axis(cand_idx, pos, axis=-1)
    gate = jax.nn.softmax(top_s, axis=-1)
    nb = T // PEER_BLOCK

    def expert_block(args):
        xb, ib, gb = args
        ub = u_tab[ib]
        vb = v_tab[ib]
        act = jax.nn.gelu(jnp.einsum('td,thkd->thk', xb, ub).astype(F32), approximate=False)
        return jnp.einsum('thk,thkd->td', (gb * act).astype(vb.dtype), vb)

    out = lax.map(expert_block, (xt.reshape(nb, PEER_BLOCK, D_MODEL),
                                 idx.reshape(nb, PEER_BLOCK, PEER_HEADS, PEER_TOPK),
                                 gate.reshape(nb, PEER_BLOCK, PEER_HEADS, PEER_TOPK)))
    return out.reshape(bsz, seq, D_MODEL).astype(hn.dtype)


def setup_inputs(seed: int = 0) -> dict:
    key = jax.random.key(seed)
    ks = jax.random.split(key, 40)
    nrm = lambda k, shape, scale: jax.random.normal(k, shape, F32) * scale
    gain = lambda k, shape: 1.0 + 0.02 * jax.random.normal(k, shape, F32)
    dt = jnp.exp(jax.random.uniform(ks[10], (N_EVEN, SSD_HEADS), F32) * (np.log(0.1) - np.log(0.001)) + np.log(0.001))
    f_bias = jnp.linspace(3.0, 6.0, MLSTM_HEADS, dtype=F32)[None, :] + 0.1 * jax.random.normal(ks[21], (N_EVEN, MLSTM_HEADS), F32)
    return {
        'x': nrm(ks[0], (BATCH, SEQ, D_MODEL), 1.0),
        'mem': nrm(ks[1], (BATCH, MEM_LEN, D_MODEL), 1.0),
        'norm_mix': gain(ks[2], (DEPTH, D_MODEL)),
        'norm_xattn': gain(ks[3], (DEPTH, D_MODEL)),
        'norm_mem': gain(ks[4], (DEPTH, D_MODEL)),
        'norm_ffn': gain(ks[5], (DEPTH, D_MODEL)),
        'norm_final': gain(ks[6], (D_MODEL,)),
        'ev_w_in': nrm(ks[7], (N_EVEN, D_MODEL, EVEN_IN), D_MODEL ** -0.5),
        'ev_w_out': nrm(ks[8], (N_EVEN, SSD_D_INNER + MLSTM_D_INNER, D_MODEL), (SSD_D_INNER + MLSTM_D_INNER) ** -0.5),
        'ssd_conv_w': nrm(ks[9], (N_EVEN, CONV_WIDTH, SSD_XBC), CONV_WIDTH ** -0.5),
        'ssd_conv_b': nrm(ks[11], (N_EVEN, SSD_XBC), 0.02),
        'ssd_dt_bias': dt + jnp.log(-jnp.expm1(-dt)),
        'ssd_a_log': jnp.log(jax.random.uniform(ks[12], (N_EVEN, SSD_HEADS), F32, 1.0, 16.0)),
        'ssd_d_skip': 1.0 + 0.1 * jax.random.normal(ks[13], (N_EVEN, SSD_HEADS), F32),
        'ssd_norm': gain(ks[14], (N_EVEN, SSD_D_INNER)),
        'ml_conv_w': nrm(ks[15], (N_EVEN, CONV_WIDTH, MLSTM_D_INNER), CONV_WIDTH ** -0.5),
        'ml_conv_b': nrm(ks[16], (N_EVEN, MLSTM_D_INNER), 0.02),
        'ml_wq': nrm(ks[17], (N_EVEN, MLSTM_HEADS, MLSTM_HEAD_DIM, MLSTM_HEAD_DIM), MLSTM_HEAD_DIM ** -0.5),
        'ml_wk': nrm(ks[18], (N_EVEN, MLSTM_HEADS, MLSTM_HEAD_DIM, MLSTM_HEAD_DIM), MLSTM_HEAD_DIM ** -0.5),
        'ml_wv': nrm(ks[19], (N_EVEN, MLSTM_HEADS, MLSTM_HEAD_DIM, MLSTM_HEAD_DIM), MLSTM_HEAD_DIM ** -0.5),
        'ml_i_bias': nrm(ks[20], (N_EVEN, MLSTM_HEADS), 0.1),
        'ml_f_bias': f_bias,
        'ml_norm': gain(ks[22], (N_EVEN, MLSTM_D_INNER)),
        'od_w_in': nrm(ks[23], (N_ODD, D_MODEL, 4 * HGRN_D_INNER), D_MODEL ** -0.5),
        'od_w_out': nrm(ks[24], (N_ODD, HGRN_D_INNER, D_MODEL), HGRN_D_INNER ** -0.5),
        'hgrn_lb_logits': nrm(ks[25], (DEPTH, HGRN_D_INNER), 0.1),
        'hgrn_norm': gain(ks[26], (N_ODD, HGRN_D_INNER)),
        'xa_wq': nrm(ks[27], (DEPTH, D_MODEL, D_MODEL), D_MODEL ** -0.5),
        'xa_wkv': nrm(ks[28], (DEPTH, D_MODEL, 2 * D_MODEL), D_MODEL ** -0.5),
        'xa_wo': nrm(ks[29], (DEPTH, D_MODEL, D_MODEL), D_MODEL ** -0.5),
        'peer_wq': nrm(ks[30], (DEPTH, D_MODEL, PEER_HEADS * PEER_QUERY_DIM), D_MODEL ** -0.5),
        'peer_subkeys': nrm(ks[31], (DEPTH, PEER_HEADS, 2, PEER_N_KEYS, PEER_HALF), PEER_HALF ** -0.5),
        'peer_u': nrm(ks[32], (DEPTH, PEER_N_EXPERTS, D_MODEL), D_MODEL ** -0.5),
        'peer_v': nrm(ks[33], (DEPTH, PEER_N_EXPERTS, D_MODEL), (PEER_HEADS * PEER_TOPK) ** -0.5),
    }


def reference(x, mem, norm_mix, norm_xattn, norm_mem, norm_ffn, norm_final,
              ev_w_in, ev_w_out, ssd_conv_w, ssd_conv_b, ssd_dt_bias, ssd_a_log, ssd_d_skip, ssd_norm,
              ml_conv_w, ml_conv_b, ml_wq, ml_wk, ml_wv, ml_i_bias, ml_f_bias, ml_norm,
              od_w_in, od_w_out, hgrn_lb_logits, hgrn_norm,
              xa_wq, xa_wkv, xa_wo,
              peer_wq, peer_subkeys, peer_u, peer_v):
    lb_cum = jnp.cumsum(jax.nn.softmax(hgrn_lb_logits.astype(F32), axis=0), axis=0)
    h = x
    for layer in range(DEPTH):
        hn = rms_norm(h, norm_mix[layer])
        if layer % 2 == 0:
            e = layer // 2
            mix = even_mixer(hn, ev_w_in[e], ev_w_out[e], ssd_conv_w[e], ssd_conv_b[e], ssd_dt_bias[e],
                             ssd_a_log[e], ssd_d_skip[e], ssd_norm[e], ml_conv_w[e], ml_conv_b[e],
                             ml_wq[e], ml_wk[e], ml_wv[e], ml_i_bias[e], ml_f_bias[e], ml_norm[e])
        else:
            o = layer // 2
            mix = odd_mixer(hn, od_w_in[o], od_w_out[o], lb_cum[layer - 1], hgrn_norm[o])
        h = h + mix.astype(h.dtype)
        mem_n = rms_norm(mem, norm_mem[layer])
        h = h + mem_cross_attention(rms_norm(h, norm_xattn[layer]), mem_n, xa_wq[layer], xa_wkv[layer], xa_wo[layer]).astype(h.dtype)
        h = h + peer_ffn(rms_norm(h, norm_ffn[layer]), peer_wq[layer], peer_subkeys[layer], peer_u[layer], peer_v[layer]).astype(h.dtype)
    return rms_norm(h, norm_final)
```

```python
import functools

import jax
import jax.numpy as jnp
from jax import lax
from jax.experimental import pallas as pl
from jax.experimental.pallas import tpu as pltpu

F32 = jnp.float32
BF16 = jnp.bfloat16
HIGHEST = lax.Precision.HIGHEST

D_MODEL = 1024
EPS = 1e-6
CONV_WIDTH = 4
MEM_LEN = 256
SSD_HEAD_DIM = 64
SSD_HEADS = 16
SSD_GROUPS = 2
SSD_STATE = 128
SSD_XBC = D_MODEL + 2 * SSD_GROUPS * SSD_STATE
SSD_CHUNK = 128
MLSTM_HEADS = 4
MLSTM_HEAD_DIM = 256
MLSTM_CHUNK = 128
HGRN_HEADS = 8
HGRN_HEAD_DIM = 128
HGRN_CHUNK = 64
XATTN_HEADS = 4
XATTN_HEAD_DIM = 256
PEER_HEADS = 8
PEER_N_KEYS = 128
PEER_TOPK = 16
PEER_HALF = 128
PEER_SEL = PEER_HEADS * PEER_TOPK

SUBLANES = 8
LANES = 128
SMALL_W = LANES
SMALL_DT = 0
SMALL_IG = SSD_HEADS
SMALL_FG = SSD_HEADS + MLSTM_HEADS

VMEM_LIMIT = 52 * 1024 * 1024


def _cparams(sem):
    return pltpu.CompilerParams(dimension_semantics=sem, vmem_limit_bytes=VMEM_LIMIT)


def _sigmoid(x):
    return 1.0 / (1.0 + jnp.exp(-x))


def _softplus(x):
    return jnp.maximum(x, 0.0) + jnp.log(1.0 + jnp.exp(-jnp.abs(x)))


def _rms(x, w):
    return x * lax.rsqrt(jnp.mean(x * x, axis=-1, keepdims=True) + EPS) * w


def _dot(a, b):
    return jnp.dot(a.astype(BF16), b.astype(BF16), preferred_element_type=F32)


def _dot_nt(a, b):
    return lax.dot_general(a.astype(BF16), b.astype(BF16), (((1,), (1,)), ((), ())),
                           preferred_element_type=F32)


def _cumsum_rows(x):
    n = x.shape[0]
    r = lax.broadcasted_iota(jnp.int32, (n, n), 0)
    c = lax.broadcasted_iota(jnp.int32, (n, n), 1)
    tril = (r >= c).astype(F32)
    return jnp.dot(tril, x, precision=HIGHEST, preferred_element_type=F32)


def _causal(n):
    r = lax.broadcasted_iota(jnp.int32, (n, n), 0)
    c = lax.broadcasted_iota(jnp.int32, (n, n), 1)
    return r >= c


def _norm_proj_body(x_ref, nw_ref, w_ref, *out_refs, splits):
    xn = _rms(x_ref[...], nw_ref[...]).astype(BF16)
    off = 0
    for o_ref, n in zip(out_refs, splits):
        o_ref[...] = jnp.dot(xn, w_ref[:, off:off + n], preferred_element_type=F32).astype(o_ref.dtype)
        off += n


def norm_proj(x, nw, w_bf16, splits, tm=512, out_dtype=F32):
    t, d = x.shape
    n = w_bf16.shape[1]
    assert sum(splits) == n
    tm = min(tm, t)
    assert t % tm == 0
    return pl.pallas_call(
        functools.partial(_norm_proj_body, splits=tuple(splits)),
        grid=(t // tm,),
        in_specs=[pl.BlockSpec((tm, d), lambda i: (i, 0)),
                  pl.BlockSpec((1, d), lambda i: (0, 0)),
                  pl.BlockSpec((d, n), lambda i: (0, 0))],
        out_specs=[pl.BlockSpec((tm, s), lambda i: (i, 0)) for s in splits],
        out_shape=[jax.ShapeDtypeStruct((t, s), out_dtype) for s in splits],
        compiler_params=_cparams(("parallel",)),
        name="norm_proj",
    )(x, nw.reshape(1, d), w_bf16)


def _proj_residual_body(*refs, n_parts):
    a_refs = refs[:n_parts]
    w_refs = refs[n_parts:2 * n_parts]
    r_ref = refs[2 * n_parts]
    o_ref = refs[2 * n_parts + 1]
    acc = r_ref[...]
    for a_ref, w_ref in zip(a_refs, w_refs):
        acc = acc + jnp.dot(a_ref[...].astype(BF16), w_ref[...], preferred_element_type=F32)
    o_ref[...] = acc


def proj_residual(parts, ws_bf16, res, tm=512):
    t, d = res.shape
    tm = min(tm, t)
    n_parts = len(parts)
    in_specs = ([pl.BlockSpec((tm, p.shape[1]), lambda i: (i, 0)) for p in parts]
                + [pl.BlockSpec(w.shape, lambda i: (0, 0)) for w in ws_bf16]
                + [pl.BlockSpec((tm, d), lambda i: (i, 0))])
    return pl.pallas_call(
        functools.partial(_proj_residual_body, n_parts=n_parts),
        grid=(t // tm,),
        in_specs=in_specs,
        out_specs=pl.BlockSpec((tm, d), lambda i: (i, 0)),
        out_shape=jax.ShapeDtypeStruct((t, d), F32),
        compiler_params=_cparams(("parallel",)),
        name="proj_residual",
    )(*parts, *ws_bf16, res)


def _conv_silu(x_in, xpad_ref, cw_ref, cb_ref, n_rows):
    xpad_ref[SUBLANES:SUBLANES + n_rows, :] = x_in
    acc = cb_ref[...] + cw_ref[CONV_WIDTH - 1:CONV_WIDTH, :] * x_in
    for k in range(CONV_WIDTH - 1):
        start = SUBLANES - (CONV_WIDTH - 1) + k
        acc = acc + cw_ref[k:k + 1, :] * xpad_ref[start:start + n_rows, :]
    xpad_ref[0:SUBLANES, :] = x_in[n_rows - SUBLANES:n_rows, :]
    return acc * _sigmoid(acc)


def _ssd_body(z_ref, xbc_ref, sm_ref, cw_ref, cb_ref, bias_ref, alog_ref, dsk_ref, nw_ref,
              o_ref, xpad_ref, state_ref, ybuf_ref):
    L, P, N = SSD_CHUNK, SSD_HEAD_DIM, SSD_STATE
    heads_per_group = SSD_HEADS // SSD_GROUPS

    @pl.when(pl.program_id(1) == 0)
    def _init():
        xpad_ref[0:SUBLANES, :] = jnp.zeros((SUBLANES, SSD_XBC), F32)
        state_ref[...] = jnp.zeros_like(state_ref)

    xa = _conv_silu(xbc_ref[...], xpad_ref, cw_ref, cb_ref, L)
    dt = _softplus(sm_ref[...] + bias_ref[...])
    a = -jnp.exp(alog_ref[...])
    cs_col = _cumsum_rows(dt * a)
    cs_row = cs_col.T
    causal = _causal(L)

    for g in range(SSD_GROUPS):
        b_g = xa[:, D_MODEL + g * N:D_MODEL + (g + 1) * N]
        c_g = xa[:, D_MODEL + (SSD_GROUPS + g) * N:D_MODEL + (SSD_GROUPS + g + 1) * N]
        cb = _dot_nt(c_g, b_g)
        b_gt = b_g.T.astype(BF16)
        c_gb = c_g.astype(BF16)
        for e in range(heads_per_group):
            he = g * heads_per_group + e
            col = SMALL_DT + he
            cs_c = cs_col[:, col:col + 1]
            cs_r = cs_row[col:col + 1, :]
            cs_last = cs_col[L - 1:L, col:col + 1]
            decay = jnp.exp(jnp.where(causal, cs_c - cs_r, -jnp.inf))
            x_e = xa[:, he * P:(he + 1) * P]
            x_dt = x_e * dt[:, col:col + 1]
            y = _dot(cb * decay, x_dt)
            s_prev = state_ref[he]
            y = y + jnp.dot(c_gb, s_prev.astype(BF16), preferred_element_type=F32) * jnp.exp(cs_c)
            y = y + x_e * dsk_ref[:, col:col + 1]
            ybuf_ref[:, he * P:(he + 1) * P] = y
            x_end = (x_dt * jnp.exp(cs_last - cs_c)).astype(BF16)
            state_ref[he] = jnp.exp(cs_last) * s_prev + jnp.dot(b_gt, x_end, preferred_element_type=F32)

    zz = z_ref[...]
    y = ybuf_ref[...] * (zz * _sigmoid(zz))
    gw = D_MODEL // SSD_GROUPS
    for g in range(SSD_GROUPS):
        o_ref[:, g * gw:(g + 1) * gw] = _rms(y[:, g * gw:(g + 1) * gw], nw_ref[:, g * gw:(g + 1) * gw])


def ssd_mixer(z, xbc, small, conv_w, conv_b, bias_pad, alog_pad, dskip_pad, norm_w, bsz):
    t = z.shape[0]
    L = SSD_CHUNK
    nc = t // bsz // L
    row = lambda b, c: (b * nc + c, 0)
    const = lambda b, c: (0, 0)
    return pl.pallas_call(
        _ssd_body,
        grid=(bsz, nc),
        in_specs=[pl.BlockSpec((L, D_MODEL), row),
                  pl.BlockSpec((L, SSD_XBC), row),
                  pl.BlockSpec((L, SMALL_W), row),
                  pl.BlockSpec((CONV_WIDTH, SSD_XBC), const),
                  pl.BlockSpec((1, SSD_XBC), const),
                  pl.BlockSpec((1, SMALL_W), const),
                  pl.BlockSpec((1, SMALL_W), const),
                  pl.BlockSpec((1, SMALL_W), const),
                  pl.BlockSpec((1, D_MODEL), const)],
        out_specs=pl.BlockSpec((L, D_MODEL), row),
        out_shape=jax.ShapeDtypeStruct((t, D_MODEL), F32),
        scratch_shapes=[pltpu.VMEM((L + SUBLANES, SSD_XBC), F32),
                        pltpu.VMEM((SSD_HEADS, SSD_STATE, SSD_HEAD_DIM), F32),
                        pltpu.VMEM((L, D_MODEL), F32)],
        compiler_params=_cparams(("parallel", "arbitrary")),
        name="ssd_mixer",
    )(z, xbc, small, conv_w, conv_b.reshape(1, -1), bias_pad, alog_pad, dskip_pad, norm_w.reshape(1, -1))


def _mlstm_body(xm_ref, og_ref, sm_ref, cw_ref, cb_ref, bias_ref, wq_ref, wk_ref, wv_ref, nw_ref,
                o_ref, xpad_ref, c_ref, n_ref, m_ref):
    L, H, Dh = MLSTM_CHUNK, MLSTM_HEADS, MLSTM_HEAD_DIM

    @pl.when(pl.program_id(1) == 0)
    def _init():
        xpad_ref[0:SUBLANES, :] = jnp.zeros((SUBLANES, D_MODEL), F32)
        c_ref[...] = jnp.zeros_like(c_ref)
        n_ref[...] = jnp.zeros_like(n_ref)
        m_ref[...] = jnp.zeros_like(m_ref)

    xm = xm_ref[...]
    xc = _conv_silu(xm, xpad_ref, cw_ref, cb_ref, L)
    smb = sm_ref[...] + bias_ref[...]
    logf = jnp.minimum(smb, 0.0) - jnp.log(1.0 + jnp.exp(-jnp.abs(smb)))
    bcum_col = _cumsum_rows(logf)
    bcum_row = bcum_col.T
    smb_row = smb.T
    causal = _causal(L)

    for h in range(H):
        ci, cf = SMALL_IG + h, SMALL_FG + h
        bc = bcum_col[:, cf:cf + 1]
        br = bcum_row[cf:cf + 1, :]
        ic = smb[:, ci:ci + 1]
        ir = smb_row[ci:ci + 1, :]
        b_end = bcum_col[L - 1:L, cf:cf + 1]
        dlog = jnp.where(causal, bc - br + ir, -jnp.inf)
        m_intra = jnp.max(dlog, axis=1, keepdims=True)
        g_end = b_end - bc + ic
        m_chunk = jnp.max(g_end, axis=0, keepdims=True)
        w_end = jnp.exp(g_end - m_chunk)

        sl = slice(h * Dh, (h + 1) * Dh)
        q = _dot(xc[:, sl], wq_ref[h])
        k = _dot(xc[:, sl], wk_ref[h]) * (Dh ** -0.5)
        v = _dot(xm[:, sl], wv_ref[h])
        qb, kb, vb = q.astype(BF16), k.astype(BF16), v.astype(BF16)

        m_prev = m_ref[h:h + 1, 0:1]
        c_prev = c_ref[h]
        n_prev = n_ref[h:h + 1, :]
        m_t = jnp.maximum(bc + m_prev, m_intra)
        a_t = jnp.exp(bc + m_prev - m_t)
        qk = _dot_nt(qb, kb) * jnp.exp(dlog - m_t)
        num = a_t * jnp.dot(qb, c_prev.astype(BF16), preferred_element_type=F32) \
            + jnp.dot(qk.astype(BF16), vb, preferred_element_type=F32)
        den = a_t * jnp.sum(q * n_prev, axis=1, keepdims=True) + jnp.sum(qk, axis=1, keepdims=True)
        hid = num / jnp.maximum(jnp.abs(den), jnp.exp(-m_t))
        hid = _rms(hid, nw_ref[:, sl])
        og = og_ref[:, sl]
        o_ref[:, sl] = hid * _sigmoid(og)

        m_new = jnp.maximum(b_end + m_prev, m_chunk)
        a_old = jnp.exp(b_end + m_prev - m_new)
        a_new = jnp.exp(m_chunk - m_new)
        kw = k * w_end
        c_local = jnp.dot(kw.T.astype(BF16), vb, preferred_element_type=F32)
        c_ref[h] = a_old * c_prev + a_new * c_local
        n_ref[h:h + 1, :] = a_old * n_prev + a_new * jnp.sum(kw, axis=0, keepdims=True)
        m_ref[h:h + 1, :] = jnp.broadcast_to(m_new, (1, LANES))


def mlstm_mixer(xm, og, small, conv_w, conv_b, bias_pad, wq, wk, wv, norm_w, bsz):
    t = xm.shape[0]
    L = MLSTM_CHUNK
    nc = t // bsz // L
    row = lambda b, c: (b * nc + c, 0)
    const = lambda b, c: (0, 0)
    const3 = lambda b, c: (0, 0, 0)
    wspec = pl.BlockSpec((MLSTM_HEADS, MLSTM_HEAD_DIM, MLSTM_HEAD_DIM), const3)
    return pl.pallas_call(
        _mlstm_body,
        grid=(bsz, nc),
        in_specs=[pl.BlockSpec((L, D_MODEL), row),
                  pl.BlockSpec((L, D_MODEL), row),
                  pl.BlockSpec((L, SMALL_W), row),
                  pl.BlockSpec((CONV_WIDTH, D_MODEL), const),
                  pl.BlockSpec((1, D_MODEL), const),
                  pl.BlockSpec((1, SMALL_W), const),
                  wspec, wspec, wspec,
                  pl.BlockSpec((1, D_MODEL), const)],
        out_specs=pl.BlockSpec((L, D_MODEL), row),
        out_shape=jax.ShapeDtypeStruct((t, D_MODEL), F32),
        scratch_shapes=[pltpu.VMEM((L + SUBLANES, D_MODEL), F32),
                        pltpu.VMEM((MLSTM_HEADS, MLSTM_HEAD_DIM, MLSTM_HEAD_DIM), F32),
                        pltpu.VMEM((SUBLANES, MLSTM_HEAD_DIM), F32),
                        pltpu.VMEM((SUBLANES, LANES), F32)],
        compiler_params=_cparams(("parallel", "arbitrary")),
        name="mlstm_mixer",
    )(xm, og, small, conv_w, conv_b.reshape(1, -1), bias_pad, wq, wk, wv, norm_w.reshape(1, -1))


def _hgrn_body(uq_ref, uf_ref, ui_ref, ug_ref, lbl_ref, nw_ref, o_ref, s_ref):
    L, H, Dk = HGRN_CHUNK, HGRN_HEADS, HGRN_HEAD_DIM

    @pl.when(pl.program_id(1) == 0)
    def _init():
        s_ref[...] = jnp.zeros_like(s_ref)

    lg = lbl_ref[...]
    lmax = jnp.max(lg, axis=0, keepdims=True)
    le = jnp.exp(lg - lmax)
    lb = le[0:1, :] / jnp.sum(le, axis=0, keepdims=True)

    uf = uf_ref[...]
    uq = uq_ref[...]
    logf = jnp.log(lb + (1.0 - lb) * _sigmoid(uf))
    k = (1.0 - lb) * _sigmoid(-uf)
    q = uq * _sigmoid(uq)
    v = ui_ref[...]
    gc = _cumsum_rows(logf)
    g_ref = gc[L // 2:L // 2 + 1, :]
    g_last = gc[L - 1:L, :]
    q_in = (q * jnp.exp(gc - g_ref)).astype(BF16)
    k_in = (k * jnp.exp(g_ref - gc)).astype(BF16)
    q_off = (q * jnp.exp(gc)).astype(BF16)
    k_end = k * jnp.exp(g_last - gc)
    decay = jnp.exp(g_last)
    causal = _causal(L)
    vb = v.astype(BF16)
    ug = ug_ref[...]

    for h in range(H):
        sl = slice(h * Dk, (h + 1) * Dk)
        att = jnp.where(causal, _dot_nt(q_in[:, sl], k_in[:, sl]), 0.0)
        st_prev = s_ref[h]
        o = jnp.dot(att.astype(BF16), vb[:, sl], preferred_element_type=F32) \
            + _dot_nt(q_off[:, sl], st_prev)
        s_ref[h] = decay[:, sl] * st_prev + _dot(v[:, sl].T, k_end[:, sl])
        g = ug[:, sl]
        o_ref[:, sl] = _rms(o, nw_ref[:, sl]) * (g * _sigmoid(g))


def hgrn_mixer(uq, uf, ui, ug, lb_logits, norm_w, bsz):
    t = uq.shape[0]
    L = HGRN_CHUNK
    nc = t // bsz // L
    row = lambda b, c: (b * nc + c, 0)
    const = lambda b, c: (0, 0)
    rs = pl.BlockSpec((L, D_MODEL), row)
    return pl.pallas_call(
        _hgrn_body,
        grid=(bsz, nc),
        in_specs=[rs, rs, rs, rs,
                  pl.BlockSpec((2, D_MODEL), const),
                  pl.BlockSpec((1, D_MODEL), const)],
        out_specs=rs,
        out_shape=jax.ShapeDtypeStruct((t, D_MODEL), F32),
        scratch_shapes=[pltpu.VMEM((HGRN_HEADS, HGRN_HEAD_DIM, HGRN_HEAD_DIM), F32)],
        compiler_params=_cparams(("parallel", "arbitrary")),
        name="hgrn_mixer",
    )(uq, uf, ui, ug, lb_logits, norm_w.reshape(1, -1))


def _xattn_body(h_ref, nw_ref, wq_ref, k_ref, v_ref, wo_ref, o_ref, obuf_ref):
    hv = h_ref[...]
    hn = _rms(hv, nw_ref[...]).astype(BF16)
    q = jnp.dot(hn, wq_ref[...], preferred_element_type=F32)
    scale = XATTN_HEAD_DIM ** -0.5
    for a in range(XATTN_HEADS):
        sl = slice(a * XATTN_HEAD_DIM, (a + 1) * XATTN_HEAD_DIM)
        s = _dot_nt(q[:, sl], k_ref[:, sl]) * scale
        s = s - jnp.max(s, axis=1, keepdims=True)
        p = jnp.exp(s)
        p = p / jnp.sum(p, axis=1, keepdims=True)
        obuf_ref[:, sl] = jnp.dot(p.astype(BF16), v_ref[:, sl], preferred_element_type=F32).astype(BF16)
    o_ref[...] = hv + jnp.dot(obuf_ref[...], wo_ref[...], preferred_element_type=F32)


def xattn_residual(h, nw, wq_bf16, k_bf16, v_bf16, wo_bf16, bsz, tm=512):
    t, d = h.shape
    seq = t // bsz
    tm = min(tm, seq)
    nt = seq // tm
    row = lambda b, i: (b * nt + i, 0)
    const = lambda b, i: (0, 0)
    mem = lambda b, i: (b, 0)
    return pl.pallas_call(
        _xattn_body,
        grid=(bsz, nt),
        in_specs=[pl.BlockSpec((tm, d), row),
                  pl.BlockSpec((1, d), const),
                  pl.BlockSpec((d, d), const),
                  pl.BlockSpec((MEM_LEN, d), mem),
                  pl.BlockSpec((MEM_LEN, d), mem),
                  pl.BlockSpec((d, d), const)],
        out_specs=pl.BlockSpec((tm, d), row),
        out_shape=jax.ShapeDtypeStruct((t, d), F32),
        scratch_shapes=[pltpu.VMEM((tm, d), BF16)],
        compiler_params=_cparams(("parallel", "parallel")),
        name="xattn_residual",
    )(h, nw.reshape(1, d), wq_bf16, k_bf16, v_bf16, wo_bf16)


def _topk_rows(s, k):
    rows = s.shape[0]
    iota = lax.broadcasted_iota(jnp.int32, s.shape, 0)
    vals, idxs = [], []
    for _ in range(k):
        m = jnp.max(s, axis=0, keepdims=True)
        i = jnp.min(jnp.where(s == m, iota, rows), axis=0, keepdims=True)
        vals.append(m)
        idxs.append(i)
        s = jnp.where(iota == i, -jnp.inf, s)
    return jnp.concatenate(vals, axis=0), jnp.concatenate(idxs, axis=0)


def _select_rows(table, sel, n):
    out = jnp.zeros(sel.shape, jnp.int32)
    for a in range(n):
        out = jnp.where(sel == a, table[a:a + 1, :], out)
    return out


def _peer_route_body(h_ref, nw_ref, wq_ref, sk_ref, hn_ref, idx_ref, gate_ref,
                     hnb_ref, st_ref, oi_ref, og_ref, *, n_col_blocks):
    K = PEER_TOPK

    @pl.when(pl.program_id(1) == 0)
    def _norm():
        xn = _rms(h_ref[...], nw_ref[...])
        hn_ref[...] = xn
        hnb_ref[...] = xn.astype(BF16)

    q = jnp.dot(hnb_ref[...], wq_ref[...], preferred_element_type=F32)
    for p in range(2):
        qp = q[:, p * PEER_HALF:(p + 1) * PEER_HALF]
        st = _dot_nt(sk_ref[0, p], qp)
        for cb in range(n_col_blocks):
            st_ref[p * n_col_blocks + cb] = st[:, cb * LANES:(cb + 1) * LANES]

    def col_block(cb, carry):
        v1, i1 = _topk_rows(st_ref[cb], K)
        v2, i2 = _topk_rows(st_ref[n_col_blocks + cb], K)
        cand = jnp.concatenate([v1[a:a + 1, :] + v2 for a in range(K)], axis=0)
        top_s, pos = _topk_rows(cand, K)
        e1 = _select_rows(i1, lax.shift_right_logical(pos, 4), K)
        e2 = _select_rows(i2, lax.bitwise_and(pos, K - 1), K)
        oi_ref[cb] = e1 * PEER_N_KEYS + e2
        ex = jnp.exp(top_s - jnp.max(top_s, axis=0, keepdims=True))
        og_ref[cb] = ex / jnp.sum(ex, axis=0, keepdims=True)
        return carry

    lax.fori_loop(0, n_col_blocks, col_block, 0)
    for cb in range(n_col_blocks):
        idx_ref[0, :, cb * LANES:(cb + 1) * LANES] = oi_ref[cb]
        gate_ref[0, :, cb * LANES:(cb + 1) * LANES] = og_ref[cb]


def peer_route(h, nw, wq_bf16, subkeys_bf16, tm=512):
    t, d = h.shape
    tm = min(tm, t)
    ncb = tm // LANES
    hq = 2 * PEER_HALF
    return pl.pallas_call(
        functools.partial(_peer_route_body, n_col_blocks=ncb),
        grid=(t // tm, PEER_HEADS),
        in_specs=[pl.BlockSpec((tm, d), lambda i, hd: (i, 0)),
                  pl.BlockSpec((1, d), lambda i, hd: (0, 0)),
                  pl.BlockSpec((d, hq), lambda i, hd: (0, hd)),
                  pl.BlockSpec((1, 2, PEER_N_KEYS, PEER_HALF), lambda i, hd: (hd, 0, 0, 0))],
        out_specs=[pl.BlockSpec((tm, d), lambda i, hd: (i, 0)),
                   pl.BlockSpec((1, PEER_TOPK, tm), lambda i, hd: (hd, 0, i)),
                   pl.BlockSpec((1, PEER_TOPK, tm), lambda i, hd: (hd, 0, i))],
        out_shape=[jax.ShapeDtypeStruct((t, d), F32),
                   jax.ShapeDtypeStruct((PEER_HEADS, PEER_TOPK, t), jnp.int32),
                   jax.ShapeDtypeStruct((PEER_HEADS, PEER_TOPK, t), F32)],
        scratch_shapes=[pltpu.VMEM((tm, d), BF16),
                        pltpu.VMEM((2 * ncb, PEER_N_KEYS, LANES), F32),
                        pltpu.VMEM((ncb, PEER_TOPK, LANES), jnp.int32),
                        pltpu.VMEM((ncb, PEER_TOPK, LANES), F32)],
        compiler_params=_cparams(("parallel", "arbitrary")),
        name="peer_route",
    )(h, nw.reshape(1, d), wq_bf16, subkeys_bf16)


PEER_TOK_BLOCK = 8
PEER_ISSUE_UNROLL = 8


def _peer_expert_body(idx_ref, x_ref, gate_ref, h_ref, u_hbm, v_hbm, o_ref, ubuf, vbuf, sem):
    i = pl.program_id(0)
    n_blocks = pl.num_programs(0) - 1
    tb = PEER_TOK_BLOCK

    @pl.when(i < n_blocks)
    def _issue():
        slot = i % 2

        def tok(tk, carry):
            def rows(jo, carry2):
                for ji in range(PEER_ISSUE_UNROLL):
                    j = jo * PEER_ISSUE_UNROLL + ji
                    e = idx_ref[tk, j]
                    pltpu.make_async_copy(u_hbm.at[pl.ds(e, 1), :], ubuf.at[slot, tk, pl.ds(j, 1), :],
                                          sem.at[0, slot]).start()
                    pltpu.make_async_copy(v_hbm.at[pl.ds(e, 1), :], vbuf.at[slot, tk, pl.ds(j, 1), :],
                                          sem.at[1, slot]).start()
                return carry2
            return lax.fori_loop(0, PEER_SEL // PEER_ISSUE_UNROLL, rows, carry)

        lax.fori_loop(0, tb, tok, 0)

    @pl.when(i > 0)
    def _compute():
        slot = (i - 1) % 2
        pltpu.make_async_copy(ubuf.at[slot], ubuf.at[slot], sem.at[0, slot]).wait()
        pltpu.make_async_copy(vbuf.at[slot], vbuf.at[slot], sem.at[1, slot]).wait()
        for tk in range(tb):
            x = x_ref[tk:tk + 1, :]
            hsel = jnp.sum(ubuf[slot, tk] * x, axis=1, keepdims=True)
            act = 0.5 * hsel * (1.0 + lax.erf(hsel * (2.0 ** -0.5)))
            w = gate_ref[:, tk:tk + 1] * act
            o_ref[tk:tk + 1, :] = h_ref[tk:tk + 1, :] + jnp.sum(vbuf[slot, tk] * w, axis=0, keepdims=True)


def peer_expert_residual(idx_tok, hn, gate_sel, h, u_tab, v_tab):
    t, d = h.shape
    tb = PEER_TOK_BLOCK
    n = t // tb
    prev = lambda i: (jnp.maximum(i - 1, 0), 0)
    gate_blocks = gate_sel.reshape(PEER_SEL, n, tb).transpose(1, 0, 2)
    return pl.pallas_call(
        _peer_expert_body,
        grid=(n + 1,),
        in_specs=[pl.BlockSpec((tb, PEER_SEL), lambda i: (jnp.minimum(i, n - 1), 0),
                               memory_space=pltpu.SMEM),
                  pl.BlockSpec((tb, d), prev),
                  pl.BlockSpec((None, PEER_SEL, tb), lambda i: (jnp.maximum(i - 1, 0), 0, 0)),
                  pl.BlockSpec((tb, d), prev),
                  pl.BlockSpec(memory_space=pl.ANY),
                  pl.BlockSpec(memory_space=pl.ANY)],
        out_specs=pl.BlockSpec((tb, d), prev),
        out_shape=jax.ShapeDtypeStruct((t, d), F32),
        scratch_shapes=[pltpu.VMEM((2, tb, PEER_SEL, d), F32),
                        pltpu.VMEM((2, tb, PEER_SEL, d), F32),
                        pltpu.SemaphoreType.DMA((2, 2))],
        compiler_params=_cparams(("arbitrary",)),
        name="peer_expert",
    )(idx_tok, hn, gate_blocks, h, u_tab, v_tab)


def _rmsnorm_body(x_ref, w_ref, o_ref):
    o_ref[...] = _rms(x_ref[...], w_ref[...])


def rmsnorm(x, w, tm=1024):
    t, d = x.shape
    tm = min(tm, t)
    return pl.pallas_call(
        _rmsnorm_body,
        grid=(t // tm,),
        in_specs=[pl.BlockSpec((tm, d), lambda i: (i, 0)), pl.BlockSpec((1, d), lambda i: (0, 0))],
        out_specs=pl.BlockSpec((tm, d), lambda i: (i, 0)),
        out_shape=jax.ShapeDtypeStruct((t, d), F32),
        compiler_params=_cparams(("parallel",)),
        name="rmsnorm",
    )(x, w.reshape(1, d))


def _pad_cols(v, start, width=SMALL_W):
    out = jnp.zeros((1, width), F32)
    return out.at[0, start:start + v.shape[0]].set(v.astype(F32))


def _even_mix(h, bsz, norm_w, w_in, w_out, ssd_conv_w, ssd_conv_b, ssd_dt_bias, ssd_a_log, ssd_d_skip,
              ssd_norm, ml_conv_w, ml_conv_b, ml_wq, ml_wk, ml_wv, ml_i_bias, ml_f_bias, ml_norm):
    d = D_MODEL
    o1 = d
    o2 = o1 + SSD_XBC
    o3 = o2 + SSD_HEADS
    o4 = o3 + d
    o5 = o4 + d
    o6 = o5 + MLSTM_HEADS
    o7 = o6 + MLSTM_HEADS
    w_small = jnp.concatenate([w_in[:, o2:o3], w_in[:, o5:o6], w_in[:, o6:o7],
                               jnp.zeros((d, SMALL_W - (o3 - o2) - (o7 - o5)), w_in.dtype)], axis=1)
    w_all = jnp.concatenate([w_in[:, :o1], w_in[:, o1:o2], w_in[:, o3:o4], w_in[:, o4:o5], w_small],
                            axis=1).astype(BF16)
    z, xbc, xm, og, small = norm_proj(h, norm_w, w_all, (d, SSD_XBC, d, d, SMALL_W), tm=256)
    bias_pad = (_pad_cols(ssd_dt_bias, SMALL_DT) + _pad_cols(ml_i_bias, SMALL_IG)
                + _pad_cols(ml_f_bias, SMALL_FG))
    y_a = ssd_mixer(z, xbc, small, ssd_conv_w, ssd_conv_b, bias_pad, _pad_cols(ssd_a_log, SMALL_DT),
                    _pad_cols(ssd_d_skip, SMALL_DT), ssd_norm, bsz)
    y_b = mlstm_mixer(xm, og, small, ml_conv_w, ml_conv_b, bias_pad, ml_wq.astype(BF16),
                      ml_wk.astype(BF16), ml_wv.astype(BF16), ml_norm, bsz)
    w_out_b = w_out.astype(BF16)
    return proj_residual([y_a, y_b], [w_out_b[:d], w_out_b[d:]], h)


def _odd_mix(h, bsz, norm_w, w_in, w_out, lb_logits, hg_norm):
    d = D_MODEL
    uq, uf, ui, ug = norm_proj(h, norm_w, w_in.astype(BF16), (d, d, d, d), tm=256)
    y = hgrn_mixer(uq, uf, ui, ug, lb_logits, hg_norm, bsz)
    return proj_residual([y], [w_out.astype(BF16)], h)


def _xattn(h, mem2d, bsz, norm_x, norm_m, wq, wkv, wo):
    k, v = norm_proj(mem2d, norm_m, wkv.astype(BF16), (D_MODEL, D_MODEL), out_dtype=BF16)
    return xattn_residual(h, norm_x, wq.astype(BF16), k, v, wo.astype(BF16), bsz)


def _peer(h, norm_w, wq, subkeys, u_tab, v_tab):
    t = h.shape[0]
    hn, idx, gate = peer_route(h, norm_w, wq.astype(BF16), subkeys.astype(BF16))
    idx_tok = idx.reshape(PEER_SEL, t).T
    return peer_expert_residual(idx_tok, hn, gate.reshape(PEER_SEL, t), h, u_tab, v_tab)


def kernel(x, mem, norm_mix, norm_xattn, norm_mem, norm_ffn, norm_final, ev_w_in, ev_w_out, ssd_conv_w, ssd_conv_b, ssd_dt_bias, ssd_a_log, ssd_d_skip, ssd_norm, ml_conv_w, ml_conv_b, ml_wq, ml_wk, ml_wv, ml_i_bias, ml_f_bias, ml_norm, od_w_in, od_w_out, hgrn_lb_logits, hgrn_norm, xa_wq, xa_wkv, xa_wo, peer_wq, peer_subkeys, peer_u, peer_v):
    bsz, seq, d = x.shape
    depth = norm_mix.shape[0]
    assert depth == 2 and hgrn_lb_logits.shape[0] == 2
    h = x.reshape(bsz * seq, d)
    mem2d = mem.reshape(bsz * MEM_LEN, d)
    for layer in range(depth):
        if layer % 2 == 0:
            e = layer // 2
            h = _even_mix(h, bsz, norm_mix[layer], ev_w_in[e], ev_w_out[e], ssd_conv_w[e], ssd_conv_b[e],
                          ssd_dt_bias[e], ssd_a_log[e], ssd_d_skip[e], ssd_norm[e], ml_conv_w[e],
                          ml_conv_b[e], ml_wq[e], ml_wk[e], ml_wv[e], ml_i_bias[e], ml_f_bias[e], ml_norm[e])
        else:
            o = layer // 2
            h = _odd_mix(h, bsz, norm_mix[layer], od_w_in[o], od_w_out[o], hgrn_lb_logits, hgrn_norm[o])
        h = _xattn(h, mem2d, bsz, norm_xattn[layer], norm_mem[layer], xa_wq[layer], xa_wkv[layer], xa_wo[layer])
        h = _peer(h, norm_ffn[layer], peer_wq[layer], peer_subkeys[layer], peer_u[layer], peer_v[layer])
    return rmsnorm(h, norm_final).reshape(bsz, seq, d)
```

```python
import functools

import jax
import jax.numpy as jnp
from jax import lax
from jax.experimental import pallas as pl
from jax.experimental.pallas import tpu as pltpu

F32 = jnp.float32
BF16 = jnp.bfloat16
HIGHEST = lax.Precision.HIGHEST

D_MODEL = 1024
EPS = 1e-6
CONV_WIDTH = 4
MEM_LEN = 256
SSD_HEAD_DIM = 64
SSD_HEADS = 16
SSD_GROUPS = 2
SSD_STATE = 128
SSD_XBC = D_MODEL + 2 * SSD_GROUPS * SSD_STATE
SSD_CHUNK = 128
MLSTM_HEADS = 4
MLSTM_HEAD_DIM = 256
MLSTM_CHUNK = 128
HGRN_HEADS = 8
HGRN_HEAD_DIM = 128
HGRN_CHUNK = 64
XATTN_HEADS = 4
XATTN_HEAD_DIM = 256
PEER_HEADS = 8
PEER_N_KEYS = 128
PEER_TOPK = 16
PEER_HALF = 128
PEER_SEL = PEER_HEADS * PEER_TOPK

SUBLANES = 8
LANES = 128
SMALL_W = LANES
SMALL_DT = 0
SMALL_IG = SSD_HEADS
SMALL_FG = SSD_HEADS + MLSTM_HEADS

VMEM_LIMIT = 52 * 1024 * 1024


def _cparams(sem):
    return pltpu.CompilerParams(dimension_semantics=sem, vmem_limit_bytes=VMEM_LIMIT)


def _sigmoid(x):
    return 1.0 / (1.0 + jnp.exp(-x))


def _softplus(x):
    return jnp.maximum(x, 0.0) + jnp.log(1.0 + jnp.exp(-jnp.abs(x)))


def _rms(x, w):
    return x * lax.rsqrt(jnp.mean(x * x, axis=-1, keepdims=True) + EPS) * w


def _dot(a, b):
    return jnp.dot(a.astype(BF16), b.astype(BF16), preferred_element_type=F32)


def _dot_nt(a, b):
    return lax.dot_general(a.astype(BF16), b.astype(BF16), (((1,), (1,)), ((), ())),
                           preferred_element_type=F32)


def _cumsum_rows(x):
    n = x.shape[0]
    r = lax.broadcasted_iota(jnp.int32, (n, n), 0)
    c = lax.broadcasted_iota(jnp.int32, (n, n), 1)
    tril = (r >= c).astype(F32)
    return jnp.dot(tril, x, precision=HIGHEST, preferred_element_type=F32)


def _causal(n):
    r = lax.broadcasted_iota(jnp.int32, (n, n), 0)
    c = lax.broadcasted_iota(jnp.int32, (n, n), 1)
    return r >= c


def _norm_proj_body(x_ref, nw_ref, w_ref, *out_refs, splits):
    xn = _rms(x_ref[...], nw_ref[...]).astype(BF16)
    off = 0
    for o_ref, n in zip(out_refs, splits):
        o_ref[...] = jnp.dot(xn, w_ref[:, off:off + n], preferred_element_type=F32).astype(o_ref.dtype)
        off += n


def norm_proj(x, nw, w_bf16, splits, tm=512, out_dtype=F32):
    t, d = x.shape
    n = w_bf16.shape[1]
    assert sum(splits) == n
    tm = min(tm, t)
    assert t % tm == 0
    return pl.pallas_call(
        functools.partial(_norm_proj_body, splits=tuple(splits)),
        grid=(t // tm,),
        in_specs=[pl.BlockSpec((tm, d), lambda i: (i, 0)),
                  pl.BlockSpec((1, d), lambda i: (0, 0)),
                  pl.BlockSpec((d, n), lambda i: (0, 0))],
        out_specs=[pl.BlockSpec((tm, s), lambda i: (i, 0)) for s in splits],
        out_shape=[jax.ShapeDtypeStruct((t, s), out_dtype) for s in splits],
        compiler_params=_cparams(("parallel",)),
        name="norm_proj",
    )(x, nw.reshape(1, d), w_bf16)


def _proj_residual_body(*refs, n_parts):
    a_refs = refs[:n_parts]
    w_refs = refs[n_parts:2 * n_parts]
    r_ref = refs[2 * n_parts]
    o_ref = refs[2 * n_parts + 1]
    acc = r_ref[...]
    for a_ref, w_ref in zip(a_refs, w_refs):
        acc = acc + jnp.dot(a_ref[...].astype(BF16), w_ref[...], preferred_element_type=F32)
    o_ref[...] = acc


def proj_residual(parts, ws_bf16, res, tm=512):
    t, d = res.shape
    tm = min(tm, t)
    n_parts = len(parts)
    in_specs = ([pl.BlockSpec((tm, p.shape[1]), lambda i: (i, 0)) for p in parts]
                + [pl.BlockSpec(w.shape, lambda i: (0, 0)) for w in ws_bf16]
                + [pl.BlockSpec((tm, d), lambda i: (i, 0))])
    return pl.pallas_call(
        functools.partial(_proj_residual_body, n_parts=n_parts),
        grid=(t // tm,),
        in_specs=in_specs,
        out_specs=pl.BlockSpec((tm, d), lambda i: (i, 0)),
        out_shape=jax.ShapeDtypeStruct((t, d), F32),
        compiler_params=_cparams(("parallel",)),
        name="proj_residual",
    )(*parts, *ws_bf16, res)


def _conv_silu(x_in, xpad_ref, cw_ref, cb_ref, n_rows):
    xpad_ref[SUBLANES:SUBLANES + n_rows, :] = x_in
    acc = cb_ref[...] + cw_ref[CONV_WIDTH - 1:CONV_WIDTH, :] * x_in
    for k in range(CONV_WIDTH - 1):
        start = SUBLANES - (CONV_WIDTH - 1) + k
        acc = acc + cw_ref[k:k + 1, :] * xpad_ref[start:start + n_rows, :]
    xpad_ref[0:SUBLANES, :] = x_in[n_rows - SUBLANES:n_rows, :]
    return acc * _sigmoid(acc)


def _ssd_body(z_ref, xbc_ref, sm_ref, cw_ref, cb_ref, bias_ref, alog_ref, dsk_ref, nw_ref,
              o_ref, xpad_ref, state_ref, ybuf_ref):
    L, P, N = SSD_CHUNK, SSD_HEAD_DIM, SSD_STATE
    heads_per_group = SSD_HEADS // SSD_GROUPS

    @pl.when(pl.program_id(1) == 0)
    def _init():
        xpad_ref[0:SUBLANES, :] = jnp.zeros((SUBLANES, SSD_XBC), F32)
        state_ref[...] = jnp.zeros_like(state_ref)

    xa = _conv_silu(xbc_ref[...], xpad_ref, cw_ref, cb_ref, L)
    dt = _softplus(sm_ref[...] + bias_ref[...])
    a = -jnp.exp(alog_ref[...])
    cs_col = _cumsum_rows(dt * a)
    cs_row = cs_col.T
    causal = _causal(L)

    for g in range(SSD_GROUPS):
        b_g = xa[:, D_MODEL + g * N:D_MODEL + (g + 1) * N]
        c_g = xa[:, D_MODEL + (SSD_GROUPS + g) * N:D_MODEL + (SSD_GROUPS + g + 1) * N]
        cb = _dot_nt(c_g, b_g)
        b_gt = b_g.T.astype(BF16)
        c_gb = c_g.astype(BF16)
        for e in range(heads_per_group):
            he = g * heads_per_group + e
            col = SMALL_DT + he
            cs_c = cs_col[:, col:col + 1]
            cs_r = cs_row[col:col + 1, :]
            cs_last = cs_col[L - 1:L, col:col + 1]
            decay = jnp.exp(jnp.where(causal, cs_c - cs_r, -jnp.inf))
            x_e = xa[:, he * P:(he + 1) * P]
            x_dt = x_e * dt[:, col:col + 1]
            y = _dot(cb * decay, x_dt)
            s_prev = state_ref[he]
            y = y + jnp.dot(c_gb, s_prev.astype(BF16), preferred_element_type=F32) * jnp.exp(cs_c)
            y = y + x_e * dsk_ref[:, col:col + 1]
            ybuf_ref[:, he * P:(he + 1) * P] = y
            x_end = (x_dt * jnp.exp(cs_last - cs_c)).astype(BF16)
            state_ref[he] = jnp.exp(cs_last) * s_prev + jnp.dot(b_gt, x_end, preferred_element_type=F32)

    zz = z_ref[...]
    y = ybuf_ref[...] * (zz * _sigmoid(zz))
    gw = D_MODEL // SSD_GROUPS
    for g in range(SSD_GROUPS):
        o_ref[:, g * gw:(g + 1) * gw] = _rms(y[:, g * gw:(g + 1) * gw], nw_ref[:, g * gw:(g + 1) * gw])


def ssd_mixer(z, xbc, small, conv_w, conv_b, bias_pad, alog_pad, dskip_pad, norm_w, bsz):
    t = z.shape[0]
    L = SSD_CHUNK
    nc = t // bsz // L
    row = lambda b, c: (b * nc + c, 0)
    const = lambda b, c: (0, 0)
    return pl.pallas_call(
        _ssd_body,
        grid=(bsz, nc),
        in_specs=[pl.BlockSpec((L, D_MODEL), row),
                  pl.BlockSpec((L, SSD_XBC), row),
                  pl.BlockSpec((L, SMALL_W), row),
                  pl.BlockSpec((CONV_WIDTH, SSD_XBC), const),
                  pl.BlockSpec((1, SSD_XBC), const),
                  pl.BlockSpec((1, SMALL_W), const),
                  pl.BlockSpec((1, SMALL_W), const),
                  pl.BlockSpec((1, SMALL_W), const),
                  pl.BlockSpec((1, D_MODEL), const)],
        out_specs=pl.BlockSpec((L, D_MODEL), row),
        out_shape=jax.ShapeDtypeStruct((t, D_MODEL), F32),
        scratch_shapes=[pltpu.VMEM((L + SUBLANES, SSD_XBC), F32),
                        pltpu.VMEM((SSD_HEADS, SSD_STATE, SSD_HEAD_DIM), F32),
                        pltpu.VMEM((L, D_MODEL), F32)],
        compiler_params=_cparams(("parallel", "arbitrary")),
        name="ssd_mixer",
    )(z, xbc, small, conv_w, conv_b.reshape(1, -1), bias_pad, alog_pad, dskip_pad, norm_w.reshape(1, -1))


def _mlstm_body(xm_ref, og_ref, sm_ref, cw_ref, cb_ref, bias_ref, wq_ref, wk_ref, wv_ref, nw_ref,
                o_ref, xpad_ref, c_ref, n_ref, m_ref):
    L, H, Dh = MLSTM_CHUNK, MLSTM_HEADS, MLSTM_HEAD_DIM

    @pl.when(pl.program_id(1) == 0)
    def _init():
        xpad_ref[0:SUBLANES, :] = jnp.zeros((SUBLANES, D_MODEL), F32)
        c_ref[...] = jnp.zeros_like(c_ref)
        n_ref[...] = jnp.zeros_like(n_ref)
        m_ref[...] = jnp.zeros_like(m_ref)

    xm = xm_ref[...]
    xc = _conv_silu(xm, xpad_ref, cw_ref, cb_ref, L)
    smb = sm_ref[...] + bias_ref[...]
    logf = jnp.minimum(smb, 0.0) - jnp.log(1.0 + jnp.exp(-jnp.abs(smb)))
    bcum_col = _cumsum_rows(logf)
    bcum_row = bcum_col.T
    smb_row = smb.T
    causal = _causal(L)

    for h in range(H):
        ci, cf = SMALL_IG + h, SMALL_FG + h
        bc = bcum_col[:, cf:cf + 1]
        br = bcum_row[cf:cf + 1, :]
        ic = smb[:, ci:ci + 1]
        ir = smb_row[ci:ci + 1, :]
        b_end = bcum_col[L - 1:L, cf:cf + 1]
        dlog = jnp.where(causal, bc - br + ir, -jnp.inf)
        m_intra = jnp.max(dlog, axis=1, keepdims=True)
        g_end = b_end - bc + ic
        m_chunk = jnp.max(g_end, axis=0, keepdims=True)
        w_end = jnp.exp(g_end - m_chunk)

        sl = slice(h * Dh, (h + 1) * Dh)
        q = _dot(xc[:, sl], wq_ref[h])
        k = _dot(xc[:, sl], wk_ref[h]) * (Dh ** -0.5)
        v = _dot(xm[:, sl], wv_ref[h])
        qb, kb, vb = q.astype(BF16), k.astype(BF16), v.astype(BF16)

        m_prev = m_ref[h:h + 1, 0:1]
        c_prev = c_ref[h]
        n_prev = n_ref[h:h + 1, :]
        m_t = jnp.maximum(bc + m_prev, m_intra)
        a_t = jnp.exp(bc + m_prev - m_t)
        qk = _dot_nt(qb, kb) * jnp.exp(dlog - m_t)
        num = a_t * jnp.dot(qb, c_prev.astype(BF16), preferred_element_type=F32) \
            + jnp.dot(qk.astype(BF16), vb, preferred_element_type=F32)
        den = a_t * jnp.sum(q * n_prev, axis=1, keepdims=True) + jnp.sum(qk, axis=1, keepdims=True)
        hid = num / jnp.maximum(jnp.abs(den), jnp.exp(-m_t))
        hid = _rms(hid, nw_ref[:, sl])
        og = og_ref[:, sl]
        o_ref[:, sl] = hid * _sigmoid(og)

        m_new = jnp.maximum(b_end + m_prev, m_chunk)
        a_old = jnp.exp(b_end + m_prev - m_new)
        a_new = jnp.exp(m_chunk - m_new)
        kw = k * w_end
        c_local = jnp.dot(kw.T.astype(BF16), vb, preferred_element_type=F32)
        c_ref[h] = a_old * c_prev + a_new * c_local
        n_ref[h:h + 1, :] = a_old * n_prev + a_new * jnp.sum(kw, axis=0, keepdims=True)
        m_ref[h:h + 1, :] = jnp.broadcast_to(m_new, (1, LANES))


def mlstm_mixer(xm, og, small, conv_w, conv_b, bias_pad, wq, wk, wv, norm_w, bsz):
    t = xm.shape[0]
    L = MLSTM_CHUNK
    nc = t // bsz // L
    row = lambda b, c: (b * nc + c, 0)
    const = lambda b, c: (0, 0)
    const3 = lambda b, c: (0, 0, 0)
    wspec = pl.BlockSpec((MLSTM_HEADS, MLSTM_HEAD_DIM, MLSTM_HEAD_DIM), const3)
    return pl.pallas_call(
        _mlstm_body,
        grid=(bsz, nc),
        in_specs=[pl.BlockSpec((L, D_MODEL), row),
                  pl.BlockSpec((L, D_MODEL), row),
                  pl.BlockSpec((L, SMALL_W), row),
                  pl.BlockSpec((CONV_WIDTH, D_MODEL), const),
                  pl.BlockSpec((1, D_MODEL), const),
                  pl.BlockSpec((1, SMALL_W), const),
                  wspec, wspec, wspec,
                  pl.BlockSpec((1, D_MODEL), const)],
        out_specs=pl.BlockSpec((L, D_MODEL), row),
        out_shape=jax.ShapeDtypeStruct((t, D_MODEL), F32),
        scratch_shapes=[pltpu.VMEM((L + SUBLANES, D_MODEL), F32),
                        pltpu.VMEM((MLSTM_HEADS, MLSTM_HEAD_DIM, MLSTM_HEAD_DIM), F32),
                        pltpu.VMEM((SUBLANES, MLSTM_HEAD_DIM), F32),
                        pltpu.VMEM((SUBLANES, LANES), F32)],
        compiler_params=_cparams(("parallel", "arbitrary")),
        name="mlstm_mixer",
    )(xm, og, small, conv_w, conv_b.reshape(1, -1), bias_pad, wq, wk, wv, norm_w.reshape(1, -1))


def _hgrn_body(uq_ref, uf_ref, ui_ref, ug_ref, lbl_ref, nw_ref, o_ref, s_ref):
    L, H, Dk = HGRN_CHUNK, HGRN_HEADS, HGRN_HEAD_DIM

    @pl.when(pl.program_id(1) == 0)
    def _init():
        s_ref[...] = jnp.zeros_like(s_ref)

    lg = lbl_ref[...]
    lmax = jnp.max(lg, axis=0, keepdims=True)
    le = jnp.exp(lg - lmax)
    lb = le[0:1, :] / jnp.sum(le, axis=0, keepdims=True)

    uf = uf_ref[...]
    uq = uq_ref[...]
    logf = jnp.log(lb + (1.0 - lb) * _sigmoid(uf))
    k = (1.0 - lb) * _sigmoid(-uf)
    q = uq * _sigmoid(uq)
    v = ui_ref[...]
    gc = _cumsum_rows(logf)
    g_ref = gc[L // 2:L // 2 + 1, :]
    g_last = gc[L - 1:L, :]
    q_in = (q * jnp.exp(gc - g_ref)).astype(BF16)
    k_in = (k * jnp.exp(g_ref - gc)).astype(BF16)
    q_off = (q * jnp.exp(gc)).astype(BF16)
    k_end = k * jnp.exp(g_last - gc)
    decay = jnp.exp(g_last)
    causal = _causal(L)
    vb = v.astype(BF16)
    ug = ug_ref[...]

    for h in range(H):
        sl = slice(h * Dk, (h + 1) * Dk)
        att = jnp.where(causal, _dot_nt(q_in[:, sl], k_in[:, sl]), 0.0)
        st_prev = s_ref[h]
        o = jnp.dot(att.astype(BF16), vb[:, sl], preferred_element_type=F32) \
            + _dot_nt(q_off[:, sl], st_prev)
        s_ref[h] = decay[:, sl] * st_prev + _dot(v[:, sl].T, k_end[:, sl])
        g = ug[:, sl]
        o_ref[:, sl] = _rms(o, nw_ref[:, sl]) * (g * _sigmoid(g))


def hgrn_mixer(uq, uf, ui, ug, lb_logits, norm_w, bsz):
    t = uq.shape[0]
    L = HGRN_CHUNK
    nc = t // bsz // L
    row = lambda b, c: (b * nc + c, 0)
    const = lambda b, c: (0, 0)
    rs = pl.BlockSpec((L, D_MODEL), row)
    return pl.pallas_call(
        _hgrn_body,
        grid=(bsz, nc),
        in_specs=[rs, rs, rs, rs,
                  pl.BlockSpec((2, D_MODEL), const),
                  pl.BlockSpec((1, D_MODEL), const)],
        out_specs=rs,
        out_shape=jax.ShapeDtypeStruct((t, D_MODEL), F32),
        scratch_shapes=[pltpu.VMEM((HGRN_HEADS, HGRN_HEAD_DIM, HGRN_HEAD_DIM), F32)],
        compiler_params=_cparams(("parallel", "arbitrary")),
        name="hgrn_mixer",
    )(uq, uf, ui, ug, lb_logits, norm_w.reshape(1, -1))


def _xattn_body(h_ref, nw_ref, wq_ref, k_ref, v_ref, wo_ref, o_ref, obuf_ref):
    hv = h_ref[...]
    hn = _rms(hv, nw_ref[...]).astype(BF16)
    q = jnp.dot(hn, wq_ref[...], preferred_element_type=F32)
    scale = XATTN_HEAD_DIM ** -0.5
    for a in range(XATTN_HEADS):
        sl = slice(a * XATTN_HEAD_DIM, (a + 1) * XATTN_HEAD_DIM)
        s = _dot_nt(q[:, sl], k_ref[:, sl]) * scale
        s = s - jnp.max(s, axis=1, keepdims=True)
        p = jnp.exp(s)
        p = p / jnp.sum(p, axis=1, keepdims=True)
        obuf_ref[:, sl] = jnp.dot(p.astype(BF16), v_ref[:, sl], preferred_element_type=F32).astype(BF16)
    o_ref[...] = hv + jnp.dot(obuf_ref[...], wo_ref[...], preferred_element_type=F32)


def xattn_residual(h, nw, wq_bf16, k_bf16, v_bf16, wo_bf16, bsz, tm=512):
    t, d = h.shape
    seq = t // bsz
    tm = min(tm, seq)
    nt = seq // tm
    row = lambda b, i: (b * nt + i, 0)
    const = lambda b, i: (0, 0)
    mem = lambda b, i: (b, 0)
    return pl.pallas_call(
        _xattn_body,
        grid=(bsz, nt),
        in_specs=[pl.BlockSpec((tm, d), row),
                  pl.BlockSpec((1, d), const),
                  pl.BlockSpec((d, d), const),
                  pl.BlockSpec((MEM_LEN, d), mem),
                  pl.BlockSpec((MEM_LEN, d), mem),
                  pl.BlockSpec((d, d), const)],
        out_specs=pl.BlockSpec((tm, d), row),
        out_shape=jax.ShapeDtypeStruct((t, d), F32),
        scratch_shapes=[pltpu.VMEM((tm, d), BF16)],
        compiler_params=_cparams(("parallel", "parallel")),
        name="xattn_residual",
    )(h, nw.reshape(1, d), wq_bf16, k_bf16, v_bf16, wo_bf16)


def _topk_rows(s, k):
    rows = s.shape[0]
    iota = lax.broadcasted_iota(jnp.int32, s.shape, 0)
    vals, idxs = [], []
    for _ in range(k):
        m = jnp.max(s, axis=0, keepdims=True)
        i = jnp.min(jnp.where(s == m, iota, rows), axis=0, keepdims=True)
        vals.append(m)
        idxs.append(i)
        s = jnp.where(iota == i, -jnp.inf, s)
    return jnp.concatenate(vals, axis=0), jnp.concatenate(idxs, axis=0)


def _select_rows(table, sel, n):
    out = jnp.zeros(sel.shape, jnp.int32)
    for a in range(n):
        out = jnp.where(sel == a, table[a:a + 1, :], out)
    return out


def _peer_route_body(h_ref, nw_ref, wq_ref, sk_ref, hn_ref, idx_ref, gate_ref,
                     hnb_ref, st_ref, oi_ref, og_ref, *, n_col_blocks):
    K = PEER_TOPK

    @pl.when(pl.program_id(1) == 0)
    def _norm():
        xn = _rms(h_ref[...], nw_ref[...])
        hn_ref[...] = xn
        hnb_ref[...] = xn.astype(BF16)

    q = jnp.dot(hnb_ref[...], wq_ref[...], preferred_element_type=F32)
    for p in range(2):
        qp = q[:, p * PEER_HALF:(p + 1) * PEER_HALF]
        st = _dot_nt(sk_ref[0, p], qp)
        for cb in range(n_col_blocks):
            st_ref[p * n_col_blocks + cb] = st[:, cb * LANES:(cb + 1) * LANES]

    def col_block(cb, carry):
        v1, i1 = _topk_rows(st_ref[cb], K)
        v2, i2 = _topk_rows(st_ref[n_col_blocks + cb], K)
        cand = jnp.concatenate([v1[a:a + 1, :] + v2 for a in range(K)], axis=0)
        top_s, pos = _topk_rows(cand, K)
        e1 = _select_rows(i1, lax.shift_right_logical(pos, 4), K)
        e2 = _select_rows(i2, lax.bitwise_and(pos, K - 1), K)
        oi_ref[cb] = e1 * PEER_N_KEYS + e2
        ex = jnp.exp(top_s - jnp.max(top_s, axis=0, keepdims=True))
        og_ref[cb] = ex / jnp.sum(ex, axis=0, keepdims=True)
        return carry

    lax.fori_loop(0, n_col_blocks, col_block, 0)
    for cb in range(n_col_blocks):
        idx_ref[0, :, cb * LANES:(cb + 1) * LANES] = oi_ref[cb]
        gate_ref[0, :, cb * LANES:(cb + 1) * LANES] = og_ref[cb]


def peer_route(h, nw, wq_bf16, subkeys_bf16, tm=512):
    t, d = h.shape
    tm = min(tm, t)
    ncb = tm // LANES
    hq = 2 * PEER_HALF
    return pl.pallas_call(
        functools.partial(_peer_route_body, n_col_blocks=ncb),
        grid=(t // tm, PEER_HEADS),
        in_specs=[pl.BlockSpec((tm, d), lambda i, hd: (i, 0)),
                  pl.BlockSpec((1, d), lambda i, hd: (0, 0)),
                  pl.BlockSpec((d, hq), lambda i, hd: (0, hd)),
                  pl.BlockSpec((1, 2, PEER_N_KEYS, PEER_HALF), lambda i, hd: (hd, 0, 0, 0))],
        out_specs=[pl.BlockSpec((tm, d), lambda i, hd: (i, 0)),
                   pl.BlockSpec((1, PEER_TOPK, tm), lambda i, hd: (hd, 0, i)),
                   pl.BlockSpec((1, PEER_TOPK, tm), lambda i, hd: (hd, 0, i))],
        out_shape=[jax.ShapeDtypeStruct((t, d), F32),
                   jax.ShapeDtypeStruct((PEER_HEADS, PEER_TOPK, t), jnp.int32),
                   jax.ShapeDtypeStruct((PEER_HEADS, PEER_TOPK, t), F32)],
        scratch_shapes=[pltpu.VMEM((tm, d), BF16),
                        pltpu.VMEM((2 * ncb, PEER_N_KEYS, LANES), F32),
                        pltpu.VMEM((ncb, PEER_TOPK, LANES), jnp.int32),
                        pltpu.VMEM((ncb, PEER_TOPK, LANES), F32)],
        compiler_params=_cparams(("parallel", "arbitrary")),
        name="peer_route",
    )(h, nw.reshape(1, d), wq_bf16, subkeys_bf16)


PEER_TOK_BLOCK = 16
PEER_ISSUE_UNROLL = 8
ROW_CHUNKS = D_MODEL // LANES
UV_CHUNKS = 2 * ROW_CHUNKS
ROW_PITCH = UV_CHUNKS + 1


def _peer_expert_body(idx_ref, x_ref, gate_ref, h_ref, uv_hbm, o_ref, buf, sem):
    i = pl.program_id(0)
    n_blocks = pl.num_programs(0) - 1
    tb = PEER_TOK_BLOCK

    @pl.when(i < n_blocks)
    def _issue():
        slot = i % 2

        def rows(ro, carry):
            for ri in range(PEER_ISSUE_UNROLL):
                r = ro * PEER_ISSUE_UNROLL + ri
                pltpu.make_async_copy(uv_hbm.at[idx_ref[r]], buf.at[slot, pl.ds(r * ROW_PITCH, UV_CHUNKS), :],
                                      sem.at[slot]).start()
            return carry

        lax.fori_loop(0, tb * PEER_SEL // PEER_ISSUE_UNROLL, rows, 0)

    @pl.when(i > 0)
    def _compute():
        slot = (i - 1) % 2
        done = buf.at[slot, pl.ds(0, tb * PEER_SEL * UV_CHUNKS), :]
        pltpu.make_async_copy(done, done, sem.at[slot]).wait()
        gate_t = gate_ref[...].T
        for tk in range(tb):
            base = tk * PEER_SEL * ROW_PITCH
            acc = None
            for c in range(ROW_CHUNKS):
                u_c = buf[slot, pl.ds(base + c, PEER_SEL, stride=ROW_PITCH), :]
                term = u_c * x_ref[tk, c:c + 1, :]
                acc = term if acc is None else acc + term
            hsel = jnp.sum(acc, axis=1, keepdims=True)
            act = 0.5 * hsel * (1.0 + lax.erf(hsel * (2.0 ** -0.5)))
            w = gate_t[:, tk:tk + 1] * act
            outs = []
            for c in range(ROW_CHUNKS):
                v_c = buf[slot, pl.ds(base + ROW_CHUNKS + c, PEER_SEL, stride=ROW_PITCH), :]
                outs.append(jnp.sum(v_c * w, axis=0, keepdims=True))
            o_ref[tk] = h_ref[tk] + jnp.concatenate(outs, axis=0)


def peer_expert_residual(idx_flat, hn, gate_tok, h, uv_tab):
    t, d = h.shape
    tb = PEER_TOK_BLOCK
    assert t % tb == 0
    n = t // tb
    prev3 = lambda i: (jnp.maximum(i - 1, 0), 0, 0)
    tok_spec = pl.BlockSpec((tb, ROW_CHUNKS, LANES), prev3)
    out = pl.pallas_call(
        _peer_expert_body,
        grid=(n + 1,),
        in_specs=[pl.BlockSpec((tb * PEER_SEL,), lambda i: (jnp.minimum(i, n - 1),),
                               memory_space=pltpu.SMEM),
                  tok_spec,
                  pl.BlockSpec((tb, PEER_SEL), lambda i: (jnp.maximum(i - 1, 0), 0)),
                  tok_spec,
                  pl.BlockSpec(memory_space=pl.ANY)],
        out_specs=tok_spec,
        out_shape=jax.ShapeDtypeStruct((t, ROW_CHUNKS, LANES), F32),
        scratch_shapes=[pltpu.VMEM((2, tb * PEER_SEL * ROW_PITCH, LANES), F32),
                        pltpu.SemaphoreType.DMA((2,))],
        compiler_params=_cparams(("arbitrary",)),
        name="peer_expert",
    )(idx_flat, hn.reshape(t, ROW_CHUNKS, LANES), gate_tok, h.reshape(t, ROW_CHUNKS, LANES), uv_tab)
    return out.reshape(t, d)


def _rmsnorm_body(x_ref, w_ref, o_ref):
    o_ref[...] = _rms(x_ref[...], w_ref[...])


def rmsnorm(x, w, tm=1024):
    t, d = x.shape
    tm = min(tm, t)
    return pl.pallas_call(
        _rmsnorm_body,
        grid=(t // tm,),
        in_specs=[pl.BlockSpec((tm, d), lambda i: (i, 0)), pl.BlockSpec((1, d), lambda i: (0, 0))],
        out_specs=pl.BlockSpec((tm, d), lambda i: (i, 0)),
        out_shape=jax.ShapeDtypeStruct((t, d), F32),
        compiler_params=_cparams(("parallel",)),
        name="rmsnorm",
    )(x, w.reshape(1, d))


def _pad_cols(v, start, width=SMALL_W):
    out = jnp.zeros((1, width), F32)
    return out.at[0, start:start + v.shape[0]].set(v.astype(F32))


def _even_mix(h, bsz, norm_w, w_in, w_out, ssd_conv_w, ssd_conv_b, ssd_dt_bias, ssd_a_log, ssd_d_skip,
              ssd_norm, ml_conv_w, ml_conv_b, ml_wq, ml_wk, ml_wv, ml_i_bias, ml_f_bias, ml_norm):
    d = D_MODEL
    o1 = d
    o2 = o1 + SSD_XBC
    o3 = o2 + SSD_HEADS
    o4 = o3 + d
    o5 = o4 + d
    o6 = o5 + MLSTM_HEADS
    o7 = o6 + MLSTM_HEADS
    w_small = jnp.concatenate([w_in[:, o2:o3], w_in[:, o5:o6], w_in[:, o6:o7],
                               jnp.zeros((d, SMALL_W - (o3 - o2) - (o7 - o5)), w_in.dtype)], axis=1)
    w_all = jnp.concatenate([w_in[:, :o1], w_in[:, o1:o2], w_in[:, o3:o4], w_in[:, o4:o5], w_small],
                            axis=1).astype(BF16)
    z, xbc, xm, og, small = norm_proj(h, norm_w, w_all, (d, SSD_XBC, d, d, SMALL_W), tm=256)
    bias_pad = (_pad_cols(ssd_dt_bias, SMALL_DT) + _pad_cols(ml_i_bias, SMALL_IG)
                + _pad_cols(ml_f_bias, SMALL_FG))
    y_a = ssd_mixer(z, xbc, small, ssd_conv_w, ssd_conv_b, bias_pad, _pad_cols(ssd_a_log, SMALL_DT),
                    _pad_cols(ssd_d_skip, SMALL_DT), ssd_norm, bsz)
    y_b = mlstm_mixer(xm, og, small, ml_conv_w, ml_conv_b, bias_pad, ml_wq.astype(BF16),
                      ml_wk.astype(BF16), ml_wv.astype(BF16), ml_norm, bsz)
    w_out_b = w_out.astype(BF16)
    return proj_residual([y_a, y_b], [w_out_b[:d], w_out_b[d:]], h)


def _odd_mix(h, bsz, norm_w, w_in, w_out, lb_logits, hg_norm):
    d = D_MODEL
    uq, uf, ui, ug = norm_proj(h, norm_w, w_in.astype(BF16), (d, d, d, d), tm=256)
    y = hgrn_mixer(uq, uf, ui, ug, lb_logits, hg_norm, bsz)
    return proj_residual([y], [w_out.astype(BF16)], h)


def _xattn(h, mem2d, bsz, norm_x, norm_m, wq, wkv, wo):
    k, v = norm_proj(mem2d, norm_m, wkv.astype(BF16), (D_MODEL, D_MODEL), out_dtype=BF16)
    return xattn_residual(h, norm_x, wq.astype(BF16), k, v, wo.astype(BF16), bsz)


def _peer(h, norm_w, wq, subkeys, u_tab, v_tab):
    t = h.shape[0]
    n_exp = u_tab.shape[0]
    hn, idx, gate = peer_route(h, norm_w, wq.astype(BF16), subkeys.astype(BF16))
    idx_flat = idx.reshape(PEER_SEL, t).T.reshape(t * PEER_SEL)
    gate_tok = gate.reshape(PEER_SEL, t).T
    uv_tab = jnp.concatenate([u_tab.reshape(n_exp, ROW_CHUNKS, LANES), v_tab.reshape(n_exp, ROW_CHUNKS, LANES)],
                             axis=1)
    return peer_expert_residual(idx_flat, hn, gate_tok, h, uv_tab)


def kernel(x, mem, norm_mix, norm_xattn, norm_mem, norm_ffn, norm_final, ev_w_in, ev_w_out, ssd_conv_w, ssd_conv_b, ssd_dt_bias, ssd_a_log, ssd_d_skip, ssd_norm, ml_conv_w, ml_conv_b, ml_wq, ml_wk, ml_wv, ml_i_bias, ml_f_bias, ml_norm, od_w_in, od_w_out, hgrn_lb_logits, hgrn_norm, xa_wq, xa_wkv, xa_wo, peer_wq, peer_subkeys, peer_u, peer_v):
    bsz, seq, d = x.shape
    depth = norm_mix.shape[0]
    assert depth == 2 and hgrn_lb_logits.shape[0] == 2
    h = x.reshape(bsz * seq, d)
    mem2d = mem.reshape(bsz * MEM_LEN, d)
    for layer in range(depth):
        if layer % 2 == 0:
            e = layer // 2
            h = _even_mix(h, bsz, norm_mix[layer], ev_w_in[e], ev_w_out[e], ssd_conv_w[e], ssd_conv_b[e],
                          ssd_dt_bias[e], ssd_a_log[e], ssd_d_skip[e], ssd_norm[e], ml_conv_w[e],
                          ml_conv_b[e], ml_wq[e], ml_wk[e], ml_wv[e], ml_i_bias[e], ml_f_bias[e], ml_norm[e])
        else:
            o = layer // 2
            h = _odd_mix(h, bsz, norm_mix[layer], od_w_in[o], od_w_out[o], hgrn_lb_logits, hgrn_norm[o])
        h = _xattn(h, mem2d, bsz, norm_xattn[layer], norm_mem[layer], xa_wq[layer], xa_wkv[layer], xa_wo[layer])
        h = _peer(h, norm_ffn[layer], peer_wq[layer], peer_subkeys[layer], peer_u[layer], peer_v[layer])
    return rmsnorm(h, norm_final).reshape(bsz, seq, d)
```

```python
import functools

import jax
import jax.numpy as jnp
from jax import lax
from jax.experimental import pallas as pl
from jax.experimental.pallas import tpu as pltpu

F32 = jnp.float32
BF16 = jnp.bfloat16
HIGHEST = lax.Precision.HIGHEST

D_MODEL = 1024
EPS = 1e-6
CONV_WIDTH = 4
MEM_LEN = 256
SSD_HEAD_DIM = 64
SSD_HEADS = 16
SSD_GROUPS = 2
SSD_STATE = 128
SSD_XBC = D_MODEL + 2 * SSD_GROUPS * SSD_STATE
SSD_CHUNK = 128
MLSTM_HEADS = 4
MLSTM_HEAD_DIM = 256
MLSTM_CHUNK = 128
HGRN_HEADS = 8
HGRN_HEAD_DIM = 128
HGRN_CHUNK = 64
XATTN_HEADS = 4
XATTN_HEAD_DIM = 256
PEER_HEADS = 8
PEER_N_KEYS = 128
PEER_TOPK = 16
PEER_HALF = 128
PEER_SEL = PEER_HEADS * PEER_TOPK

SUBLANES = 8
LANES = 128
SMALL_W = LANES
SMALL_DT = 0
SMALL_IG = SSD_HEADS
SMALL_FG = SSD_HEADS + MLSTM_HEADS

VMEM_LIMIT = 52 * 1024 * 1024


def _cparams(sem):
    return pltpu.CompilerParams(dimension_semantics=sem, vmem_limit_bytes=VMEM_LIMIT)


def _sigmoid(x):
    return 1.0 / (1.0 + jnp.exp(-x))


def _softplus(x):
    return jnp.maximum(x, 0.0) + jnp.log(1.0 + jnp.exp(-jnp.abs(x)))


def _rms(x, w):
    return x * lax.rsqrt(jnp.mean(x * x, axis=-1, keepdims=True) + EPS) * w


def _dot(a, b):
    return jnp.dot(a.astype(BF16), b.astype(BF16), preferred_element_type=F32)


def _dot_nt(a, b):
    return lax.dot_general(a.astype(BF16), b.astype(BF16), (((1,), (1,)), ((), ())),
                           preferred_element_type=F32)


def _cumsum_rows(x):
    n = x.shape[0]
    r = lax.broadcasted_iota(jnp.int32, (n, n), 0)
    c = lax.broadcasted_iota(jnp.int32, (n, n), 1)
    tril = (r >= c).astype(F32)
    return jnp.dot(tril, x, precision=HIGHEST, preferred_element_type=F32)


def _causal(n):
    r = lax.broadcasted_iota(jnp.int32, (n, n), 0)
    c = lax.broadcasted_iota(jnp.int32, (n, n), 1)
    return r >= c


def _norm_proj_body(x_ref, nw_ref, w_ref, *out_refs, splits):
    xn = _rms(x_ref[...], nw_ref[...]).astype(BF16)
    off = 0
    for o_ref, n in zip(out_refs, splits):
        o_ref[...] = jnp.dot(xn, w_ref[:, off:off + n], preferred_element_type=F32).astype(o_ref.dtype)
        off += n


def norm_proj(x, nw, w_bf16, splits, tm=512, out_dtype=F32):
    t, d = x.shape
    n = w_bf16.shape[1]
    assert sum(splits) == n
    tm = min(tm, t)
    assert t % tm == 0
    return pl.pallas_call(
        functools.partial(_norm_proj_body, splits=tuple(splits)),
        grid=(t // tm,),
        in_specs=[pl.BlockSpec((tm, d), lambda i: (i, 0)),
                  pl.BlockSpec((1, d), lambda i: (0, 0)),
                  pl.BlockSpec((d, n), lambda i: (0, 0))],
        out_specs=[pl.BlockSpec((tm, s), lambda i: (i, 0)) for s in splits],
        out_shape=[jax.ShapeDtypeStruct((t, s), out_dtype) for s in splits],
        compiler_params=_cparams(("parallel",)),
        name="norm_proj",
    )(x, nw.reshape(1, d), w_bf16)


def _proj_residual_body(*refs, n_parts):
    a_refs = refs[:n_parts]
    w_refs = refs[n_parts:2 * n_parts]
    r_ref = refs[2 * n_parts]
    o_ref = refs[2 * n_parts + 1]
    acc = r_ref[...]
    for a_ref, w_ref in zip(a_refs, w_refs):
        acc = acc + jnp.dot(a_ref[...].astype(BF16), w_ref[...], preferred_element_type=F32)
    o_ref[...] = acc


def proj_residual(parts, ws_bf16, res, tm=512):
    t, d = res.shape
    tm = min(tm, t)
    n_parts = len(parts)
    in_specs = ([pl.BlockSpec((tm, p.shape[1]), lambda i: (i, 0)) for p in parts]
                + [pl.BlockSpec(w.shape, lambda i: (0, 0)) for w in ws_bf16]
                + [pl.BlockSpec((tm, d), lambda i: (i, 0))])
    return pl.pallas_call(
        functools.partial(_proj_residual_body, n_parts=n_parts),
        grid=(t // tm,),
        in_specs=in_specs,
        out_specs=pl.BlockSpec((tm, d), lambda i: (i, 0)),
        out_shape=jax.ShapeDtypeStruct((t, d), F32),
        compiler_params=_cparams(("parallel",)),
        name="proj_residual",
    )(*parts, *ws_bf16, res)


def _conv_silu(x_in, xpad_ref, cw_ref, cb_ref, n_rows):
    xpad_ref[SUBLANES:SUBLANES + n_rows, :] = x_in
    acc = cb_ref[...] + cw_ref[CONV_WIDTH - 1:CONV_WIDTH, :] * x_in
    for k in range(CONV_WIDTH - 1):
        start = SUBLANES - (CONV_WIDTH - 1) + k
        acc = acc + cw_ref[k:k + 1, :] * xpad_ref[start:start + n_rows, :]
    xpad_ref[0:SUBLANES, :] = x_in[n_rows - SUBLANES:n_rows, :]
    return acc * _sigmoid(acc)


def _ssd_body(z_ref, xbc_ref, sm_ref, cw_ref, cb_ref, bias_ref, alog_ref, dsk_ref, nw_ref,
              o_ref, xpad_ref, state_ref, ybuf_ref):
    L, P, N = SSD_CHUNK, SSD_HEAD_DIM, SSD_STATE
    heads_per_group = SSD_HEADS // SSD_GROUPS

    @pl.when(pl.program_id(1) == 0)
    def _init():
        xpad_ref[0:SUBLANES, :] = jnp.zeros((SUBLANES, SSD_XBC), F32)
        state_ref[...] = jnp.zeros_like(state_ref)

    xa = _conv_silu(xbc_ref[...], xpad_ref, cw_ref, cb_ref, L)
    dt = _softplus(sm_ref[...] + bias_ref[...])
    a = -jnp.exp(alog_ref[...])
    cs_col = _cumsum_rows(dt * a)
    cs_row = cs_col.T
    causal = _causal(L)

    for g in range(SSD_GROUPS):
        b_g = xa[:, D_MODEL + g * N:D_MODEL + (g + 1) * N]
        c_g = xa[:, D_MODEL + (SSD_GROUPS + g) * N:D_MODEL + (SSD_GROUPS + g + 1) * N]
        cb = _dot_nt(c_g, b_g)
        b_gt = b_g.T.astype(BF16)
        c_gb = c_g.astype(BF16)
        for e in range(heads_per_group):
            he = g * heads_per_group + e
            col = SMALL_DT + he
            cs_c = cs_col[:, col:col + 1]
            cs_r = cs_row[col:col + 1, :]
            cs_last = cs_col[L - 1:L, col:col + 1]
            decay = jnp.exp(jnp.where(causal, cs_c - cs_r, -jnp.inf))
            x_e = xa[:, he * P:(he + 1) * P]
            x_dt = x_e * dt[:, col:col + 1]
            y = _dot(cb * decay, x_dt)
            s_prev = state_ref[he]
            y = y + jnp.dot(c_gb, s_prev.astype(BF16), preferred_element_type=F32) * jnp.exp(cs_c)
            y = y + x_e * dsk_ref[:, col:col + 1]
            ybuf_ref[:, he * P:(he + 1) * P] = y
            x_end = (x_dt * jnp.exp(cs_last - cs_c)).astype(BF16)
            state_ref[he] = jnp.exp(cs_last) * s_prev + jnp.dot(b_gt, x_end, preferred_element_type=F32)

    zz = z_ref[...]
    y = ybuf_ref[...] * (zz * _sigmoid(zz))
    gw = D_MODEL // SSD_GROUPS
    for g in range(SSD_GROUPS):
        o_ref[:, g * gw:(g + 1) * gw] = _rms(y[:, g * gw:(g + 1) * gw], nw_ref[:, g * gw:(g + 1) * gw])


def ssd_mixer(z, xbc, small, conv_w, conv_b, bias_pad, alog_pad, dskip_pad, norm_w, bsz):
    t = z.shape[0]
    L = SSD_CHUNK
    nc = t // bsz // L
    row = lambda b, c: (b * nc + c, 0)
    const = lambda b, c: (0, 0)
    return pl.pallas_call(
        _ssd_body,
        grid=(bsz, nc),
        in_specs=[pl.BlockSpec((L, D_MODEL), row),
                  pl.BlockSpec((L, SSD_XBC), row),
                  pl.BlockSpec((L, SMALL_W), row),
                  pl.BlockSpec((CONV_WIDTH, SSD_XBC), const),
                  pl.BlockSpec((1, SSD_XBC), const),
                  pl.BlockSpec((1, SMALL_W), const),
                  pl.BlockSpec((1, SMALL_W), const),
                  pl.BlockSpec((1, SMALL_W), const),
                  pl.BlockSpec((1, D_MODEL), const)],
        out_specs=pl.BlockSpec((L, D_MODEL), row),
        out_shape=jax.ShapeDtypeStruct((t, D_MODEL), F32),
        scratch_shapes=[pltpu.VMEM((L + SUBLANES, SSD_XBC), F32),
                        pltpu.VMEM((SSD_HEADS, SSD_STATE, SSD_HEAD_DIM), F32),
                        pltpu.VMEM((L, D_MODEL), F32)],
        compiler_params=_cparams(("parallel", "arbitrary")),
        name="ssd_mixer",
    )(z, xbc, small, conv_w, conv_b.reshape(1, -1), bias_pad, alog_pad, dskip_pad, norm_w.reshape(1, -1))


def _mlstm_body(xm_ref, og_ref, sm_ref, cw_ref, cb_ref, bias_ref, wq_ref, wk_ref, wv_ref, nw_ref,
                o_ref, xpad_ref, c_ref, n_ref, m_ref):
    L, H, Dh = MLSTM_CHUNK, MLSTM_HEADS, MLSTM_HEAD_DIM

    @pl.when(pl.program_id(1) == 0)
    def _init():
        xpad_ref[0:SUBLANES, :] = jnp.zeros((SUBLANES, D_MODEL), F32)
        c_ref[...] = jnp.zeros_like(c_ref)
        n_ref[...] = jnp.zeros_like(n_ref)
        m_ref[...] = jnp.zeros_like(m_ref)

    xm = xm_ref[...]
    xc = _conv_silu(xm, xpad_ref, cw_ref, cb_ref, L)
    smb = sm_ref[...] + bias_ref[...]
    logf = jnp.minimum(smb, 0.0) - jnp.log(1.0 + jnp.exp(-jnp.abs(smb)))
    bcum_col = _cumsum_rows(logf)
    bcum_row = bcum_col.T
    smb_row = smb.T
    causal = _causal(L)

    for h in range(H):
        ci, cf = SMALL_IG + h, SMALL_FG + h
        bc = bcum_col[:, cf:cf + 1]
        br = bcum_row[cf:cf + 1, :]
        ic = smb[:, ci:ci + 1]
        ir = smb_row[ci:ci + 1, :]
        b_end = bcum_col[L - 1:L, cf:cf + 1]
        dlog = jnp.where(causal, bc - br + ir, -jnp.inf)
        m_intra = jnp.max(dlog, axis=1, keepdims=True)
        g_end = b_end - bc + ic
        m_chunk = jnp.max(g_end, axis=0, keepdims=True)
        w_end = jnp.exp(g_end - m_chunk)

        sl = slice(h * Dh, (h + 1) * Dh)
        q = _dot(xc[:, sl], wq_ref[h])
        k = _dot(xc[:, sl], wk_ref[h]) * (Dh ** -0.5)
        v = _dot(xm[:, sl], wv_ref[h])
        qb, kb, vb = q.astype(BF16), k.astype(BF16), v.astype(BF16)

        m_prev = m_ref[h:h + 1, 0:1]
        c_prev = c_ref[h]
        n_prev = n_ref[h:h + 1, :]
        m_t = jnp.maximum(bc + m_prev, m_intra)
        a_t = jnp.exp(bc + m_prev - m_t)
        qk = _dot_nt(qb, kb) * jnp.exp(dlog - m_t)
        num = a_t * jnp.dot(qb, c_prev.astype(BF16), preferred_element_type=F32) \
            + jnp.dot(qk.astype(BF16), vb, preferred_element_type=F32)
        den = a_t * jnp.sum(q * n_prev, axis=1, keepdims=True) + jnp.sum(qk, axis=1, keepdims=True)
        hid = num / jnp.maximum(jnp.abs(den), jnp.exp(-m_t))
        hid = _rms(hid, nw_ref[:, sl])
        og = og_ref[:, sl]
        o_ref[:, sl] = hid * _sigmoid(og)

        m_new = jnp.maximum(b_end + m_prev, m_chunk)
        a_old = jnp.exp(b_end + m_prev - m_new)
        a_new = jnp.exp(m_chunk - m_new)
        kw = k * w_end
        c_local = jnp.dot(kw.T.astype(BF16), vb, preferred_element_type=F32)
        c_ref[h] = a_old * c_prev + a_new * c_local
        n_ref[h:h + 1, :] = a_old * n_prev + a_new * jnp.sum(kw, axis=0, keepdims=True)
        m_ref[h:h + 1, :] = jnp.broadcast_to(m_new, (1, LANES))


def mlstm_mixer(xm, og, small, conv_w, conv_b, bias_pad, wq, wk, wv, norm_w, bsz):
    t = xm.shape[0]
    L = MLSTM_CHUNK
    nc = t // bsz // L
    row = lambda b, c: (b * nc + c, 0)
    const = lambda b, c: (0, 0)
    const3 = lambda b, c: (0, 0, 0)
    wspec = pl.BlockSpec((MLSTM_HEADS, MLSTM_HEAD_DIM, MLSTM_HEAD_DIM), const3)
    return pl.pallas_call(
        _mlstm_body,
        grid=(bsz, nc),
        in_specs=[pl.BlockSpec((L, D_MODEL), row),
                  pl.BlockSpec((L, D_MODEL), row),
                  pl.BlockSpec((L, SMALL_W), row),
                  pl.BlockSpec((CONV_WIDTH, D_MODEL), const),
                  pl.BlockSpec((1, D_MODEL), const),
                  pl.BlockSpec((1, SMALL_W), const),
                  wspec, wspec, wspec,
                  pl.BlockSpec((1, D_MODEL), const)],
        out_specs=pl.BlockSpec((L, D_MODEL), row),
        out_shape=jax.ShapeDtypeStruct((t, D_MODEL), F32),
        scratch_shapes=[pltpu.VMEM((L + SUBLANES, D_MODEL), F32),
                        pltpu.VMEM((MLSTM_HEADS, MLSTM_HEAD_DIM, MLSTM_HEAD_DIM), F32),
                        pltpu.VMEM((SUBLANES, MLSTM_HEAD_DIM), F32),
                        pltpu.VMEM((SUBLANES, LANES), F32)],
        compiler_params=_cparams(("parallel", "arbitrary")),
        name="mlstm_mixer",
    )(xm, og, small, conv_w, conv_b.reshape(1, -1), bias_pad, wq, wk, wv, norm_w.reshape(1, -1))


def _hgrn_body(uq_ref, uf_ref, ui_ref, ug_ref, lbl_ref, nw_ref, o_ref, s_ref):
    L, H, Dk = HGRN_CHUNK, HGRN_HEADS, HGRN_HEAD_DIM

    @pl.when(pl.program_id(1) == 0)
    def _init():
        s_ref[...] = jnp.zeros_like(s_ref)

    lg = lbl_ref[...]
    lmax = jnp.max(lg, axis=0, keepdims=True)
    le = jnp.exp(lg - lmax)
    lb = le[0:1, :] / jnp.sum(le, axis=0, keepdims=True)

    uf = uf_ref[...]
    uq = uq_ref[...]
    logf = jnp.log(lb + (1.0 - lb) * _sigmoid(uf))
    k = (1.0 - lb) * _sigmoid(-uf)
    q = uq * _sigmoid(uq)
    v = ui_ref[...]
    gc = _cumsum_rows(logf)
    g_ref = gc[L // 2:L // 2 + 1, :]
    g_last = gc[L - 1:L, :]
    q_in = (q * jnp.exp(gc - g_ref)).astype(BF16)
    k_in = (k * jnp.exp(g_ref - gc)).astype(BF16)
    q_off = (q * jnp.exp(gc)).astype(BF16)
    k_end = k * jnp.exp(g_last - gc)
    decay = jnp.exp(g_last)
    causal = _causal(L)
    vb = v.astype(BF16)
    ug = ug_ref[...]

    for h in range(H):
        sl = slice(h * Dk, (h + 1) * Dk)
        att = jnp.where(causal, _dot_nt(q_in[:, sl], k_in[:, sl]), 0.0)
        st_prev = s_ref[h]
        o = jnp.dot(att.astype(BF16), vb[:, sl], preferred_element_type=F32) \
            + _dot_nt(q_off[:, sl], st_prev)
        s_ref[h] = decay[:, sl] * st_prev + _dot(v[:, sl].T, k_end[:, sl])
        g = ug[:, sl]
        o_ref[:, sl] = _rms(o, nw_ref[:, sl]) * (g * _sigmoid(g))


def hgrn_mixer(uq, uf, ui, ug, lb_logits, norm_w, bsz):
    t = uq.shape[0]
    L = HGRN_CHUNK
    nc = t // bsz // L
    row = lambda b, c: (b * nc + c, 0)
    const = lambda b, c: (0, 0)
    rs = pl.BlockSpec((L, D_MODEL), row)
    return pl.pallas_call(
        _hgrn_body,
        grid=(bsz, nc),
        in_specs=[rs, rs, rs, rs,
                  pl.BlockSpec((2, D_MODEL), const),
                  pl.BlockSpec((1, D_MODEL), const)],
        out_specs=rs,
        out_shape=jax.ShapeDtypeStruct((t, D_MODEL), F32),
        scratch_shapes=[pltpu.VMEM((HGRN_HEADS, HGRN_HEAD_DIM, HGRN_HEAD_DIM), F32)],
        compiler_params=_cparams(("parallel", "arbitrary")),
        name="hgrn_mixer",
    )(uq, uf, ui, ug, lb_logits, norm_w.reshape(1, -1))


def _xattn_body(h_ref, nw_ref, wq_ref, k_ref, v_ref, wo_ref, o_ref, obuf_ref):
    hv = h_ref[...]
    hn = _rms(hv, nw_ref[...]).astype(BF16)
    q = jnp.dot(hn, wq_ref[...], preferred_element_type=F32)
    scale = XATTN_HEAD_DIM ** -0.5
    for a in range(XATTN_HEADS):
        sl = slice(a * XATTN_HEAD_DIM, (a + 1) * XATTN_HEAD_DIM)
        s = _dot_nt(q[:, sl], k_ref[:, sl]) * scale
        s = s - jnp.max(s, axis=1, keepdims=True)
        p = jnp.exp(s)
        p = p / jnp.sum(p, axis=1, keepdims=True)
        obuf_ref[:, sl] = jnp.dot(p.astype(BF16), v_ref[:, sl], preferred_element_type=F32).astype(BF16)
    o_ref[...] = hv + jnp.dot(obuf_ref[...], wo_ref[...], preferred_element_type=F32)


def xattn_residual(h, nw, wq_bf16, k_bf16, v_bf16, wo_bf16, bsz, tm=512):
    t, d = h.shape
    seq = t // bsz
    tm = min(tm, seq)
    nt = seq // tm
    row = lambda b, i: (b * nt + i, 0)
    const = lambda b, i: (0, 0)
    mem = lambda b, i: (b, 0)
    return pl.pallas_call(
        _xattn_body,
        grid=(bsz, nt),
        in_specs=[pl.BlockSpec((tm, d), row),
                  pl.BlockSpec((1, d), const),
                  pl.BlockSpec((d, d), const),
                  pl.BlockSpec((MEM_LEN, d), mem),
                  pl.BlockSpec((MEM_LEN, d), mem),
                  pl.BlockSpec((d, d), const)],
        out_specs=pl.BlockSpec((tm, d), row),
        out_shape=jax.ShapeDtypeStruct((t, d), F32),
        scratch_shapes=[pltpu.VMEM((tm, d), BF16)],
        compiler_params=_cparams(("parallel", "parallel")),
        name="xattn_residual",
    )(h, nw.reshape(1, d), wq_bf16, k_bf16, v_bf16, wo_bf16)


def _topk_rows(s, k):
    rows = s.shape[0]
    iota = lax.broadcasted_iota(jnp.int32, s.shape, 0)
    vals, idxs = [], []
    for _ in range(k):
        m = jnp.max(s, axis=0, keepdims=True)
        i = jnp.min(jnp.where(s == m, iota, rows), axis=0, keepdims=True)
        vals.append(m)
        idxs.append(i)
        s = jnp.where(iota == i, -jnp.inf, s)
    return jnp.concatenate(vals, axis=0), jnp.concatenate(idxs, axis=0)


def _select_rows(table, sel, n):
    out = jnp.zeros(sel.shape, jnp.int32)
    for a in range(n):
        out = jnp.where(sel == a, table[a:a + 1, :], out)
    return out


def _peer_route_body(h_ref, nw_ref, wq_ref, sk_ref, hn_ref, idx_ref, gate_ref,
                     hnb_ref, st_ref, oi_ref, og_ref, *, n_col_blocks):
    K = PEER_TOPK

    @pl.when(pl.program_id(1) == 0)
    def _norm():
        xn = _rms(h_ref[...], nw_ref[...])
        hn_ref[...] = xn
        hnb_ref[...] = xn.astype(BF16)

    q = jnp.dot(hnb_ref[...], wq_ref[...], preferred_element_type=F32)
    for p in range(2):
        qp = q[:, p * PEER_HALF:(p + 1) * PEER_HALF]
        st = _dot_nt(sk_ref[0, p], qp)
        for cb in range(n_col_blocks):
            st_ref[p * n_col_blocks + cb] = st[:, cb * LANES:(cb + 1) * LANES]

    def col_block(cb, carry):
        v1, i1 = _topk_rows(st_ref[cb], K)
        v2, i2 = _topk_rows(st_ref[n_col_blocks + cb], K)
        sub = lax.broadcasted_iota(jnp.int32, (SUBLANES, LANES), 0)
        blocks = [v1[0:1, :] + v2[0:SUBLANES, :], v1[0:1, :] + v2[SUBLANES:K, :]]
        for a in range(1, SUBLANES):
            blocks.append(jnp.where(sub < K // (a + 1), v1[a:a + 1, :] + v2[0:SUBLANES, :], -jnp.inf))
        blocks.append(v1[SUBLANES:K, :] + v2[0:1, :])
        cand = jnp.concatenate(blocks, axis=0)
        top_s, pos = _topk_rows(cand, K)
        blk = lax.shift_right_logical(pos, 3)
        within = lax.bitwise_and(pos, SUBLANES - 1)
        a_sel = jnp.where(blk <= 1, 0, jnp.where(blk <= SUBLANES, blk - 1, within + SUBLANES))
        b_sel = jnp.where(blk == 1, within + SUBLANES, jnp.where(blk <= SUBLANES, within, 0))
        e1 = _select_rows(i1, a_sel, K)
        e2 = _select_rows(i2, b_sel, K)
        oi_ref[cb] = e1 * PEER_N_KEYS + e2
        ex = jnp.exp(top_s - jnp.max(top_s, axis=0, keepdims=True))
        og_ref[cb] = ex / jnp.sum(ex, axis=0, keepdims=True)
        return carry

    lax.fori_loop(0, n_col_blocks, col_block, 0)
    for cb in range(n_col_blocks):
        idx_ref[0, :, cb * LANES:(cb + 1) * LANES] = oi_ref[cb]
        gate_ref[0, :, cb * LANES:(cb + 1) * LANES] = og_ref[cb]


def peer_route(h, nw, wq_bf16, subkeys_bf16, tm=512):
    t, d = h.shape
    tm = min(tm, t)
    ncb = tm // LANES
    hq = 2 * PEER_HALF
    return pl.pallas_call(
        functools.partial(_peer_route_body, n_col_blocks=ncb),
        grid=(t // tm, PEER_HEADS),
        in_specs=[pl.BlockSpec((tm, d), lambda i, hd: (i, 0)),
                  pl.BlockSpec((1, d), lambda i, hd: (0, 0)),
                  pl.BlockSpec((d, hq), lambda i, hd: (0, hd)),
                  pl.BlockSpec((1, 2, PEER_N_KEYS, PEER_HALF), lambda i, hd: (hd, 0, 0, 0))],
        out_specs=[pl.BlockSpec((tm, d), lambda i, hd: (i, 0)),
                   pl.BlockSpec((1, PEER_TOPK, tm), lambda i, hd: (hd, 0, i)),
                   pl.BlockSpec((1, PEER_TOPK, tm), lambda i, hd: (hd, 0, i))],
        out_shape=[jax.ShapeDtypeStruct((t, d), F32),
                   jax.ShapeDtypeStruct((PEER_HEADS, PEER_TOPK, t), jnp.int32),
                   jax.ShapeDtypeStruct((PEER_HEADS, PEER_TOPK, t), F32)],
        scratch_shapes=[pltpu.VMEM((tm, d), BF16),
                        pltpu.VMEM((2 * ncb, PEER_N_KEYS, LANES), F32),
                        pltpu.VMEM((ncb, PEER_TOPK, LANES), jnp.int32),
                        pltpu.VMEM((ncb, PEER_TOPK, LANES), F32)],
        compiler_params=_cparams(("parallel", "arbitrary")),
        name="peer_route",
    )(h, nw.reshape(1, d), wq_bf16, subkeys_bf16)


PEER_TOK_BLOCK = 16
PEER_ISSUE_UNROLL = 16
ROW_CHUNKS = D_MODEL // LANES
UV_CHUNKS = 2 * ROW_CHUNKS
ROW_PITCH = UV_CHUNKS + 1


def _start_row_copy(uv_hbm, expert, buf, slot, row, sem, priority):
    pltpu.make_async_copy(uv_hbm.at[expert], buf.at[slot, pl.ds(row * ROW_PITCH, UV_CHUNKS), :],
                          sem.at[slot]).start(priority=priority)


def _wait_slot(buf, slot, sem):
    done = buf.at[slot, pl.ds(0, PEER_TOK_BLOCK * PEER_SEL * UV_CHUNKS), :]
    pltpu.make_async_copy(done, done, sem.at[slot]).wait()


def _peer_expert_body(idx0_ref, idx_next_ref, x_ref, gate_ref, h_ref, uv_hbm, o_ref, buf, sem):
    i = pl.program_id(0)
    n_blocks = pl.num_programs(0)
    tb = PEER_TOK_BLOCK
    slot = i % 2
    nslot = 1 - slot

    @pl.when(i == 0)
    def _prime():
        def rows(ro, carry):
            for ri in range(PEER_ISSUE_UNROLL):
                r = ro * PEER_ISSUE_UNROLL + ri
                _start_row_copy(uv_hbm, idx0_ref[r], buf, 0, r, sem, ri % 2)
            return carry

        lax.fori_loop(0, tb * PEER_SEL // PEER_ISSUE_UNROLL, rows, 0)

    _wait_slot(buf, slot, sem)
    gate_t = gate_ref[...].T
    group = PEER_SEL // UV_CHUNKS

    for tk in range(tb):
        base = tk * PEER_SEL * ROW_PITCH

        def issue(piece):
            for k in range(group):
                r = tk * PEER_SEL + piece * group + k
                _start_row_copy(uv_hbm, idx_next_ref[r], buf, nslot, r, sem, k % 2)

        acc = None
        for c in range(ROW_CHUNKS):
            issue(c)
            u_c = buf[slot, pl.ds(base + c, PEER_SEL, stride=ROW_PITCH), :]
            term = u_c * x_ref[tk, c:c + 1, :]
            acc = term if acc is None else acc + term
        hsel = jnp.sum(acc, axis=1, keepdims=True)
        act = 0.5 * hsel * (1.0 + lax.erf(hsel * (2.0 ** -0.5)))
        w = gate_t[:, tk:tk + 1] * act
        outs = []
        for c in range(ROW_CHUNKS):
            issue(ROW_CHUNKS + c)
            v_c = buf[slot, pl.ds(base + ROW_CHUNKS + c, PEER_SEL, stride=ROW_PITCH), :]
            outs.append(jnp.sum(v_c * w, axis=0, keepdims=True))
        o_ref[tk] = h_ref[tk] + jnp.concatenate(outs, axis=0)

    @pl.when(i == n_blocks - 1)
    def _drain():
        _wait_slot(buf, nslot, sem)


def peer_expert_residual(idx_flat, hn, gate_tok, h, uv_tab):
    t, d = h.shape
    tb = PEER_TOK_BLOCK
    assert t % tb == 0
    n = t // tb
    tok_spec = pl.BlockSpec((tb, ROW_CHUNKS, LANES), lambda i: (i, 0, 0))
    out = pl.pallas_call(
        _peer_expert_body,
        grid=(n,),
        in_specs=[pl.BlockSpec((tb * PEER_SEL,), lambda i: (0,), memory_space=pltpu.SMEM),
                  pl.BlockSpec((tb * PEER_SEL,), lambda i: (jnp.minimum(i + 1, n - 1),),
                               memory_space=pltpu.SMEM),
                  tok_spec,
                  pl.BlockSpec((tb, PEER_SEL), lambda i: (i, 0)),
                  tok_spec,
                  pl.BlockSpec(memory_space=pl.ANY)],
        out_specs=tok_spec,
        out_shape=jax.ShapeDtypeStruct((t, ROW_CHUNKS, LANES), F32),
        scratch_shapes=[pltpu.VMEM((2, tb * PEER_SEL * ROW_PITCH, LANES), F32),
                        pltpu.SemaphoreType.DMA((2,))],
        compiler_params=_cparams(("arbitrary",)),
        name="peer_expert",
    )(idx_flat, idx_flat, hn.reshape(t, ROW_CHUNKS, LANES), gate_tok, h.reshape(t, ROW_CHUNKS, LANES), uv_tab)
    return out.reshape(t, d)


def _rmsnorm_body(x_ref, w_ref, o_ref):
    o_ref[...] = _rms(x_ref[...], w_ref[...])


def rmsnorm(x, w, tm=1024):
    t, d = x.shape
    tm = min(tm, t)
    return pl.pallas_call(
        _rmsnorm_body,
        grid=(t // tm,),
        in_specs=[pl.BlockSpec((tm, d), lambda i: (i, 0)), pl.BlockSpec((1, d), lambda i: (0, 0))],
        out_specs=pl.BlockSpec((tm, d), lambda i: (i, 0)),
        out_shape=jax.ShapeDtypeStruct((t, d), F32),
        compiler_params=_cparams(("parallel",)),
        name="rmsnorm",
    )(x, w.reshape(1, d))


def _pad_cols(v, start, width=SMALL_W):
    out = jnp.zeros((1, width), F32)
    return out.at[0, start:start + v.shape[0]].set(v.astype(F32))


def _even_mix(h, bsz, norm_w, w_in, w_out, ssd_conv_w, ssd_conv_b, ssd_dt_bias, ssd_a_log, ssd_d_skip,
              ssd_norm, ml_conv_w, ml_conv_b, ml_wq, ml_wk, ml_wv, ml_i_bias, ml_f_bias, ml_norm):
    d = D_MODEL
    o1 = d
    o2 = o1 + SSD_XBC
    o3 = o2 + SSD_HEADS
    o4 = o3 + d
    o5 = o4 + d
    o6 = o5 + MLSTM_HEADS
    o7 = o6 + MLSTM_HEADS
    w_small = jnp.concatenate([w_in[:, o2:o3], w_in[:, o5:o6], w_in[:, o6:o7],
                               jnp.zeros((d, SMALL_W - (o3 - o2) - (o7 - o5)), w_in.dtype)], axis=1)
    w_all = jnp.concatenate([w_in[:, :o1], w_in[:, o1:o2], w_in[:, o3:o4], w_in[:, o4:o5], w_small],
                            axis=1).astype(BF16)
    z, xbc, xm, og, small = norm_proj(h, norm_w, w_all, (d, SSD_XBC, d, d, SMALL_W), tm=256)
    bias_pad = (_pad_cols(ssd_dt_bias, SMALL_DT) + _pad_cols(ml_i_bias, SMALL_IG)
                + _pad_cols(ml_f_bias, SMALL_FG))
    y_a = ssd_mixer(z, xbc, small, ssd_conv_w, ssd_conv_b, bias_pad, _pad_cols(ssd_a_log, SMALL_DT),
                    _pad_cols(ssd_d_skip, SMALL_DT), ssd_norm, bsz)
    y_b = mlstm_mixer(xm, og, small, ml_conv_w, ml_conv_b, bias_pad, ml_wq.astype(BF16),
                      ml_wk.astype(BF16), ml_wv.astype(BF16), ml_norm, bsz)
    w_out_b = w_out.astype(BF16)
    return proj_residual([y_a, y_b], [w_out_b[:d], w_out_b[d:]], h)


def _odd_mix(h, bsz, norm_w, w_in, w_out, lb_logits, hg_norm):
    d = D_MODEL
    uq, uf, ui, ug = norm_proj(h, norm_w, w_in.astype(BF16), (d, d, d, d), tm=256)
    y = hgrn_mixer(uq, uf, ui, ug, lb_logits, hg_norm, bsz)
    return proj_residual([y], [w_out.astype(BF16)], h)


def _xattn(h, mem2d, bsz, norm_x, norm_m, wq, wkv, wo):
    k, v = norm_proj(mem2d, norm_m, wkv.astype(BF16), (D_MODEL, D_MODEL), out_dtype=BF16)
    return xattn_residual(h, norm_x, wq.astype(BF16), k, v, wo.astype(BF16), bsz)


def _peer(h, norm_w, wq, subkeys, u_tab, v_tab):
    t = h.shape[0]
    n_exp = u_tab.shape[0]
    hn, idx, gate = peer_route(h, norm_w, wq.astype(BF16), subkeys.astype(BF16))
    idx_flat = idx.reshape(PEER_SEL, t).T.reshape(t * PEER_SEL)
    gate_tok = gate.reshape(PEER_SEL, t).T
    uv_tab = jnp.concatenate([u_tab.reshape(n_exp, ROW_CHUNKS, LANES), v_tab.reshape(n_exp, ROW_CHUNKS, LANES)],
                             axis=1)
    return peer_expert_residual(idx_flat, hn, gate_tok, h, uv_tab)


def kernel(x, mem, norm_mix, norm_xattn, norm_mem, norm_ffn, norm_final, ev_w_in, ev_w_out, ssd_conv_w, ssd_conv_b, ssd_dt_bias, ssd_a_log, ssd_d_skip, ssd_norm, ml_conv_w, ml_conv_b, ml_wq, ml_wk, ml_wv, ml_i_bias, ml_f_bias, ml_norm, od_w_in, od_w_out, hgrn_lb_logits, hgrn_norm, xa_wq, xa_wkv, xa_wo, peer_wq, peer_subkeys, peer_u, peer_v):
    bsz, seq, d = x.shape
    depth = norm_mix.shape[0]
    assert depth == 2 and hgrn_lb_logits.shape[0] == 2
    h = x.reshape(bsz * seq, d)
    mem2d = mem.reshape(bsz * MEM_LEN, d)
    for layer in range(depth):
        if layer % 2 == 0:
            e = layer // 2
            h = _even_mix(h, bsz, norm_mix[layer], ev_w_in[e], ev_w_out[e], ssd_conv_w[e], ssd_conv_b[e],
                          ssd_dt_bias[e], ssd_a_log[e], ssd_d_skip[e], ssd_norm[e], ml_conv_w[e],
                          ml_conv_b[e], ml_wq[e], ml_wk[e], ml_wv[e], ml_i_bias[e], ml_f_bias[e], ml_norm[e])
        else:
            o = layer // 2
            h = _odd_mix(h, bsz, norm_mix[layer], od_w_in[o], od_w_out[o], hgrn_lb_logits, hgrn_norm[o])
        h = _xattn(h, mem2d, bsz, norm_xattn[layer], norm_mem[layer], xa_wq[layer], xa_wkv[layer], xa_wo[layer])
        h = _peer(h, norm_ffn[layer], peer_wq[layer], peer_subkeys[layer], peer_u[layer], peer_v[layer])
    return rmsnorm(h, norm_final).reshape(bsz, seq, d)
```

```python
import functools

import jax
import jax.numpy as jnp
from jax import lax
from jax.experimental import pallas as pl
from jax.experimental.pallas import tpu as pltpu

F32 = jnp.float32
BF16 = jnp.bfloat16
HIGHEST = lax.Precision.HIGHEST

D_MODEL = 1024
EPS = 1e-6
CONV_WIDTH = 4
MEM_LEN = 256
SSD_HEAD_DIM = 64
SSD_HEADS = 16
SSD_GROUPS = 2
SSD_STATE = 128
SSD_XBC = D_MODEL + 2 * SSD_GROUPS * SSD_STATE
SSD_CHUNK = 128
MLSTM_HEADS = 4
MLSTM_HEAD_DIM = 256
MLSTM_CHUNK = 128
HGRN_HEADS = 8
HGRN_HEAD_DIM = 128
HGRN_CHUNK = 64
XATTN_HEADS = 4
XATTN_HEAD_DIM = 256
PEER_HEADS = 8
PEER_N_KEYS = 128
PEER_TOPK = 16
PEER_HALF = 128
PEER_SEL = PEER_HEADS * PEER_TOPK

SUBLANES = 8
LANES = 128
SMALL_W = LANES
SMALL_DT = 0
SMALL_IG = SSD_HEADS
SMALL_FG = SSD_HEADS + MLSTM_HEADS

VMEM_LIMIT = 52 * 1024 * 1024


def _cparams(sem):
    return pltpu.CompilerParams(dimension_semantics=sem, vmem_limit_bytes=VMEM_LIMIT)


def _sigmoid(x):
    return 1.0 / (1.0 + jnp.exp(-x))


def _softplus(x):
    return jnp.maximum(x, 0.0) + jnp.log(1.0 + jnp.exp(-jnp.abs(x)))


def _rms(x, w):
    return x * lax.rsqrt(jnp.mean(x * x, axis=-1, keepdims=True) + EPS) * w


def _dot(a, b):
    return jnp.dot(a.astype(BF16), b.astype(BF16), preferred_element_type=F32)


def _dot_nt(a, b):
    return lax.dot_general(a.astype(BF16), b.astype(BF16), (((1,), (1,)), ((), ())),
                           preferred_element_type=F32)


def _cumsum_rows(x):
    n = x.shape[0]
    r = lax.broadcasted_iota(jnp.int32, (n, n), 0)
    c = lax.broadcasted_iota(jnp.int32, (n, n), 1)
    tril = (r >= c).astype(F32)
    return jnp.dot(tril, x, precision=HIGHEST, preferred_element_type=F32)


def _causal(n):
    r = lax.broadcasted_iota(jnp.int32, (n, n), 0)
    c = lax.broadcasted_iota(jnp.int32, (n, n), 1)
    return r >= c


def _norm_proj_body(x_ref, nw_ref, w_ref, *out_refs, splits):
    xn = _rms(x_ref[...], nw_ref[...]).astype(BF16)
    off = 0
    for o_ref, n in zip(out_refs, splits):
        o_ref[...] = jnp.dot(xn, w_ref[:, off:off + n], preferred_element_type=F32).astype(o_ref.dtype)
        off += n


def norm_proj(x, nw, w_bf16, splits, tm=512, out_dtype=F32):
    t, d = x.shape
    n = w_bf16.shape[1]
    assert sum(splits) == n
    tm = min(tm, t)
    assert t % tm == 0
    return pl.pallas_call(
        functools.partial(_norm_proj_body, splits=tuple(splits)),
        grid=(t // tm,),
        in_specs=[pl.BlockSpec((tm, d), lambda i: (i, 0)),
                  pl.BlockSpec((1, d), lambda i: (0, 0)),
                  pl.BlockSpec((d, n), lambda i: (0, 0))],
        out_specs=[pl.BlockSpec((tm, s), lambda i: (i, 0)) for s in splits],
        out_shape=[jax.ShapeDtypeStruct((t, s), out_dtype) for s in splits],
        compiler_params=_cparams(("parallel",)),
        name="norm_proj",
    )(x, nw.reshape(1, d), w_bf16)


def _proj_residual_body(*refs, n_parts):
    a_refs = refs[:n_parts]
    w_refs = refs[n_parts:2 * n_parts]
    r_ref = refs[2 * n_parts]
    o_ref = refs[2 * n_parts + 1]
    acc = r_ref[...]
    for a_ref, w_ref in zip(a_refs, w_refs):
        acc = acc + jnp.dot(a_ref[...].astype(BF16), w_ref[...], preferred_element_type=F32)
    o_ref[...] = acc


def proj_residual(parts, ws_bf16, res, tm=512):
    t, d = res.shape
    tm = min(tm, t)
    n_parts = len(parts)
    in_specs = ([pl.BlockSpec((tm, p.shape[1]), lambda i: (i, 0)) for p in parts]
                + [pl.BlockSpec(w.shape, lambda i: (0, 0)) for w in ws_bf16]
                + [pl.BlockSpec((tm, d), lambda i: (i, 0))])
    return pl.pallas_call(
        functools.partial(_proj_residual_body, n_parts=n_parts),
        grid=(t // tm,),
        in_specs=in_specs,
        out_specs=pl.BlockSpec((tm, d), lambda i: (i, 0)),
        out_shape=jax.ShapeDtypeStruct((t, d), F32),
        compiler_params=_cparams(("parallel",)),
        name="proj_residual",
    )(*parts, *ws_bf16, res)


def _conv_silu(x_in, xpad_ref, cw_ref, cb_ref, n_rows):
    xpad_ref[SUBLANES:SUBLANES + n_rows, :] = x_in
    acc = cb_ref[...] + cw_ref[CONV_WIDTH - 1:CONV_WIDTH, :] * x_in
    for k in range(CONV_WIDTH - 1):
        start = SUBLANES - (CONV_WIDTH - 1) + k
        acc = acc + cw_ref[k:k + 1, :] * xpad_ref[start:start + n_rows, :]
    xpad_ref[0:SUBLANES, :] = x_in[n_rows - SUBLANES:n_rows, :]
    return acc * _sigmoid(acc)


def _ssd_body(z_ref, xbc_ref, sm_ref, cw_ref, cb_ref, bias_ref, alog_ref, dsk_ref, nw_ref,
              o_ref, xpad_ref, state_ref, ybuf_ref):
    L, P, N = SSD_CHUNK, SSD_HEAD_DIM, SSD_STATE
    heads_per_group = SSD_HEADS // SSD_GROUPS

    @pl.when(pl.program_id(1) == 0)
    def _init():
        xpad_ref[0:SUBLANES, :] = jnp.zeros((SUBLANES, SSD_XBC), F32)
        state_ref[...] = jnp.zeros_like(state_ref)

    xa = _conv_silu(xbc_ref[...], xpad_ref, cw_ref, cb_ref, L)
    dt = _softplus(sm_ref[...] + bias_ref[...])
    a = -jnp.exp(alog_ref[...])
    cs_col = _cumsum_rows(dt * a)
    cs_row = cs_col.T
    causal = _causal(L)

    for g in range(SSD_GROUPS):
        b_g = xa[:, D_MODEL + g * N:D_MODEL + (g + 1) * N]
        c_g = xa[:, D_MODEL + (SSD_GROUPS + g) * N:D_MODEL + (SSD_GROUPS + g + 1) * N]
        cb = _dot_nt(c_g, b_g)
        b_gt = b_g.T.astype(BF16)
        c_gb = c_g.astype(BF16)
        for e in range(heads_per_group):
            he = g * heads_per_group + e
            col = SMALL_DT + he
            cs_c = cs_col[:, col:col + 1]
            cs_r = cs_row[col:col + 1, :]
            cs_last = cs_col[L - 1:L, col:col + 1]
            decay = jnp.exp(jnp.where(causal, cs_c - cs_r, -jnp.inf))
            x_e = xa[:, he * P:(he + 1) * P]
            x_dt = x_e * dt[:, col:col + 1]
            y = _dot(cb * decay, x_dt)
            s_prev = state_ref[he]
            y = y + jnp.dot(c_gb, s_prev.astype(BF16), preferred_element_type=F32) * jnp.exp(cs_c)
            y = y + x_e * dsk_ref[:, col:col + 1]
            ybuf_ref[:, he * P:(he + 1) * P] = y
            x_end = (x_dt * jnp.exp(cs_last - cs_c)).astype(BF16)
            state_ref[he] = jnp.exp(cs_last) * s_prev + jnp.dot(b_gt, x_end, preferred_element_type=F32)

    zz = z_ref[...]
    y = ybuf_ref[...] * (zz * _sigmoid(zz))
    gw = D_MODEL // SSD_GROUPS
    for g in range(SSD_GROUPS):
        o_ref[:, g * gw:(g + 1) * gw] = _rms(y[:, g * gw:(g + 1) * gw], nw_ref[:, g * gw:(g + 1) * gw])


def ssd_mixer(z, xbc, small, conv_w, conv_b, bias_pad, alog_pad, dskip_pad, norm_w, bsz):
    t = z.shape[0]
    L = SSD_CHUNK
    nc = t // bsz // L
    row = lambda b, c: (b * nc + c, 0)
    const = lambda b, c: (0, 0)
    return pl.pallas_call(
        _ssd_body,
        grid=(bsz, nc),
        in_specs=[pl.BlockSpec((L, D_MODEL), row),
                  pl.BlockSpec((L, SSD_XBC), row),
                  pl.BlockSpec((L, SMALL_W), row),
                  pl.BlockSpec((CONV_WIDTH, SSD_XBC), const),
                  pl.BlockSpec((1, SSD_XBC), const),
                  pl.BlockSpec((1, SMALL_W), const),
                  pl.BlockSpec((1, SMALL_W), const),
                  pl.BlockSpec((1, SMALL_W), const),
                  pl.BlockSpec((1, D_MODEL), const)],
        out_specs=pl.BlockSpec((L, D_MODEL), row),
        out_shape=jax.ShapeDtypeStruct((t, D_MODEL), F32),
        scratch_shapes=[pltpu.VMEM((L + SUBLANES, SSD_XBC), F32),
                        pltpu.VMEM((SSD_HEADS, SSD_STATE, SSD_HEAD_DIM), F32),
                        pltpu.VMEM((L, D_MODEL), F32)],
        compiler_params=_cparams(("parallel", "arbitrary")),
        name="ssd_mixer",
    )(z, xbc, small, conv_w, conv_b.reshape(1, -1), bias_pad, alog_pad, dskip_pad, norm_w.reshape(1, -1))


def _mlstm_body(xm_ref, og_ref, sm_ref, cw_ref, cb_ref, bias_ref, wq_ref, wk_ref, wv_ref, nw_ref,
                o_ref, xpad_ref, c_ref, n_ref, m_ref):
    L, H, Dh = MLSTM_CHUNK, MLSTM_HEADS, MLSTM_HEAD_DIM

    @pl.when(pl.program_id(1) == 0)
    def _init():
        xpad_ref[0:SUBLANES, :] = jnp.zeros((SUBLANES, D_MODEL), F32)
        c_ref[...] = jnp.zeros_like(c_ref)
        n_ref[...] = jnp.zeros_like(n_ref)
        m_ref[...] = jnp.zeros_like(m_ref)

    xm = xm_ref[...]
    xc = _conv_silu(xm, xpad_ref, cw_ref, cb_ref, L)
    smb = sm_ref[...] + bias_ref[...]
    logf = jnp.minimum(smb, 0.0) - jnp.log(1.0 + jnp.exp(-jnp.abs(smb)))
    bcum_col = _cumsum_rows(logf)
    bcum_row = bcum_col.T
    smb_row = smb.T
    causal = _causal(L)

    for h in range(H):
        ci, cf = SMALL_IG + h, SMALL_FG + h
        bc = bcum_col[:, cf:cf + 1]
        br = bcum_row[cf:cf + 1, :]
        ic = smb[:, ci:ci + 1]
        ir = smb_row[ci:ci + 1, :]
        b_end = bcum_col[L - 1:L, cf:cf + 1]
        dlog = jnp.where(causal, bc - br + ir, -jnp.inf)
        m_intra = jnp.max(dlog, axis=1, keepdims=True)
        g_end = b_end - bc + ic
        m_chunk = jnp.max(g_end, axis=0, keepdims=True)
        w_end = jnp.exp(g_end - m_chunk)

        sl = slice(h * Dh, (h + 1) * Dh)
        q = _dot(xc[:, sl], wq_ref[h])
        k = _dot(xc[:, sl], wk_ref[h]) * (Dh ** -0.5)
        v = _dot(xm[:, sl], wv_ref[h])
        qb, kb, vb = q.astype(BF16), k.astype(BF16), v.astype(BF16)

        m_prev = m_ref[h:h + 1, 0:1]
        c_prev = c_ref[h]
        n_prev = n_ref[h:h + 1, :]
        m_t = jnp.maximum(bc + m_prev, m_intra)
        a_t = jnp.exp(bc + m_prev - m_t)
        qk = _dot_nt(qb, kb) * jnp.exp(dlog - m_t)
        num = a_t * jnp.dot(qb, c_prev.astype(BF16), preferred_element_type=F32) \
            + jnp.dot(qk.astype(BF16), vb, preferred_element_type=F32)
        den = a_t * jnp.sum(q * n_prev, axis=1, keepdims=True) + jnp.sum(qk, axis=1, keepdims=True)
        hid = num / jnp.maximum(jnp.abs(den), jnp.exp(-m_t))
        hid = _rms(hid, nw_ref[:, sl])
        og = og_ref[:, sl]
        o_ref[:, sl] = hid * _sigmoid(og)

        m_new = jnp.maximum(b_end + m_prev, m_chunk)
        a_old = jnp.exp(b_end + m_prev - m_new)
        a_new = jnp.exp(m_chunk - m_new)
        kw = k * w_end
        c_local = jnp.dot(kw.T.astype(BF16), vb, preferred_element_type=F32)
        c_ref[h] = a_old * c_prev + a_new * c_local
        n_ref[h:h + 1, :] = a_old * n_prev + a_new * jnp.sum(kw, axis=0, keepdims=True)
        m_ref[h:h + 1, :] = jnp.broadcast_to(m_new, (1, LANES))


def mlstm_mixer(xm, og, small, conv_w, conv_b, bias_pad, wq, wk, wv, norm_w, bsz):
    t = xm.shape[0]
    L = MLSTM_CHUNK
    nc = t // bsz // L
    row = lambda b, c: (b * nc + c, 0)
    const = lambda b, c: (0, 0)
    const3 = lambda b, c: (0, 0, 0)
    wspec = pl.BlockSpec((MLSTM_HEADS, MLSTM_HEAD_DIM, MLSTM_HEAD_DIM), const3)
    return pl.pallas_call(
        _mlstm_body,
        grid=(bsz, nc),
        in_specs=[pl.BlockSpec((L, D_MODEL), row),
                  pl.BlockSpec((L, D_MODEL), row),
                  pl.BlockSpec((L, SMALL_W), row),
                  pl.BlockSpec((CONV_WIDTH, D_MODEL), const),
                  pl.BlockSpec((1, D_MODEL), const),
                  pl.BlockSpec((1, SMALL_W), const),
                  wspec, wspec, wspec,
                  pl.BlockSpec((1, D_MODEL), const)],
        out_specs=pl.BlockSpec((L, D_MODEL), row),
        out_shape=jax.ShapeDtypeStruct((t, D_MODEL), F32),
        scratch_shapes=[pltpu.VMEM((L + SUBLANES, D_MODEL), F32),
                        pltpu.VMEM((MLSTM_HEADS, MLSTM_HEAD_DIM, MLSTM_HEAD_DIM), F32),
                        pltpu.VMEM((SUBLANES, MLSTM_HEAD_DIM), F32),
                        pltpu.VMEM((SUBLANES, LANES), F32)],
        compiler_params=_cparams(("parallel", "arbitrary")),
        name="mlstm_mixer",
    )(xm, og, small, conv_w, conv_b.reshape(1, -1), bias_pad, wq, wk, wv, norm_w.reshape(1, -1))


def _hgrn_body(uq_ref, uf_ref, ui_ref, ug_ref, lbl_ref, nw_ref, o_ref, s_ref):
    L, H, Dk = HGRN_CHUNK, HGRN_HEADS, HGRN_HEAD_DIM

    @pl.when(pl.program_id(1) == 0)
    def _init():
        s_ref[...] = jnp.zeros_like(s_ref)

    lg = lbl_ref[...]
    lmax = jnp.max(lg, axis=0, keepdims=True)
    le = jnp.exp(lg - lmax)
    lb = le[0:1, :] / jnp.sum(le, axis=0, keepdims=True)

    uf = uf_ref[...]
    uq = uq_ref[...]
    logf = jnp.log(lb + (1.0 - lb) * _sigmoid(uf))
    k = (1.0 - lb) * _sigmoid(-uf)
    q = uq * _sigmoid(uq)
    v = ui_ref[...]
    gc = _cumsum_rows(logf)
    g_ref = gc[L // 2:L // 2 + 1, :]
    g_last = gc[L - 1:L, :]
    q_in = (q * jnp.exp(gc - g_ref)).astype(BF16)
    k_in = (k * jnp.exp(g_ref - gc)).astype(BF16)
    q_off = (q * jnp.exp(gc)).astype(BF16)
    k_end = k * jnp.exp(g_last - gc)
    decay = jnp.exp(g_last)
    causal = _causal(L)
    vb = v.astype(BF16)
    ug = ug_ref[...]

    for h in range(H):
        sl = slice(h * Dk, (h + 1) * Dk)
        att = jnp.where(causal, _dot_nt(q_in[:, sl], k_in[:, sl]), 0.0)
        st_prev = s_ref[h]
        o = jnp.dot(att.astype(BF16), vb[:, sl], preferred_element_type=F32) \
            + _dot_nt(q_off[:, sl], st_prev)
        s_ref[h] = decay[:, sl] * st_prev + _dot(v[:, sl].T, k_end[:, sl])
        g = ug[:, sl]
        o_ref[:, sl] = _rms(o, nw_ref[:, sl]) * (g * _sigmoid(g))


def hgrn_mixer(uq, uf, ui, ug, lb_logits, norm_w, bsz):
    t = uq.shape[0]
    L = HGRN_CHUNK
    nc = t // bsz // L
    row = lambda b, c: (b * nc + c, 0)
    const = lambda b, c: (0, 0)
    rs = pl.BlockSpec((L, D_MODEL), row)
    return pl.pallas_call(
        _hgrn_body,
        grid=(bsz, nc),
        in_specs=[rs, rs, rs, rs,
                  pl.BlockSpec((2, D_MODEL), const),
                  pl.BlockSpec((1, D_MODEL), const)],
        out_specs=rs,
        out_shape=jax.ShapeDtypeStruct((t, D_MODEL), F32),
        scratch_shapes=[pltpu.VMEM((HGRN_HEADS, HGRN_HEAD_DIM, HGRN_HEAD_DIM), F32)],
        compiler_params=_cparams(("parallel", "arbitrary")),
        name="hgrn_mixer",
    )(uq, uf, ui, ug, lb_logits, norm_w.reshape(1, -1))


def _xattn_body(h_ref, nw_ref, wq_ref, k_ref, v_ref, wo_ref, o_ref, obuf_ref):
    hv = h_ref[...]
    hn = _rms(hv, nw_ref[...]).astype(BF16)
    q = jnp.dot(hn, wq_ref[...], preferred_element_type=F32)
    scale = XATTN_HEAD_DIM ** -0.5
    for a in range(XATTN_HEADS):
        sl = slice(a * XATTN_HEAD_DIM, (a + 1) * XATTN_HEAD_DIM)
        s = _dot_nt(q[:, sl], k_ref[:, sl]) * scale
        s = s - jnp.max(s, axis=1, keepdims=True)
        p = jnp.exp(s)
        p = p / jnp.sum(p, axis=1, keepdims=True)
        obuf_ref[:, sl] = jnp.dot(p.astype(BF16), v_ref[:, sl], preferred_element_type=F32).astype(BF16)
    o_ref[...] = hv + jnp.dot(obuf_ref[...], wo_ref[...], preferred_element_type=F32)


def xattn_residual(h, nw, wq_bf16, k_bf16, v_bf16, wo_bf16, bsz, tm=512):
    t, d = h.shape
    seq = t // bsz
    tm = min(tm, seq)
    nt = seq // tm
    row = lambda b, i: (b * nt + i, 0)
    const = lambda b, i: (0, 0)
    mem = lambda b, i: (b, 0)
    return pl.pallas_call(
        _xattn_body,
        grid=(bsz, nt),
        in_specs=[pl.BlockSpec((tm, d), row),
                  pl.BlockSpec((1, d), const),
                  pl.BlockSpec((d, d), const),
                  pl.BlockSpec((MEM_LEN, d), mem),
                  pl.BlockSpec((MEM_LEN, d), mem),
                  pl.BlockSpec((d, d), const)],
        out_specs=pl.BlockSpec((tm, d), row),
        out_shape=jax.ShapeDtypeStruct((t, d), F32),
        scratch_shapes=[pltpu.VMEM((tm, d), BF16)],
        compiler_params=_cparams(("parallel", "parallel")),
        name="xattn_residual",
    )(h, nw.reshape(1, d), wq_bf16, k_bf16, v_bf16, wo_bf16)


def _topk_rows(s, k):
    rows = s.shape[0]
    iota = lax.broadcasted_iota(jnp.int32, s.shape, 0)
    vals, idxs = [], []
    for _ in range(k):
        m = jnp.max(s, axis=0, keepdims=True)
        i = jnp.min(jnp.where(s == m, iota, rows), axis=0, keepdims=True)
        vals.append(m)
        idxs.append(i)
        s = jnp.where(iota == i, -jnp.inf, s)
    return jnp.concatenate(vals, axis=0), jnp.concatenate(idxs, axis=0)


def _select_rows(table, sel, n):
    out = jnp.zeros(sel.shape, jnp.int32)
    for a in range(n):
        out = jnp.where(sel == a, table[a:a + 1, :], out)
    return out


def _peer_route_body(h_ref, nw_ref, wq_ref, sk_ref, hn_ref, idx_ref, gate_ref,
                     hnb_ref, st_ref, oi_ref, og_ref, *, n_col_blocks):
    K = PEER_TOPK
    head = pl.program_id(1)

    @pl.when(head == 0)
    def _norm():
        xn = _rms(h_ref[...], nw_ref[...])
        hn_ref[...] = xn
        hnb_ref[...] = xn.astype(BF16)

    q = jnp.dot(hnb_ref[...], wq_ref[...], preferred_element_type=F32)
    for p in range(2):
        qp = q[:, p * PEER_HALF:(p + 1) * PEER_HALF]
        st = _dot_nt(sk_ref[0, p], qp)
        for cb in range(n_col_blocks):
            st_ref[p * n_col_blocks + cb] = st[:, cb * LANES:(cb + 1) * LANES]

    def col_block(cb, carry):
        v1, i1 = _topk_rows(st_ref[cb], K)
        v2, i2 = _topk_rows(st_ref[n_col_blocks + cb], K)
        sub = lax.broadcasted_iota(jnp.int32, (SUBLANES, LANES), 0)
        blocks = [v1[0:1, :] + v2[0:SUBLANES, :], v1[0:1, :] + v2[SUBLANES:K, :]]
        for a in range(1, SUBLANES):
            blocks.append(jnp.where(sub < K // (a + 1), v1[a:a + 1, :] + v2[0:SUBLANES, :], -jnp.inf))
        blocks.append(v1[SUBLANES:K, :] + v2[0:1, :])
        cand = jnp.concatenate(blocks, axis=0)
        top_s, pos = _topk_rows(cand, K)
        blk = lax.shift_right_logical(pos, 3)
        within = lax.bitwise_and(pos, SUBLANES - 1)
        a_sel = jnp.where(blk <= 1, 0, jnp.where(blk <= SUBLANES, blk - 1, within + SUBLANES))
        b_sel = jnp.where(blk == 1, within + SUBLANES, jnp.where(blk <= SUBLANES, within, 0))
        e1 = _select_rows(i1, a_sel, K)
        e2 = _select_rows(i2, b_sel, K)
        oi_ref[cb, pl.ds(row0, K), :] = e1 * PEER_N_KEYS + e2
        ex = jnp.exp(top_s - jnp.max(top_s, axis=0, keepdims=True))
        og_ref[cb, pl.ds(row0, K), :] = ex / jnp.sum(ex, axis=0, keepdims=True)
        return carry

    row0 = pl.multiple_of(head * K, K)
    lax.fori_loop(0, n_col_blocks, col_block, 0)

    @pl.when(head == PEER_HEADS - 1)
    def _emit():
        for cb in range(n_col_blocks):
            idx_ref[cb * LANES:(cb + 1) * LANES, :] = oi_ref[cb].T
            gate_ref[cb * LANES:(cb + 1) * LANES, :] = og_ref[cb].T


def peer_route(h, nw, wq_bf16, subkeys_bf16, tm=512):
    t, d = h.shape
    tm = min(tm, t)
    ncb = tm // LANES
    hq = 2 * PEER_HALF
    return pl.pallas_call(
        functools.partial(_peer_route_body, n_col_blocks=ncb),
        grid=(t // tm, PEER_HEADS),
        in_specs=[pl.BlockSpec((tm, d), lambda i, hd: (i, 0)),
                  pl.BlockSpec((1, d), lambda i, hd: (0, 0)),
                  pl.BlockSpec((d, hq), lambda i, hd: (0, hd)),
                  pl.BlockSpec((1, 2, PEER_N_KEYS, PEER_HALF), lambda i, hd: (hd, 0, 0, 0))],
        out_specs=[pl.BlockSpec((tm, d), lambda i, hd: (i, 0)),
                   pl.BlockSpec((tm, PEER_SEL), lambda i, hd: (i, 0)),
                   pl.BlockSpec((tm, PEER_SEL), lambda i, hd: (i, 0))],
        out_shape=[jax.ShapeDtypeStruct((t, d), F32),
                   jax.ShapeDtypeStruct((t, PEER_SEL), jnp.int32),
                   jax.ShapeDtypeStruct((t, PEER_SEL), F32)],
        scratch_shapes=[pltpu.VMEM((tm, d), BF16),
                        pltpu.VMEM((2 * ncb, PEER_N_KEYS, LANES), F32),
                        pltpu.VMEM((ncb, PEER_SEL, LANES), jnp.int32),
                        pltpu.VMEM((ncb, PEER_SEL, LANES), F32)],
        compiler_params=_cparams(("parallel", "arbitrary")),
        name="peer_route",
    )(h, nw.reshape(1, d), wq_bf16, subkeys_bf16)


PEER_TOK_BLOCK = 16
PEER_ISSUE_UNROLL = 16
ROW_CHUNKS = D_MODEL // LANES
UV_CHUNKS = ROW_CHUNKS
PIECES = 2 * ROW_CHUNKS
ROW_PITCH = UV_CHUNKS + 1
U_MASK = 0xFFFF0000


def _start_row_copy(uv_hbm, expert, buf, slot, row, sem, priority):
    pltpu.make_async_copy(uv_hbm.at[expert], buf.at[slot, pl.ds(row * ROW_PITCH, UV_CHUNKS), :],
                          sem.at[slot]).start(priority=priority)


def _wait_slot(buf, slot, sem):
    done = buf.at[slot, pl.ds(0, PEER_TOK_BLOCK * PEER_SEL * UV_CHUNKS), :]
    pltpu.make_async_copy(done, done, sem.at[slot]).wait()


def _peer_expert_body(idx0_ref, idx_next_ref, x_ref, gate_ref, h_ref, uv_hbm, o_ref, buf, sem):
    i = pl.program_id(0)
    n_blocks = pl.num_programs(0)
    tb = PEER_TOK_BLOCK
    slot = i % 2
    nslot = 1 - slot

    @pl.when(i == 0)
    def _prime():
        def rows(ro, carry):
            for ri in range(PEER_ISSUE_UNROLL):
                r = ro * PEER_ISSUE_UNROLL + ri
                _start_row_copy(uv_hbm, idx0_ref[r], buf, 0, r, sem, ri % 2)
            return carry

        lax.fori_loop(0, tb * PEER_SEL // PEER_ISSUE_UNROLL, rows, 0)

    _wait_slot(buf, slot, sem)
    gate_t = gate_ref[...].T
    group = PEER_SEL // PIECES

    for tk in range(tb):
        base = tk * PEER_SEL * ROW_PITCH

        def issue(piece):
            for k in range(group):
                r = tk * PEER_SEL + piece * group + k
                _start_row_copy(uv_hbm, idx_next_ref[r], buf, nslot, r, sem, k % 2)

        acc = None
        for c in range(ROW_CHUNKS):
            issue(c)
            uv_c = buf[slot, pl.ds(base + c, PEER_SEL, stride=ROW_PITCH), :]
            u_c = pltpu.bitcast(lax.bitwise_and(uv_c, jnp.uint32(U_MASK)), F32)
            term = u_c * x_ref[tk, c:c + 1, :]
            acc = term if acc is None else acc + term
        hsel = jnp.sum(acc, axis=1, keepdims=True)
        act = 0.5 * hsel * (1.0 + lax.erf(hsel * (2.0 ** -0.5)))
        w = gate_t[:, tk:tk + 1] * act
        outs = []
        for c in range(ROW_CHUNKS):
            issue(ROW_CHUNKS + c)
            uv_c = buf[slot, pl.ds(base + c, PEER_SEL, stride=ROW_PITCH), :]
            v_c = pltpu.bitcast(lax.shift_left(uv_c, jnp.uint32(16)), F32)
            outs.append(jnp.sum(v_c * w, axis=0, keepdims=True))
        o_ref[tk] = h_ref[tk] + jnp.concatenate(outs, axis=0)

    @pl.when(i == n_blocks - 1)
    def _drain():
        _wait_slot(buf, nslot, sem)


def peer_expert_residual(idx_flat, hn, gate_tok, h, uv_tab):
    t, d = h.shape
    tb = PEER_TOK_BLOCK
    assert t % tb == 0
    n = t // tb
    tok_spec = pl.BlockSpec((tb, ROW_CHUNKS, LANES), lambda i: (i, 0, 0))
    out = pl.pallas_call(
        _peer_expert_body,
        grid=(n,),
        in_specs=[pl.BlockSpec((tb * PEER_SEL,), lambda i: (0,), memory_space=pltpu.SMEM),
                  pl.BlockSpec((tb * PEER_SEL,), lambda i: (jnp.minimum(i + 1, n - 1),),
                               memory_space=pltpu.SMEM),
                  tok_spec,
                  pl.BlockSpec((tb, PEER_SEL), lambda i: (i, 0)),
                  tok_spec,
                  pl.BlockSpec(memory_space=pl.ANY)],
        out_specs=tok_spec,
        out_shape=jax.ShapeDtypeStruct((t, ROW_CHUNKS, LANES), F32),
        scratch_shapes=[pltpu.VMEM((2, tb * PEER_SEL * ROW_PITCH, LANES), jnp.uint32),
                        pltpu.SemaphoreType.DMA((2,))],
        compiler_params=_cparams(("arbitrary",)),
        name="peer_expert",
    )(idx_flat, idx_flat, hn.reshape(t, ROW_CHUNKS, LANES), gate_tok, h.reshape(t, ROW_CHUNKS, LANES), uv_tab)
    return out.reshape(t, d)


def _rmsnorm_body(x_ref, w_ref, o_ref):
    o_ref[...] = _rms(x_ref[...], w_ref[...])


def rmsnorm(x, w, tm=1024):
    t, d = x.shape
    tm = min(tm, t)
    return pl.pallas_call(
        _rmsnorm_body,
        grid=(t // tm,),
        in_specs=[pl.BlockSpec((tm, d), lambda i: (i, 0)), pl.BlockSpec((1, d), lambda i: (0, 0))],
        out_specs=pl.BlockSpec((tm, d), lambda i: (i, 0)),
        out_shape=jax.ShapeDtypeStruct((t, d), F32),
        compiler_params=_cparams(("parallel",)),
        name="rmsnorm",
    )(x, w.reshape(1, d))


def _pad_cols(v, start, width=SMALL_W):
    out = jnp.zeros((1, width), F32)
    return out.at[0, start:start + v.shape[0]].set(v.astype(F32))


def _even_mix(h, bsz, norm_w, w_in, w_out, ssd_conv_w, ssd_conv_b, ssd_dt_bias, ssd_a_log, ssd_d_skip,
              ssd_norm, ml_conv_w, ml_conv_b, ml_wq, ml_wk, ml_wv, ml_i_bias, ml_f_bias, ml_norm):
    d = D_MODEL
    o1 = d
    o2 = o1 + SSD_XBC
    o3 = o2 + SSD_HEADS
    o4 = o3 + d
    o5 = o4 + d
    o6 = o5 + MLSTM_HEADS
    o7 = o6 + MLSTM_HEADS
    w_small = jnp.concatenate([w_in[:, o2:o3], w_in[:, o5:o6], w_in[:, o6:o7],
                               jnp.zeros((d, SMALL_W - (o3 - o2) - (o7 - o5)), w_in.dtype)], axis=1)
    w_all = jnp.concatenate([w_in[:, :o1], w_in[:, o1:o2], w_in[:, o3:o4], w_in[:, o4:o5], w_small],
                            axis=1).astype(BF16)
    z, xbc, xm, og, small = norm_proj(h, norm_w, w_all, (d, SSD_XBC, d, d, SMALL_W), tm=256)
    bias_pad = (_pad_cols(ssd_dt_bias, SMALL_DT) + _pad_cols(ml_i_bias, SMALL_IG)
                + _pad_cols(ml_f_bias, SMALL_FG))
    y_a = ssd_mixer(z, xbc, small, ssd_conv_w, ssd_conv_b, bias_pad, _pad_cols(ssd_a_log, SMALL_DT),
                    _pad_cols(ssd_d_skip, SMALL_DT), ssd_norm, bsz)
    y_b = mlstm_mixer(xm, og, small, ml_conv_w, ml_conv_b, bias_pad, ml_wq.astype(BF16),
                      ml_wk.astype(BF16), ml_wv.astype(BF16), ml_norm, bsz)
    w_out_b = w_out.astype(BF16)
    return proj_residual([y_a, y_b], [w_out_b[:d], w_out_b[d:]], h)


def _odd_mix(h, bsz, norm_w, w_in, w_out, lb_logits, hg_norm):
    d = D_MODEL
    uq, uf, ui, ug = norm_proj(h, norm_w, w_in.astype(BF16), (d, d, d, d), tm=256)
    y = hgrn_mixer(uq, uf, ui, ug, lb_logits, hg_norm, bsz)
    return proj_residual([y], [w_out.astype(BF16)], h)


def _xattn(h, mem2d, bsz, norm_x, norm_m, wq, wkv, wo):
    k, v = norm_proj(mem2d, norm_m, wkv.astype(BF16), (D_MODEL, D_MODEL), out_dtype=BF16)
    return xattn_residual(h, norm_x, wq.astype(BF16), k, v, wo.astype(BF16), bsz)


def _peer(h, norm_w, wq, subkeys, u_tab, v_tab):
    t = h.shape[0]
    n_exp = u_tab.shape[0]
    hn, idx, gate = peer_route(h, norm_w, wq.astype(BF16), subkeys.astype(BF16))
    idx_flat = idx.reshape(t * PEER_SEL)
    u_bits = lax.bitcast_convert_type(u_tab.astype(BF16), jnp.uint16).astype(jnp.uint32)
    v_bits = lax.bitcast_convert_type(v_tab.astype(BF16), jnp.uint16).astype(jnp.uint32)
    uv_tab = ((u_bits << 16) | v_bits).reshape(n_exp, ROW_CHUNKS, LANES)
    return peer_expert_residual(idx_flat, hn, gate, h, uv_tab)


def kernel(x, mem, norm_mix, norm_xattn, norm_mem, norm_ffn, norm_final, ev_w_in, ev_w_out, ssd_conv_w, ssd_conv_b, ssd_dt_bias, ssd_a_log, ssd_d_skip, ssd_norm, ml_conv_w, ml_conv_b, ml_wq, ml_wk, ml_wv, ml_i_bias, ml_f_bias, ml_norm, od_w_in, od_w_out, hgrn_lb_logits, hgrn_norm, xa_wq, xa_wkv, xa_wo, peer_wq, peer_subkeys, peer_u, peer_v):
    bsz, seq, d = x.shape
    depth = norm_mix.shape[0]
    assert depth == 2 and hgrn_lb_logits.shape[0] == 2
    h = x.reshape(bsz * seq, d)
    mem2d = mem.reshape(bsz * MEM_LEN, d)
    for layer in range(depth):
        if layer % 2 == 0:
            e = layer // 2
            h = _even_mix(h, bsz, norm_mix[layer], ev_w_in[e], ev_w_out[e], ssd_conv_w[e], ssd_conv_b[e],
                          ssd_dt_bias[e], ssd_a_log[e], ssd_d_skip[e], ssd_norm[e], ml_conv_w[e],
                          ml_conv_b[e], ml_wq[e], ml_wk[e], ml_wv[e], ml_i_bias[e], ml_f_bias[e], ml_norm[e])
        else:
            o = layer // 2
            h = _odd_mix(h, bsz, norm_mix[layer], od_w_in[o], od_w_out[o], hgrn_lb_logits, hgrn_norm[o])
        h = _xattn(h, mem2d, bsz, norm_xattn[layer], norm_mem[layer], xa_wq[layer], xa_wkv[layer], xa_wo[layer])
        h = _peer(h, norm_ffn[layer], peer_wq[layer], peer_subkeys[layer], peer_u[layer], peer_v[layer])
    return rmsnorm(h, norm_final).reshape(bsz, seq, d)
```

```python
import functools

import jax
import jax.numpy as jnp
from jax import lax
from jax.experimental import pallas as pl
from jax.experimental.pallas import tpu as pltpu

F32 = jnp.float32
BF16 = jnp.bfloat16
HIGHEST = lax.Precision.HIGHEST

D_MODEL = 1024
EPS = 1e-6
CONV_WIDTH = 4
MEM_LEN = 256
SSD_HEAD_DIM = 64
SSD_HEADS = 16
SSD_GROUPS = 2
SSD_STATE = 128
SSD_XBC = D_MODEL + 2 * SSD_GROUPS * SSD_STATE
SSD_CHUNK = 128
MLSTM_HEADS = 4
MLSTM_HEAD_DIM = 256
MLSTM_CHUNK = 128
HGRN_HEADS = 8
HGRN_HEAD_DIM = 128
HGRN_CHUNK = 64
XATTN_HEADS = 4
XATTN_HEAD_DIM = 256
PEER_HEADS = 8
PEER_N_KEYS = 128
PEER_TOPK = 16
PEER_HALF = 128
PEER_SEL = PEER_HEADS * PEER_TOPK

SUBLANES = 8
LANES = 128
SMALL_W = LANES
SMALL_DT = 0
SMALL_IG = SSD_HEADS
SMALL_FG = SSD_HEADS + MLSTM_HEADS

VMEM_LIMIT = 52 * 1024 * 1024


def _cparams(sem):
    return pltpu.CompilerParams(dimension_semantics=sem, vmem_limit_bytes=VMEM_LIMIT)


def _sigmoid(x):
    return 1.0 / (1.0 + jnp.exp(-x))


def _softplus(x):
    return jnp.maximum(x, 0.0) + jnp.log(1.0 + jnp.exp(-jnp.abs(x)))


def _rms(x, w):
    return x * lax.rsqrt(jnp.mean(x * x, axis=-1, keepdims=True) + EPS) * w


def _dot(a, b):
    return jnp.dot(a.astype(BF16), b.astype(BF16), preferred_element_type=F32)


def _dot_nt(a, b):
    return lax.dot_general(a.astype(BF16), b.astype(BF16), (((1,), (1,)), ((), ())),
                           preferred_element_type=F32)


def _cumsum_rows(x):
    n = x.shape[0]
    r = lax.broadcasted_iota(jnp.int32, (n, n), 0)
    c = lax.broadcasted_iota(jnp.int32, (n, n), 1)
    tril = (r >= c).astype(F32)
    return jnp.dot(tril, x, precision=HIGHEST, preferred_element_type=F32)


def _causal(n):
    r = lax.broadcasted_iota(jnp.int32, (n, n), 0)
    c = lax.broadcasted_iota(jnp.int32, (n, n), 1)
    return r >= c


def _norm_proj_body(x_ref, nw_ref, w_ref, *out_refs, splits):
    xn = _rms(x_ref[...], nw_ref[...]).astype(BF16)
    off = 0
    for o_ref, n in zip(out_refs, splits):
        o_ref[...] = jnp.dot(xn, w_ref[:, off:off + n], preferred_element_type=F32).astype(o_ref.dtype)
        off += n


def norm_proj(x, nw, w_bf16, splits, tm=512, out_dtype=F32):
    t, d = x.shape
    n = w_bf16.shape[1]
    assert sum(splits) == n
    tm = min(tm, t)
    assert t % tm == 0
    return pl.pallas_call(
        functools.partial(_norm_proj_body, splits=tuple(splits)),
        grid=(t // tm,),
        in_specs=[pl.BlockSpec((tm, d), lambda i: (i, 0)),
                  pl.BlockSpec((1, d), lambda i: (0, 0)),
                  pl.BlockSpec((d, n), lambda i: (0, 0))],
        out_specs=[pl.BlockSpec((tm, s), lambda i: (i, 0)) for s in splits],
        out_shape=[jax.ShapeDtypeStruct((t, s), out_dtype) for s in splits],
        compiler_params=_cparams(("parallel",)),
        name="norm_proj",
    )(x, nw.reshape(1, d), w_bf16)


def _proj_residual_body(*refs, n_parts):
    a_refs = refs[:n_parts]
    w_refs = refs[n_parts:2 * n_parts]
    r_ref = refs[2 * n_parts]
    o_ref = refs[2 * n_parts + 1]
    acc = r_ref[...]
    for a_ref, w_ref in zip(a_refs, w_refs):
        acc = acc + jnp.dot(a_ref[...].astype(BF16), w_ref[...], preferred_element_type=F32)
    o_ref[...] = acc


def proj_residual(parts, ws_bf16, res, tm=512):
    t, d = res.shape
    tm = min(tm, t)
    n_parts = len(parts)
    in_specs = ([pl.BlockSpec((tm, p.shape[1]), lambda i: (i, 0)) for p in parts]
                + [pl.BlockSpec(w.shape, lambda i: (0, 0)) for w in ws_bf16]
                + [pl.BlockSpec((tm, d), lambda i: (i, 0))])
    return pl.pallas_call(
        functools.partial(_proj_residual_body, n_parts=n_parts),
        grid=(t // tm,),
        in_specs=in_specs,
        out_specs=pl.BlockSpec((tm, d), lambda i: (i, 0)),
        out_shape=jax.ShapeDtypeStruct((t, d), F32),
        compiler_params=_cparams(("parallel",)),
        name="proj_residual",
    )(*parts, *ws_bf16, res)


def _conv_silu(x_in, xpad_ref, cw_ref, cb_ref, n_rows):
    xpad_ref[SUBLANES:SUBLANES + n_rows, :] = x_in
    acc = cb_ref[...] + cw_ref[CONV_WIDTH - 1:CONV_WIDTH, :] * x_in
    for k in range(CONV_WIDTH - 1):
        start = SUBLANES - (CONV_WIDTH - 1) + k
        acc = acc + cw_ref[k:k + 1, :] * xpad_ref[start:start + n_rows, :]
    xpad_ref[0:SUBLANES, :] = x_in[n_rows - SUBLANES:n_rows, :]
    return acc * _sigmoid(acc)


def _ssd_body(z_ref, xbc_ref, sm_ref, cw_ref, cb_ref, bias_ref, alog_ref, dsk_ref, nw_ref,
              o_ref, xpad_ref, state_ref, ybuf_ref):
    L, P, N = SSD_CHUNK, SSD_HEAD_DIM, SSD_STATE
    heads_per_group = SSD_HEADS // SSD_GROUPS

    @pl.when(pl.program_id(1) == 0)
    def _init():
        xpad_ref[0:SUBLANES, :] = jnp.zeros((SUBLANES, SSD_XBC), F32)
        state_ref[...] = jnp.zeros_like(state_ref)

    xa = _conv_silu(xbc_ref[...], xpad_ref, cw_ref, cb_ref, L)
    dt = _softplus(sm_ref[...] + bias_ref[...])
    a = -jnp.exp(alog_ref[...])
    cs_col = _cumsum_rows(dt * a)
    cs_row = cs_col.T
    causal = _causal(L)

    for g in range(SSD_GROUPS):
        b_g = xa[:, D_MODEL + g * N:D_MODEL + (g + 1) * N]
        c_g = xa[:, D_MODEL + (SSD_GROUPS + g) * N:D_MODEL + (SSD_GROUPS + g + 1) * N]
        cb = _dot_nt(c_g, b_g)
        b_gt = b_g.T.astype(BF16)
        c_gb = c_g.astype(BF16)
        for e in range(heads_per_group):
            he = g * heads_per_group + e
            col = SMALL_DT + he
            cs_c = cs_col[:, col:col + 1]
            cs_r = cs_row[col:col + 1, :]
            cs_last = cs_col[L - 1:L, col:col + 1]
            decay = jnp.exp(jnp.where(causal, cs_c - cs_r, -jnp.inf))
            x_e = xa[:, he * P:(he + 1) * P]
            x_dt = x_e * dt[:, col:col + 1]
            y = _dot(cb * decay, x_dt)
            s_prev = state_ref[he]
            y = y + jnp.dot(c_gb, s_prev.astype(BF16), preferred_element_type=F32) * jnp.exp(cs_c)
            y = y + x_e * dsk_ref[:, col:col + 1]
            ybuf_ref[:, he * P:(he + 1) * P] = y
            x_end = (x_dt * jnp.exp(cs_last - cs_c)).astype(BF16)
            state_ref[he] = jnp.exp(cs_last) * s_prev + jnp.dot(b_gt, x_end, preferred_element_type=F32)

    zz = z_ref[...]
    y = ybuf_ref[...] * (zz * _sigmoid(zz))
    gw = D_MODEL // SSD_GROUPS
    for g in range(SSD_GROUPS):
        o_ref[:, g * gw:(g + 1) * gw] = _rms(y[:, g * gw:(g + 1) * gw], nw_ref[:, g * gw:(g + 1) * gw])


def ssd_mixer(z, xbc, small, conv_w, conv_b, bias_pad, alog_pad, dskip_pad, norm_w, bsz):
    t = z.shape[0]
    L = SSD_CHUNK
    nc = t // bsz // L
    row = lambda b, c: (b * nc + c, 0)
    const = lambda b, c: (0, 0)
    return pl.pallas_call(
        _ssd_body,
        grid=(bsz, nc),
        in_specs=[pl.BlockSpec((L, D_MODEL), row),
                  pl.BlockSpec((L, SSD_XBC), row),
                  pl.BlockSpec((L, SMALL_W), row),
                  pl.BlockSpec((CONV_WIDTH, SSD_XBC), const),
                  pl.BlockSpec((1, SSD_XBC), const),
                  pl.BlockSpec((1, SMALL_W), const),
                  pl.BlockSpec((1, SMALL_W), const),
                  pl.BlockSpec((1, SMALL_W), const),
                  pl.BlockSpec((1, D_MODEL), const)],
        out_specs=pl.BlockSpec((L, D_MODEL), row),
        out_shape=jax.ShapeDtypeStruct((t, D_MODEL), F32),
        scratch_shapes=[pltpu.VMEM((L + SUBLANES, SSD_XBC), F32),
                        pltpu.VMEM((SSD_HEADS, SSD_STATE, SSD_HEAD_DIM), F32),
                        pltpu.VMEM((L, D_MODEL), F32)],
        compiler_params=_cparams(("parallel", "arbitrary")),
        name="ssd_mixer",
    )(z, xbc, small, conv_w, conv_b.reshape(1, -1), bias_pad, alog_pad, dskip_pad, norm_w.reshape(1, -1))


def _mlstm_body(xm_ref, og_ref, sm_ref, cw_ref, cb_ref, bias_ref, wq_ref, wk_ref, wv_ref, nw_ref,
                o_ref, xpad_ref, c_ref, n_ref, m_ref):
    L, H, Dh = MLSTM_CHUNK, MLSTM_HEADS, MLSTM_HEAD_DIM

    @pl.when(pl.program_id(1) == 0)
    def _init():
        xpad_ref[0:SUBLANES, :] = jnp.zeros((SUBLANES, D_MODEL), F32)
        c_ref[...] = jnp.zeros_like(c_ref)
        n_ref[...] = jnp.zeros_like(n_ref)
        m_ref[...] = jnp.zeros_like(m_ref)

    xm = xm_ref[...]
    xc = _conv_silu(xm, xpad_ref, cw_ref, cb_ref, L)
    smb = sm_ref[...] + bias_ref[...]
    logf = jnp.minimum(smb, 0.0) - jnp.log(1.0 + jnp.exp(-jnp.abs(smb)))
    bcum_col = _cumsum_rows(logf)
    bcum_row = bcum_col.T
    smb_row = smb.T
    causal = _causal(L)

    for h in range(H):
        ci, cf = SMALL_IG + h, SMALL_FG + h
        bc = bcum_col[:, cf:cf + 1]
        br = bcum_row[cf:cf + 1, :]
        ic = smb[:, ci:ci + 1]
        ir = smb_row[ci:ci + 1, :]
        b_end = bcum_col[L - 1:L, cf:cf + 1]
        dlog = jnp.where(causal, bc - br + ir, -jnp.inf)
        m_intra = jnp.max(dlog, axis=1, keepdims=True)
        g_end = b_end - bc + ic
        m_chunk = jnp.max(g_end, axis=0, keepdims=True)
        w_end = jnp.exp(g_end - m_chunk)

        sl = slice(h * Dh, (h + 1) * Dh)
        q = _dot(xc[:, sl], wq_ref[h])
        k = _dot(xc[:, sl], wk_ref[h]) * (Dh ** -0.5)
        v = _dot(xm[:, sl], wv_ref[h])
        qb, kb, vb = q.astype(BF16), k.astype(BF16), v.astype(BF16)

        m_prev = m_ref[h:h + 1, 0:1]
        c_prev = c_ref[h]
        n_prev = n_ref[h:h + 1, :]
        m_t = jnp.maximum(bc + m_prev, m_intra)
        a_t = jnp.exp(bc + m_prev - m_t)
        qk = _dot_nt(qb, kb) * jnp.exp(dlog - m_t)
        num = a_t * jnp.dot(qb, c_prev.astype(BF16), preferred_element_type=F32) \
            + jnp.dot(qk.astype(BF16), vb, preferred_element_type=F32)
        den = a_t * jnp.sum(q * n_prev, axis=1, keepdims=True) + jnp.sum(qk, axis=1, keepdims=True)
        hid = num / jnp.maximum(jnp.abs(den), jnp.exp(-m_t))
        hid = _rms(hid, nw_ref[:, sl])
        og = og_ref[:, sl]
        o_ref[:, sl] = hid * _sigmoid(og)

        m_new = jnp.maximum(b_end + m_prev, m_chunk)
        a_old = jnp.exp(b_end + m_prev - m_new)
        a_new = jnp.exp(m_chunk - m_new)
        kw = k * w_end
        c_local = jnp.dot(kw.T.astype(BF16), vb, preferred_element_type=F32)
        c_ref[h] = a_old * c_prev + a_new * c_local
        n_ref[h:h + 1, :] = a_old * n_prev + a_new * jnp.sum(kw, axis=0, keepdims=True)
        m_ref[h:h + 1, :] = jnp.broadcast_to(m_new, (1, LANES))


def mlstm_mixer(xm, og, small, conv_w, conv_b, bias_pad, wq, wk, wv, norm_w, bsz):
    t = xm.shape[0]
    L = MLSTM_CHUNK
    nc = t // bsz // L
    row = lambda b, c: (b * nc + c, 0)
    const = lambda b, c: (0, 0)
    const3 = lambda b, c: (0, 0, 0)
    wspec = pl.BlockSpec((MLSTM_HEADS, MLSTM_HEAD_DIM, MLSTM_HEAD_DIM), const3)
    return pl.pallas_call(
        _mlstm_body,
        grid=(bsz, nc),
        in_specs=[pl.BlockSpec((L, D_MODEL), row),
                  pl.BlockSpec((L, D_MODEL), row),
                  pl.BlockSpec((L, SMALL_W), row),
                  pl.BlockSpec((CONV_WIDTH, D_MODEL), const),
                  pl.BlockSpec((1, D_MODEL), const),
                  pl.BlockSpec((1, SMALL_W), const),
                  wspec, wspec, wspec,
                  pl.BlockSpec((1, D_MODEL), const)],
        out_specs=pl.BlockSpec((L, D_MODEL), row),
        out_shape=jax.ShapeDtypeStruct((t, D_MODEL), F32),
        scratch_shapes=[pltpu.VMEM((L + SUBLANES, D_MODEL), F32),
                        pltpu.VMEM((MLSTM_HEADS, MLSTM_HEAD_DIM, MLSTM_HEAD_DIM), F32),
                        pltpu.VMEM((SUBLANES, MLSTM_HEAD_DIM), F32),
                        pltpu.VMEM((SUBLANES, LANES), F32)],
        compiler_params=_cparams(("parallel", "arbitrary")),
        name="mlstm_mixer",
    )(xm, og, small, conv_w, conv_b.reshape(1, -1), bias_pad, wq, wk, wv, norm_w.reshape(1, -1))


def _hgrn_body(uq_ref, uf_ref, ui_ref, ug_ref, lbl_ref, nw_ref, o_ref, s_ref):
    L, H, Dk = HGRN_CHUNK, HGRN_HEADS, HGRN_HEAD_DIM

    @pl.when(pl.program_id(1) == 0)
    def _init():
        s_ref[...] = jnp.zeros_like(s_ref)

    lg = lbl_ref[...]
    lmax = jnp.max(lg, axis=0, keepdims=True)
    le = jnp.exp(lg - lmax)
    lb = le[0:1, :] / jnp.sum(le, axis=0, keepdims=True)

    uf = uf_ref[...]
    uq = uq_ref[...]
    logf = jnp.log(lb + (1.0 - lb) * _sigmoid(uf))
    k = (1.0 - lb) * _sigmoid(-uf)
    q = uq * _sigmoid(uq)
    v = ui_ref[...]
    gc = _cumsum_rows(logf)
    g_ref = gc[L // 2:L // 2 + 1, :]
    g_last = gc[L - 1:L, :]
    q_in = (q * jnp.exp(gc - g_ref)).astype(BF16)
    k_in = (k * jnp.exp(g_ref - gc)).astype(BF16)
    q_off = (q * jnp.exp(gc)).astype(BF16)
    k_end = k * jnp.exp(g_last - gc)
    decay = jnp.exp(g_last)
    causal = _causal(L)
    vb = v.astype(BF16)
    ug = ug_ref[...]

    for h in range(H):
        sl = slice(h * Dk, (h + 1) * Dk)
        att = jnp.where(causal, _dot_nt(q_in[:, sl], k_in[:, sl]), 0.0)
        st_prev = s_ref[h]
        o = jnp.dot(att.astype(BF16), vb[:, sl], preferred_element_type=F32) \
            + _dot_nt(q_off[:, sl], st_prev)
        s_ref[h] = decay[:, sl] * st_prev + _dot(v[:, sl].T, k_end[:, sl])
        g = ug[:, sl]
        o_ref[:, sl] = _rms(o, nw_ref[:, sl]) * (g * _sigmoid(g))


def hgrn_mixer(uq, uf, ui, ug, lb_logits, norm_w, bsz):
    t = uq.shape[0]
    L = HGRN_CHUNK
    nc = t // bsz // L
    row = lambda b, c: (b * nc + c, 0)
    const = lambda b, c: (0, 0)
    rs = pl.BlockSpec((L, D_MODEL), row)
    return pl.pallas_call(
        _hgrn_body,
        grid=(bsz, nc),
        in_specs=[rs, rs, rs, rs,
                  pl.BlockSpec((2, D_MODEL), const),
                  pl.BlockSpec((1, D_MODEL), const)],
        out_specs=rs,
        out_shape=jax.ShapeDtypeStruct((t, D_MODEL), F32),
        scratch_shapes=[pltpu.VMEM((HGRN_HEADS, HGRN_HEAD_DIM, HGRN_HEAD_DIM), F32)],
        compiler_params=_cparams(("parallel", "arbitrary")),
        name="hgrn_mixer",
    )(uq, uf, ui, ug, lb_logits, norm_w.reshape(1, -1))


def _xattn_body(h_ref, nw_ref, wq_ref, k_ref, v_ref, wo_ref, o_ref, obuf_ref):
    hv = h_ref[...]
    hn = _rms(hv, nw_ref[...]).astype(BF16)
    q = jnp.dot(hn, wq_ref[...], preferred_element_type=F32)
    scale = XATTN_HEAD_DIM ** -0.5
    for a in range(XATTN_HEADS):
        sl = slice(a * XATTN_HEAD_DIM, (a + 1) * XATTN_HEAD_DIM)
        s = _dot_nt(q[:, sl], k_ref[:, sl]) * scale
        s = s - jnp.max(s, axis=1, keepdims=True)
        p = jnp.exp(s)
        p = p / jnp.sum(p, axis=1, keepdims=True)
        obuf_ref[:, sl] = jnp.dot(p.astype(BF16), v_ref[:, sl], preferred_element_type=F32).astype(BF16)
    o_ref[...] = hv + jnp.dot(obuf_ref[...], wo_ref[...], preferred_element_type=F32)


def xattn_residual(h, nw, wq_bf16, k_bf16, v_bf16, wo_bf16, bsz, tm=512):
    t, d = h.shape
    seq = t // bsz
    tm = min(tm, seq)
    nt = seq // tm
    row = lambda b, i: (b * nt + i, 0)
    const = lambda b, i: (0, 0)
    mem = lambda b, i: (b, 0)
    return pl.pallas_call(
        _xattn_body,
        grid=(bsz, nt),
        in_specs=[pl.BlockSpec((tm, d), row),
                  pl.BlockSpec((1, d), const),
                  pl.BlockSpec((d, d), const),
                  pl.BlockSpec((MEM_LEN, d), mem),
                  pl.BlockSpec((MEM_LEN, d), mem),
                  pl.BlockSpec((d, d), const)],
        out_specs=pl.BlockSpec((tm, d), row),
        out_shape=jax.ShapeDtypeStruct((t, d), F32),
        scratch_shapes=[pltpu.VMEM((tm, d), BF16)],
        compiler_params=_cparams(("parallel", "parallel")),
        name="xattn_residual",
    )(h, nw.reshape(1, d), wq_bf16, k_bf16, v_bf16, wo_bf16)


def _topk_rows(s, k):
    rows = s.shape[0]
    iota = lax.broadcasted_iota(jnp.int32, s.shape, 0)
    vals, idxs = [], []
    for _ in range(k):
        m = jnp.max(s, axis=0, keepdims=True)
        i = jnp.min(jnp.where(s == m, iota, rows), axis=0, keepdims=True)
        vals.append(m)
        idxs.append(i)
        s = jnp.where(iota == i, -jnp.inf, s)
    return jnp.concatenate(vals, axis=0), jnp.concatenate(idxs, axis=0)


def _select_rows(table, sel, n):
    out = jnp.zeros(sel.shape, jnp.int32)
    for a in range(n):
        out = jnp.where(sel == a, table[a:a + 1, :], out)
    return out


def _peer_route_body(h_ref, nw_ref, wq_ref, sk_ref, hn_ref, idx_ref, gate_ref,
                     hnb_ref, st_ref, oi_ref, og_ref, *, n_col_blocks):
    K = PEER_TOPK
    head = pl.program_id(1)

    @pl.when(head == 0)
    def _norm():
        xn = _rms(h_ref[...], nw_ref[...])
        hn_ref[...] = xn
        hnb_ref[...] = xn.astype(BF16)

    q = jnp.dot(hnb_ref[...], wq_ref[...], preferred_element_type=F32)
    for p in range(2):
        qp = q[:, p * PEER_HALF:(p + 1) * PEER_HALF]
        st = _dot_nt(sk_ref[0, p], qp)
        for cb in range(n_col_blocks):
            st_ref[p * n_col_blocks + cb] = st[:, cb * LANES:(cb + 1) * LANES]

    def col_block(cb, carry):
        v1, i1 = _topk_rows(st_ref[cb], K)
        v2, i2 = _topk_rows(st_ref[n_col_blocks + cb], K)
        sub = lax.broadcasted_iota(jnp.int32, (SUBLANES, LANES), 0)
        blocks = [v1[0:1, :] + v2[0:SUBLANES, :], v1[0:1, :] + v2[SUBLANES:K, :]]
        for a in range(1, SUBLANES):
            blocks.append(jnp.where(sub < K // (a + 1), v1[a:a + 1, :] + v2[0:SUBLANES, :], -jnp.inf))
        blocks.append(v1[SUBLANES:K, :] + v2[0:1, :])
        cand = jnp.concatenate(blocks, axis=0)
        top_s, pos = _topk_rows(cand, K)
        blk = lax.shift_right_logical(pos, 3)
        within = lax.bitwise_and(pos, SUBLANES - 1)
        a_sel = jnp.where(blk <= 1, 0, jnp.where(blk <= SUBLANES, blk - 1, within + SUBLANES))
        b_sel = jnp.where(blk == 1, within + SUBLANES, jnp.where(blk <= SUBLANES, within, 0))
        e1 = _select_rows(i1, a_sel, K)
        e2 = _select_rows(i2, b_sel, K)
        oi_ref[cb, pl.ds(row0, K), :] = e1 * PEER_N_KEYS + e2
        ex = jnp.exp(top_s - jnp.max(top_s, axis=0, keepdims=True))
        og_ref[cb, pl.ds(row0, K), :] = ex / jnp.sum(ex, axis=0, keepdims=True)
        return carry

    row0 = pl.multiple_of(head * K, K)
    for cb in range(n_col_blocks):
        col_block(cb, 0)

    @pl.when(head == PEER_HEADS - 1)
    def _emit():
        for cb in range(n_col_blocks):
            idx_ref[cb * LANES:(cb + 1) * LANES, :] = oi_ref[cb].T
            gate_ref[cb * LANES:(cb + 1) * LANES, :] = og_ref[cb].T


def peer_route(h, nw, wq_bf16, subkeys_bf16, tm=512):
    t, d = h.shape
    tm = min(tm, t)
    ncb = tm // LANES
    hq = 2 * PEER_HALF
    return pl.pallas_call(
        functools.partial(_peer_route_body, n_col_blocks=ncb),
        grid=(t // tm, PEER_HEADS),
        in_specs=[pl.BlockSpec((tm, d), lambda i, hd: (i, 0)),
                  pl.BlockSpec((1, d), lambda i, hd: (0, 0)),
                  pl.BlockSpec((d, hq), lambda i, hd: (0, hd)),
                  pl.BlockSpec((1, 2, PEER_N_KEYS, PEER_HALF), lambda i, hd: (hd, 0, 0, 0))],
        out_specs=[pl.BlockSpec((tm, d), lambda i, hd: (i, 0)),
                   pl.BlockSpec((tm, PEER_SEL), lambda i, hd: (i, 0)),
                   pl.BlockSpec((tm, PEER_SEL), lambda i, hd: (i, 0))],
        out_shape=[jax.ShapeDtypeStruct((t, d), F32),
                   jax.ShapeDtypeStruct((t, PEER_SEL), jnp.int32),
                   jax.ShapeDtypeStruct((t, PEER_SEL), F32)],
        scratch_shapes=[pltpu.VMEM((tm, d), BF16),
                        pltpu.VMEM((2 * ncb, PEER_N_KEYS, LANES), F32),
                        pltpu.VMEM((ncb, PEER_SEL, LANES), jnp.int32),
                        pltpu.VMEM((ncb, PEER_SEL, LANES), F32)],
        compiler_params=_cparams(("parallel", "arbitrary")),
        name="peer_route",
    )(h, nw.reshape(1, d), wq_bf16, subkeys_bf16)


PEER_TOK_BLOCK = 16
PEER_SLOTS = 3
PEER_LOOKAHEAD = PEER_SLOTS - 1
PEER_ISSUE_UNROLL = 16
ROW_CHUNKS = D_MODEL // LANES
UV_CHUNKS = ROW_CHUNKS
PIECES = 2 * ROW_CHUNKS
ROW_PITCH = UV_CHUNKS + 1
U_MASK = 0xFFFF0000


def _start_row_copy(uv_hbm, expert, buf, slot, row, sem, priority):
    pltpu.make_async_copy(uv_hbm.at[expert], buf.at[slot, pl.ds(row * ROW_PITCH, UV_CHUNKS), :],
                          sem.at[slot]).start(priority=priority)


def _wait_slot(buf, slot, sem):
    done = buf.at[slot, pl.ds(0, PEER_TOK_BLOCK * PEER_SEL * UV_CHUNKS), :]
    pltpu.make_async_copy(done, done, sem.at[slot]).wait()


def _peer_expert_body(idx_head_ref, idx_ahead_ref, x_ref, gate_ref, h_ref, uv_hbm, o_ref, buf, sem):
    i = pl.program_id(0)
    n_blocks = pl.num_programs(0)
    tb = PEER_TOK_BLOCK
    rows_per_block = tb * PEER_SEL
    slot = lax.rem(i, PEER_SLOTS)
    ahead_slot = lax.rem(i + PEER_LOOKAHEAD, PEER_SLOTS)

    @pl.when(i == 0)
    def _prime():
        for b in range(PEER_LOOKAHEAD):
            def rows(ro, carry):
                for ri in range(PEER_ISSUE_UNROLL):
                    r = ro * PEER_ISSUE_UNROLL + ri
                    _start_row_copy(uv_hbm, idx_head_ref[b * rows_per_block + r], buf, b, r, sem, ri % 2)
                return carry

            lax.fori_loop(0, rows_per_block // PEER_ISSUE_UNROLL, rows, 0)

    _wait_slot(buf, slot, sem)
    gate_t = gate_ref[...].T
    group = PEER_SEL // PIECES

    for tk in range(tb):
        base = tk * PEER_SEL * ROW_PITCH

        def issue(piece):
            for k in range(group):
                r = tk * PEER_SEL + piece * group + k
                _start_row_copy(uv_hbm, idx_ahead_ref[r], buf, ahead_slot, r, sem, k % 2)

        acc = None
        for c in range(ROW_CHUNKS):
            issue(c)
            uv_c = buf[slot, pl.ds(base + c, PEER_SEL, stride=ROW_PITCH), :]
            u_c = pltpu.bitcast(lax.bitwise_and(uv_c, jnp.uint32(U_MASK)), F32)
            term = u_c * x_ref[tk:tk + 1, c * LANES:(c + 1) * LANES]
            acc = term if acc is None else acc + term
        hsel = jnp.sum(acc, axis=1, keepdims=True)
        act = 0.5 * hsel * (1.0 + lax.erf(hsel * (2.0 ** -0.5)))
        w = gate_t[:, tk:tk + 1] * act
        outs = []
        for c in range(ROW_CHUNKS):
            issue(ROW_CHUNKS + c)
            uv_c = buf[slot, pl.ds(base + c, PEER_SEL, stride=ROW_PITCH), :]
            v_c = pltpu.bitcast(lax.shift_left(uv_c, jnp.uint32(16)), F32)
            outs.append(jnp.sum(v_c * w, axis=0, keepdims=True))
        o_ref[tk:tk + 1, :] = h_ref[tk:tk + 1, :] + jnp.concatenate(outs, axis=1)

    @pl.when(i == n_blocks - 1)
    def _drain():
        for k in range(1, PEER_SLOTS):
            _wait_slot(buf, lax.rem(i + k, PEER_SLOTS), sem)


def peer_expert_residual(idx_flat, hn, gate_tok, h, uv_tab):
    t, d = h.shape
    tb = PEER_TOK_BLOCK
    assert t % tb == 0
    n = t // tb
    assert n >= PEER_LOOKAHEAD
    tok_spec = pl.BlockSpec((tb, d), lambda i: (i, 0))
    return pl.pallas_call(
        _peer_expert_body,
        grid=(n,),
        in_specs=[pl.BlockSpec((PEER_LOOKAHEAD * tb * PEER_SEL,), lambda i: (0,), memory_space=pltpu.SMEM),
                  pl.BlockSpec((tb * PEER_SEL,), lambda i: (jnp.minimum(i + PEER_LOOKAHEAD, n - 1),),
                               memory_space=pltpu.SMEM),
                  tok_spec,
                  pl.BlockSpec((tb, PEER_SEL), lambda i: (i, 0)),
                  tok_spec,
                  pl.BlockSpec(memory_space=pl.ANY)],
        out_specs=tok_spec,
        out_shape=jax.ShapeDtypeStruct((t, d), F32),
        scratch_shapes=[pltpu.VMEM((PEER_SLOTS, tb * PEER_SEL * ROW_PITCH, LANES), jnp.uint32),
                        pltpu.SemaphoreType.DMA((PEER_SLOTS,))],
        compiler_params=_cparams(("arbitrary",)),
        name="peer_expert",
    )(idx_flat, idx_flat, hn, gate_tok, h, uv_tab)


def _rmsnorm_body(x_ref, w_ref, o_ref):
    o_ref[...] = _rms(x_ref[...], w_ref[...])


def rmsnorm(x, w, tm=1024):
    t, d = x.shape
    tm = min(tm, t)
    return pl.pallas_call(
        _rmsnorm_body,
        grid=(t // tm,),
        in_specs=[pl.BlockSpec((tm, d), lambda i: (i, 0)), pl.BlockSpec((1, d), lambda i: (0, 0))],
        out_specs=pl.BlockSpec((tm, d), lambda i: (i, 0)),
        out_shape=jax.ShapeDtypeStruct((t, d), F32),
        compiler_params=_cparams(("parallel",)),
        name="rmsnorm",
    )(x, w.reshape(1, d))


def _pad_cols(v, start, width=SMALL_W):
    out = jnp.zeros((1, width), F32)
    return out.at[0, start:start + v.shape[0]].set(v.astype(F32))


def _even_mix(h, bsz, norm_w, w_in, w_out, ssd_conv_w, ssd_conv_b, ssd_dt_bias, ssd_a_log, ssd_d_skip,
              ssd_norm, ml_conv_w, ml_conv_b, ml_wq, ml_wk, ml_wv, ml_i_bias, ml_f_bias, ml_norm):
    d = D_MODEL
    o1 = d
    o2 = o1 + SSD_XBC
    o3 = o2 + SSD_HEADS
    o4 = o3 + d
    o5 = o4 + d
    o6 = o5 + MLSTM_HEADS
    o7 = o6 + MLSTM_HEADS
    w_small = jnp.concatenate([w_in[:, o2:o3], w_in[:, o5:o6], w_in[:, o6:o7],
                               jnp.zeros((d, SMALL_W - (o3 - o2) - (o7 - o5)), w_in.dtype)], axis=1)
    w_all = jnp.concatenate([w_in[:, :o1], w_in[:, o1:o2], w_in[:, o3:o4], w_in[:, o4:o5], w_small],
                            axis=1).astype(BF16)
    z, xbc, xm, og, small = norm_proj(h, norm_w, w_all, (d, SSD_XBC, d, d, SMALL_W), tm=256)
    bias_pad = (_pad_cols(ssd_dt_bias, SMALL_DT) + _pad_cols(ml_i_bias, SMALL_IG)
                + _pad_cols(ml_f_bias, SMALL_FG))
    y_a = ssd_mixer(z, xbc, small, ssd_conv_w, ssd_conv_b, bias_pad, _pad_cols(ssd_a_log, SMALL_DT),
                    _pad_cols(ssd_d_skip, SMALL_DT), ssd_norm, bsz)
    y_b = mlstm_mixer(xm, og, small, ml_conv_w, ml_conv_b, bias_pad, ml_wq.astype(BF16),
                      ml_wk.astype(BF16), ml_wv.astype(BF16), ml_norm, bsz)
    w_out_b = w_out.astype(BF16)
    return proj_residual([y_a, y_b], [w_out_b[:d], w_out_b[d:]], h)


def _odd_mix(h, bsz, norm_w, w_in, w_out, lb_logits, hg_norm):
    d = D_MODEL
    uq, uf, ui, ug = norm_proj(h, norm_w, w_in.astype(BF16), (d, d, d, d), tm=256)
    y = hgrn_mixer(uq, uf, ui, ug, lb_logits, hg_norm, bsz)
    return proj_residual([y], [w_out.astype(BF16)], h)


def _xattn(h, mem2d, bsz, norm_x, norm_m, wq, wkv, wo):
    k, v = norm_proj(mem2d, norm_m, wkv.astype(BF16), (D_MODEL, D_MODEL), out_dtype=BF16)
    return xattn_residual(h, norm_x, wq.astype(BF16), k, v, wo.astype(BF16), bsz)


def _peer(h, norm_w, wq, subkeys, u_tab, v_tab):
    t = h.shape[0]
    n_exp = u_tab.shape[0]
    hn, idx, gate = peer_route(h, norm_w, wq.astype(BF16), subkeys.astype(BF16))
    idx_flat = idx.reshape(t * PEER_SEL)
    u_bits = lax.bitcast_convert_type(u_tab.astype(BF16), jnp.uint16).astype(jnp.uint32)
    v_bits = lax.bitcast_convert_type(v_tab.astype(BF16), jnp.uint16).astype(jnp.uint32)
    uv_tab = ((u_bits << 16) | v_bits).reshape(n_exp, ROW_CHUNKS, LANES)
    return peer_expert_residual(idx_flat, hn, gate, h, uv_tab)


def kernel(x, mem, norm_mix, norm_xattn, norm_mem, norm_ffn, norm_final, ev_w_in, ev_w_out, ssd_conv_w, ssd_conv_b, ssd_dt_bias, ssd_a_log, ssd_d_skip, ssd_norm, ml_conv_w, ml_conv_b, ml_wq, ml_wk, ml_wv, ml_i_bias, ml_f_bias, ml_norm, od_w_in, od_w_out, hgrn_lb_logits, hgrn_norm, xa_wq, xa_wkv, xa_wo, peer_wq, peer_subkeys, peer_u, peer_v):
    bsz, seq, d = x.shape
    depth = norm_mix.shape[0]
    assert depth == 2 and hgrn_lb_logits.shape[0] == 2
    h = x.reshape(bsz * seq, d)
    mem2d = mem.reshape(bsz * MEM_LEN, d)
    for layer in range(depth):
        if layer % 2 == 0:
            e = layer // 2
            h = _even_mix(h, bsz, norm_mix[layer], ev_w_in[e], ev_w_out[e], ssd_conv_w[e], ssd_conv_b[e],
                          ssd_dt_bias[e], ssd_a_log[e], ssd_d_skip[e], ssd_norm[e], ml_conv_w[e],
                          ml_conv_b[e], ml_wq[e], ml_wk[e], ml_wv[e], ml_i_bias[e], ml_f_bias[e], ml_norm[e])
        else:
            o = layer // 2
            h = _odd_mix(h, bsz, norm_mix[layer], od_w_in[o], od_w_out[o], hgrn_lb_logits, hgrn_norm[o])
        h = _xattn(h, mem2d, bsz, norm_xattn[layer], norm_mem[layer], xa_wq[layer], xa_wkv[layer], xa_wo[layer])
        h = _peer(h, norm_ffn[layer], peer_wq[layer], peer_subkeys[layer], peer_u[layer], peer_v[layer])
    return rmsnorm(h, norm_final).reshape(bsz, seq, d)
```

```python
import functools

import jax
import jax.numpy as jnp
from jax import lax
from jax.experimental import pallas as pl
from jax.experimental.pallas import tpu as pltpu

F32 = jnp.float32
BF16 = jnp.bfloat16
HIGHEST = lax.Precision.HIGHEST

D_MODEL = 1024
EPS = 1e-6
CONV_WIDTH = 4
MEM_LEN = 256
SSD_HEAD_DIM = 64
SSD_HEADS = 16
SSD_GROUPS = 2
SSD_STATE = 128
SSD_XBC = D_MODEL + 2 * SSD_GROUPS * SSD_STATE
SSD_CHUNK = 128
MLSTM_HEADS = 4
MLSTM_HEAD_DIM = 256
MLSTM_CHUNK = 128
HGRN_HEADS = 8
HGRN_HEAD_DIM = 128
HGRN_CHUNK = 64
XATTN_HEADS = 4
XATTN_HEAD_DIM = 256
PEER_HEADS = 8
PEER_N_KEYS = 128
PEER_TOPK = 16
PEER_HALF = 128
PEER_SEL = PEER_HEADS * PEER_TOPK

SUBLANES = 8
LANES = 128
SMALL_W = LANES
SMALL_DT = 0
SMALL_IG = SSD_HEADS
SMALL_FG = SSD_HEADS + MLSTM_HEADS

VMEM_LIMIT = 52 * 1024 * 1024


def _cparams(sem):
    return pltpu.CompilerParams(dimension_semantics=sem, vmem_limit_bytes=VMEM_LIMIT)


def _sigmoid(x):
    return 1.0 / (1.0 + jnp.exp(-x))


def _softplus(x):
    return jnp.maximum(x, 0.0) + jnp.log(1.0 + jnp.exp(-jnp.abs(x)))


def _rms(x, w):
    return x * lax.rsqrt(jnp.mean(x * x, axis=-1, keepdims=True) + EPS) * w


def _dot(a, b):
    return jnp.dot(a.astype(BF16), b.astype(BF16), preferred_element_type=F32)


def _dot_nt(a, b):
    return lax.dot_general(a.astype(BF16), b.astype(BF16), (((1,), (1,)), ((), ())),
                           preferred_element_type=F32)


def _cumsum_rows(x):
    n = x.shape[0]
    r = lax.broadcasted_iota(jnp.int32, (n, n), 0)
    c = lax.broadcasted_iota(jnp.int32, (n, n), 1)
    tril = (r >= c).astype(F32)
    return jnp.dot(tril, x, precision=HIGHEST, preferred_element_type=F32)


def _causal(n):
    r = lax.broadcasted_iota(jnp.int32, (n, n), 0)
    c = lax.broadcasted_iota(jnp.int32, (n, n), 1)
    return r >= c


def _norm_proj_body(x_ref, nw_ref, w_ref, *out_refs, splits):
    xn = _rms(x_ref[...], nw_ref[...]).astype(BF16)
    off = 0
    for o_ref, n in zip(out_refs, splits):
        o_ref[...] = jnp.dot(xn, w_ref[:, off:off + n], preferred_element_type=F32).astype(o_ref.dtype)
        off += n


def norm_proj(x, nw, w_bf16, splits, tm=512, out_dtype=F32):
    t, d = x.shape
    n = w_bf16.shape[1]
    assert sum(splits) == n
    tm = min(tm, t)
    assert t % tm == 0
    return pl.pallas_call(
        functools.partial(_norm_proj_body, splits=tuple(splits)),
        grid=(t // tm,),
        in_specs=[pl.BlockSpec((tm, d), lambda i: (i, 0)),
                  pl.BlockSpec((1, d), lambda i: (0, 0)),
                  pl.BlockSpec((d, n), lambda i: (0, 0))],
        out_specs=[pl.BlockSpec((tm, s), lambda i: (i, 0)) for s in splits],
        out_shape=[jax.ShapeDtypeStruct((t, s), out_dtype) for s in splits],
        compiler_params=_cparams(("parallel",)),
        name="norm_proj",
    )(x, nw.reshape(1, d), w_bf16)


def _proj_residual_body(*refs, n_parts):
    a_refs = refs[:n_parts]
    w_refs = refs[n_parts:2 * n_parts]
    r_ref = refs[2 * n_parts]
    o_ref = refs[2 * n_parts + 1]
    acc = r_ref[...]
    for a_ref, w_ref in zip(a_refs, w_refs):
        acc = acc + jnp.dot(a_ref[...].astype(BF16), w_ref[...], preferred_element_type=F32)
    o_ref[...] = acc


def proj_residual(parts, ws_bf16, res, tm=512):
    t, d = res.shape
    tm = min(tm, t)
    n_parts = len(parts)
    in_specs = ([pl.BlockSpec((tm, p.shape[1]), lambda i: (i, 0)) for p in parts]
                + [pl.BlockSpec(w.shape, lambda i: (0, 0)) for w in ws_bf16]
                + [pl.BlockSpec((tm, d), lambda i: (i, 0))])
    return pl.pallas_call(
        functools.partial(_proj_residual_body, n_parts=n_parts),
        grid=(t // tm,),
        in_specs=in_specs,
        out_specs=pl.BlockSpec((tm, d), lambda i: (i, 0)),
        out_shape=jax.ShapeDtypeStruct((t, d), F32),
        compiler_params=_cparams(("parallel",)),
        name="proj_residual",
    )(*parts, *ws_bf16, res)


def _conv_silu(x_in, xpad_ref, cw_ref, cb_ref, n_rows):
    xpad_ref[SUBLANES:SUBLANES + n_rows, :] = x_in
    acc = cb_ref[...] + cw_ref[CONV_WIDTH - 1:CONV_WIDTH, :] * x_in
    for k in range(CONV_WIDTH - 1):
        start = SUBLANES - (CONV_WIDTH - 1) + k
        acc = acc + cw_ref[k:k + 1, :] * xpad_ref[start:start + n_rows, :]
    xpad_ref[0:SUBLANES, :] = x_in[n_rows - SUBLANES:n_rows, :]
    return acc * _sigmoid(acc)


def _ssd_body(z_ref, xbc_ref, sm_ref, cw_ref, cb_ref, bias_ref, alog_ref, dsk_ref, nw_ref,
              o_ref, xpad_ref, state_ref, ybuf_ref):
    L, P, N = SSD_CHUNK, SSD_HEAD_DIM, SSD_STATE
    heads_per_group = SSD_HEADS // SSD_GROUPS

    @pl.when(pl.program_id(1) == 0)
    def _init():
        xpad_ref[0:SUBLANES, :] = jnp.zeros((SUBLANES, SSD_XBC), F32)
        state_ref[...] = jnp.zeros_like(state_ref)

    xa = _conv_silu(xbc_ref[...], xpad_ref, cw_ref, cb_ref, L)
    dt = _softplus(sm_ref[...] + bias_ref[...])
    a = -jnp.exp(alog_ref[...])
    cs_col = _cumsum_rows(dt * a)
    cs_row = cs_col.T
    causal = _causal(L)

    for g in range(SSD_GROUPS):
        b_g = xa[:, D_MODEL + g * N:D_MODEL + (g + 1) * N]
        c_g = xa[:, D_MODEL + (SSD_GROUPS + g) * N:D_MODEL + (SSD_GROUPS + g + 1) * N]
        cb = _dot_nt(c_g, b_g)
        b_gt = b_g.T.astype(BF16)
        c_gb = c_g.astype(BF16)
        for e in range(heads_per_group):
            he = g * heads_per_group + e
            col = SMALL_DT + he
            cs_c = cs_col[:, col:col + 1]
            cs_r = cs_row[col:col + 1, :]
            cs_last = cs_col[L - 1:L, col:col + 1]
            decay = jnp.exp(jnp.where(causal, cs_c - cs_r, -jnp.inf))
            x_e = xa[:, he * P:(he + 1) * P]
            x_dt = x_e * dt[:, col:col + 1]
            y = _dot(cb * decay, x_dt)
            s_prev = state_ref[he]
            y = y + jnp.dot(c_gb, s_prev.astype(BF16), preferred_element_type=F32) * jnp.exp(cs_c)
            y = y + x_e * dsk_ref[:, col:col + 1]
            ybuf_ref[:, he * P:(he + 1) * P] = y
            x_end = (x_dt * jnp.exp(cs_last - cs_c)).astype(BF16)
            state_ref[he] = jnp.exp(cs_last) * s_prev + jnp.dot(b_gt, x_end, preferred_element_type=F32)

    zz = z_ref[...]
    y = ybuf_ref[...] * (zz * _sigmoid(zz))
    gw = D_MODEL // SSD_GROUPS
    for g in range(SSD_GROUPS):
        o_ref[:, g * gw:(g + 1) * gw] = _rms(y[:, g * gw:(g + 1) * gw], nw_ref[:, g * gw:(g + 1) * gw])


def ssd_mixer(z, xbc, small, conv_w, conv_b, bias_pad, alog_pad, dskip_pad, norm_w, bsz):
    t = z.shape[0]
    L = SSD_CHUNK
    nc = t // bsz // L
    row = lambda b, c: (b * nc + c, 0)
    const = lambda b, c: (0, 0)
    return pl.pallas_call(
        _ssd_body,
        grid=(bsz, nc),
        in_specs=[pl.BlockSpec((L, D_MODEL), row),
                  pl.BlockSpec((L, SSD_XBC), row),
                  pl.BlockSpec((L, SMALL_W), row),
                  pl.BlockSpec((CONV_WIDTH, SSD_XBC), const),
                  pl.BlockSpec((1, SSD_XBC), const),
                  pl.BlockSpec((1, SMALL_W), const),
                  pl.BlockSpec((1, SMALL_W), const),
                  pl.BlockSpec((1, SMALL_W), const),
                  pl.BlockSpec((1, D_MODEL), const)],
        out_specs=pl.BlockSpec((L, D_MODEL), row),
        out_shape=jax.ShapeDtypeStruct((t, D_MODEL), F32),
        scratch_shapes=[pltpu.VMEM((L + SUBLANES, SSD_XBC), F32),
                        pltpu.VMEM((SSD_HEADS, SSD_STATE, SSD_HEAD_DIM), F32),
                        pltpu.VMEM((L, D_MODEL), F32)],
        compiler_params=_cparams(("parallel", "arbitrary")),
        name="ssd_mixer",
    )(z, xbc, small, conv_w, conv_b.reshape(1, -1), bias_pad, alog_pad, dskip_pad, norm_w.reshape(1, -1))


def _mlstm_body(xm_ref, og_ref, sm_ref, cw_ref, cb_ref, bias_ref, wq_ref, wk_ref, wv_ref, nw_ref,
                o_ref, xpad_ref, c_ref, n_ref, m_ref):
    L, H, Dh = MLSTM_CHUNK, MLSTM_HEADS, MLSTM_HEAD_DIM

    @pl.when(pl.program_id(1) == 0)
    def _init():
        xpad_ref[0:SUBLANES, :] = jnp.zeros((SUBLANES, D_MODEL), F32)
        c_ref[...] = jnp.zeros_like(c_ref)
        n_ref[...] = jnp.zeros_like(n_ref)
        m_ref[...] = jnp.zeros_like(m_ref)

    xm = xm_ref[...]
    xc = _conv_silu(xm, xpad_ref, cw_ref, cb_ref, L)
    smb = sm_ref[...] + bias_ref[...]
    logf = jnp.minimum(smb, 0.0) - jnp.log(1.0 + jnp.exp(-jnp.abs(smb)))
    bcum_col = _cumsum_rows(logf)
    bcum_row = bcum_col.T
    smb_row = smb.T
    causal = _causal(L)

    for h in range(H):
        ci, cf = SMALL_IG + h, SMALL_FG + h
        bc = bcum_col[:, cf:cf + 1]
        br = bcum_row[cf:cf + 1, :]
        ic = smb[:, ci:ci + 1]
        ir = smb_row[ci:ci + 1, :]
        b_end = bcum_col[L - 1:L, cf:cf + 1]
        dlog = jnp.where(causal, bc - br + ir, -jnp.inf)
        m_intra = jnp.max(dlog, axis=1, keepdims=True)
        g_end = b_end - bc + ic
        m_chunk = jnp.max(g_end, axis=0, keepdims=True)
        w_end = jnp.exp(g_end - m_chunk)

        sl = slice(h * Dh, (h + 1) * Dh)
        q = _dot(xc[:, sl], wq_ref[h])
        k = _dot(xc[:, sl], wk_ref[h]) * (Dh ** -0.5)
        v = _dot(xm[:, sl], wv_ref[h])
        qb, kb, vb = q.astype(BF16), k.astype(BF16), v.astype(BF16)

        m_prev = m_ref[h:h + 1, 0:1]
        c_prev = c_ref[h]
        n_prev = n_ref[h:h + 1, :]
        m_t = jnp.maximum(bc + m_prev, m_intra)
        a_t = jnp.exp(bc + m_prev - m_t)
        qk = _dot_nt(qb, kb) * jnp.exp(dlog - m_t)
        num = a_t * jnp.dot(qb, c_prev.astype(BF16), preferred_element_type=F32) \
            + jnp.dot(qk.astype(BF16), vb, preferred_element_type=F32)
        den = a_t * jnp.sum(q * n_prev, axis=1, keepdims=True) + jnp.sum(qk, axis=1, keepdims=True)
        hid = num / jnp.maximum(jnp.abs(den), jnp.exp(-m_t))
        hid = _rms(hid, nw_ref[:, sl])
        og = og_ref[:, sl]
        o_ref[:, sl] = hid * _sigmoid(og)

        m_new = jnp.maximum(b_end + m_prev, m_chunk)
        a_old = jnp.exp(b_end + m_prev - m_new)
        a_new = jnp.exp(m_chunk - m_new)
        kw = k * w_end
        c_local = jnp.dot(kw.T.astype(BF16), vb, preferred_element_type=F32)
        c_ref[h] = a_old * c_prev + a_new * c_local
        n_ref[h:h + 1, :] = a_old * n_prev + a_new * jnp.sum(kw, axis=0, keepdims=True)
        m_ref[h:h + 1, :] = jnp.broadcast_to(m_new, (1, LANES))


def mlstm_mixer(xm, og, small, conv_w, conv_b, bias_pad, wq, wk, wv, norm_w, bsz):
    t = xm.shape[0]
    L = MLSTM_CHUNK
    nc = t // bsz // L
    row = lambda b, c: (b * nc + c, 0)
    const = lambda b, c: (0, 0)
    const3 = lambda b, c: (0, 0, 0)
    wspec = pl.BlockSpec((MLSTM_HEADS, MLSTM_HEAD_DIM, MLSTM_HEAD_DIM), const3)
    return pl.pallas_call(
        _mlstm_body,
        grid=(bsz, nc),
        in_specs=[pl.BlockSpec((L, D_MODEL), row),
                  pl.BlockSpec((L, D_MODEL), row),
                  pl.BlockSpec((L, SMALL_W), row),
                  pl.BlockSpec((CONV_WIDTH, D_MODEL), const),
                  pl.BlockSpec((1, D_MODEL), const),
                  pl.BlockSpec((1, SMALL_W), const),
                  wspec, wspec, wspec,
                  pl.BlockSpec((1, D_MODEL), const)],
        out_specs=pl.BlockSpec((L, D_MODEL), row),
        out_shape=jax.ShapeDtypeStruct((t, D_MODEL), F32),
        scratch_shapes=[pltpu.VMEM((L + SUBLANES, D_MODEL), F32),
                        pltpu.VMEM((MLSTM_HEADS, MLSTM_HEAD_DIM, MLSTM_HEAD_DIM), F32),
                        pltpu.VMEM((SUBLANES, MLSTM_HEAD_DIM), F32),
                        pltpu.VMEM((SUBLANES, LANES), F32)],
        compiler_params=_cparams(("parallel", "arbitrary")),
        name="mlstm_mixer",
    )(xm, og, small, conv_w, conv_b.reshape(1, -1), bias_pad, wq, wk, wv, norm_w.reshape(1, -1))


def _hgrn_body(uq_ref, uf_ref, ui_ref, ug_ref, lbl_ref, nw_ref, o_ref, s_ref):
    L, H, Dk = HGRN_CHUNK, HGRN_HEADS, HGRN_HEAD_DIM

    @pl.when(pl.program_id(1) == 0)
    def _init():
        s_ref[...] = jnp.zeros_like(s_ref)

    lg = lbl_ref[...]
    lmax = jnp.max(lg, axis=0, keepdims=True)
    le = jnp.exp(lg - lmax)
    lb = le[0:1, :] / jnp.sum(le, axis=0, keepdims=True)

    uf = uf_ref[...]
    uq = uq_ref[...]
    logf = jnp.log(lb + (1.0 - lb) * _sigmoid(uf))
    k = (1.0 - lb) * _sigmoid(-uf)
    q = uq * _sigmoid(uq)
    v = ui_ref[...]
    gc = _cumsum_rows(logf)
    g_ref = gc[L // 2:L // 2 + 1, :]
    g_last = gc[L - 1:L, :]
    q_in = (q * jnp.exp(gc - g_ref)).astype(BF16)
    k_in = (k * jnp.exp(g_ref - gc)).astype(BF16)
    q_off = (q * jnp.exp(gc)).astype(BF16)
    k_end = k * jnp.exp(g_last - gc)
    decay = jnp.exp(g_last)
    causal = _causal(L)
    vb = v.astype(BF16)
    ug = ug_ref[...]

    for h in range(H):
        sl = slice(h * Dk, (h + 1) * Dk)
        att = jnp.where(causal, _dot_nt(q_in[:, sl], k_in[:, sl]), 0.0)
        st_prev = s_ref[h]
        o = jnp.dot(att.astype(BF16), vb[:, sl], preferred_element_type=F32) \
            + _dot_nt(q_off[:, sl], st_prev)
        s_ref[h] = decay[:, sl] * st_prev + _dot(v[:, sl].T, k_end[:, sl])
        g = ug[:, sl]
        o_ref[:, sl] = _rms(o, nw_ref[:, sl]) * (g * _sigmoid(g))


def hgrn_mixer(uq, uf, ui, ug, lb_logits, norm_w, bsz):
    t = uq.shape[0]
    L = HGRN_CHUNK
    nc = t // bsz // L
    row = lambda b, c: (b * nc + c, 0)
    const = lambda b, c: (0, 0)
    rs = pl.BlockSpec((L, D_MODEL), row)
    return pl.pallas_call(
        _hgrn_body,
        grid=(bsz, nc),
        in_specs=[rs, rs, rs, rs,
                  pl.BlockSpec((2, D_MODEL), const),
                  pl.BlockSpec((1, D_MODEL), const)],
        out_specs=rs,
        out_shape=jax.ShapeDtypeStruct((t, D_MODEL), F32),
        scratch_shapes=[pltpu.VMEM((HGRN_HEADS, HGRN_HEAD_DIM, HGRN_HEAD_DIM), F32)],
        compiler_params=_cparams(("parallel", "arbitrary")),
        name="hgrn_mixer",
    )(uq, uf, ui, ug, lb_logits, norm_w.reshape(1, -1))


def _xattn_body(h_ref, nw_ref, wq_ref, k_ref, v_ref, wo_ref, o_ref, obuf_ref):
    hv = h_ref[...]
    hn = _rms(hv, nw_ref[...]).astype(BF16)
    q = jnp.dot(hn, wq_ref[...], preferred_element_type=F32)
    scale = XATTN_HEAD_DIM ** -0.5
    for a in range(XATTN_HEADS):
        sl = slice(a * XATTN_HEAD_DIM, (a + 1) * XATTN_HEAD_DIM)
        s = _dot_nt(q[:, sl], k_ref[:, sl]) * scale
        s = s - jnp.max(s, axis=1, keepdims=True)
        p = jnp.exp(s)
        p = p / jnp.sum(p, axis=1, keepdims=True)
        obuf_ref[:, sl] = jnp.dot(p.astype(BF16), v_ref[:, sl], preferred_element_type=F32).astype(BF16)
    o_ref[...] = hv + jnp.dot(obuf_ref[...], wo_ref[...], preferred_element_type=F32)


def xattn_residual(h, nw, wq_bf16, k_bf16, v_bf16, wo_bf16, bsz, tm=512):
    t, d = h.shape
    seq = t // bsz
    tm = min(tm, seq)
    nt = seq // tm
    row = lambda b, i: (b * nt + i, 0)
    const = lambda b, i: (0, 0)
    mem = lambda b, i: (b, 0)
    return pl.pallas_call(
        _xattn_body,
        grid=(bsz, nt),
        in_specs=[pl.BlockSpec((tm, d), row),
                  pl.BlockSpec((1, d), const),
                  pl.BlockSpec((d, d), const),
                  pl.BlockSpec((MEM_LEN, d), mem),
                  pl.BlockSpec((MEM_LEN, d), mem),
                  pl.BlockSpec((d, d), const)],
        out_specs=pl.BlockSpec((tm, d), row),
        out_shape=jax.ShapeDtypeStruct((t, d), F32),
        scratch_shapes=[pltpu.VMEM((tm, d), BF16)],
        compiler_params=_cparams(("parallel", "parallel")),
        name="xattn_residual",
    )(h, nw.reshape(1, d), wq_bf16, k_bf16, v_bf16, wo_bf16)


def _topk_rows_steps(arrays, k):
    arrays = list(arrays)
    iotas = [lax.broadcasted_iota(jnp.int32, s.shape, 0) for s in arrays]
    vals = [[] for _ in arrays]
    idxs = [[] for _ in arrays]
    for _ in range(k):
        for n, s in enumerate(arrays):
            m = jnp.max(s, axis=0, keepdims=True)
            i = jnp.min(jnp.where(s == m, iotas[n], s.shape[0]), axis=0, keepdims=True)
            vals[n].append(m)
            idxs[n].append(i)
            arrays[n] = jnp.where(iotas[n] == i, -jnp.inf, s)
        yield
    return [(jnp.concatenate(v, axis=0), jnp.concatenate(i, axis=0)) for v, i in zip(vals, idxs)]


def _run(steps):
    try:
        while True:
            next(steps)
    except StopIteration as done:
        return done.value


def _select_rows(table, sel, n):
    out = jnp.zeros(sel.shape, jnp.int32)
    for a in range(n):
        out = jnp.where(sel == a, table[a:a + 1, :], out)
    return out


def _route_norm(head, h_ref, nw_ref, hn_ref, hnb_ref):
    @pl.when(head == 0)
    def _norm():
        xn = _rms(h_ref[...], nw_ref[...])
        hn_ref[...] = xn
        hnb_ref[...] = xn.astype(BF16)


def _route_steps(head, wq_ref, sk_ref, hnb_ref, st_ref, oi_ref, og_ref, n_col_blocks):
    K = PEER_TOPK
    q = jnp.dot(hnb_ref[...], wq_ref[...], preferred_element_type=F32)
    for p in range(2):
        qp = q[:, p * PEER_HALF:(p + 1) * PEER_HALF]
        st = _dot_nt(sk_ref[0, p], qp)
        for cb in range(n_col_blocks):
            st_ref[p * n_col_blocks + cb] = st[:, cb * LANES:(cb + 1) * LANES]
    yield
    tops = yield from _topk_rows_steps([st_ref[j] for j in range(2 * n_col_blocks)], K)

    sub = lax.broadcasted_iota(jnp.int32, (SUBLANES, LANES), 0)
    cands = []
    for cb in range(n_col_blocks):
        v1, v2 = tops[cb][0], tops[n_col_blocks + cb][0]
        blocks = [v1[0:1, :] + v2[0:SUBLANES, :], v1[0:1, :] + v2[SUBLANES:K, :]]
        for a in range(1, SUBLANES):
            blocks.append(jnp.where(sub < K // (a + 1), v1[a:a + 1, :] + v2[0:SUBLANES, :], -jnp.inf))
        blocks.append(v1[SUBLANES:K, :] + v2[0:1, :])
        cands.append(jnp.concatenate(blocks, axis=0))
    yield
    cand_tops = yield from _topk_rows_steps(cands, K)

    row0 = pl.multiple_of(head * K, K)
    for cb in range(n_col_blocks):
        i1, i2 = tops[cb][1], tops[n_col_blocks + cb][1]
        top_s, pos = cand_tops[cb]
        blk = lax.shift_right_logical(pos, 3)
        within = lax.bitwise_and(pos, SUBLANES - 1)
        a_sel = jnp.where(blk <= 1, 0, jnp.where(blk <= SUBLANES, blk - 1, within + SUBLANES))
        b_sel = jnp.where(blk == 1, within + SUBLANES, jnp.where(blk <= SUBLANES, within, 0))
        e1 = _select_rows(i1, a_sel, K)
        e2 = _select_rows(i2, b_sel, K)
        oi_ref[cb, pl.ds(row0, K), :] = e1 * PEER_N_KEYS + e2
        ex = jnp.exp(top_s - jnp.max(top_s, axis=0, keepdims=True))
        og_ref[cb, pl.ds(row0, K), :] = ex / jnp.sum(ex, axis=0, keepdims=True)


def _route_emit(head, idx_ref, gate_ref, oi_ref, og_ref, n_col_blocks):
    @pl.when(head == PEER_HEADS - 1)
    def _emit():
        for cb in range(n_col_blocks):
            idx_ref[cb * LANES:(cb + 1) * LANES, :] = oi_ref[cb].T
            gate_ref[cb * LANES:(cb + 1) * LANES, :] = og_ref[cb].T


def _peer_route_body(h_ref, nw_ref, wq_ref, sk_ref, hn_ref, idx_ref, gate_ref,
                     hnb_ref, st_ref, oi_ref, og_ref, *, n_col_blocks):
    head = pl.program_id(1)
    _route_norm(head, h_ref, nw_ref, hn_ref, hnb_ref)
    _run(_route_steps(head, wq_ref, sk_ref, hnb_ref, st_ref, oi_ref, og_ref, n_col_blocks))
    _route_emit(head, idx_ref, gate_ref, oi_ref, og_ref, n_col_blocks)


def peer_route(h, nw, wq_bf16, subkeys_bf16, tm=512):
    t, d = h.shape
    tm = min(tm, t)
    ncb = tm // LANES
    hq = 2 * PEER_HALF
    return pl.pallas_call(
        functools.partial(_peer_route_body, n_col_blocks=ncb),
        grid=(t // tm, PEER_HEADS),
        in_specs=[pl.BlockSpec((tm, d), lambda i, hd: (i, 0)),
                  pl.BlockSpec((1, d), lambda i, hd: (0, 0)),
                  pl.BlockSpec((d, hq), lambda i, hd: (0, hd)),
                  pl.BlockSpec((1, 2, PEER_N_KEYS, PEER_HALF), lambda i, hd: (hd, 0, 0, 0))],
        out_specs=[pl.BlockSpec((tm, d), lambda i, hd: (i, 0)),
                   pl.BlockSpec((tm, PEER_SEL), lambda i, hd: (i, 0)),
                   pl.BlockSpec((tm, PEER_SEL), lambda i, hd: (i, 0))],
        out_shape=[jax.ShapeDtypeStruct((t, d), F32),
                   jax.ShapeDtypeStruct((t, PEER_SEL), jnp.int32),
                   jax.ShapeDtypeStruct((t, PEER_SEL), F32)],
        scratch_shapes=[pltpu.VMEM((tm, d), BF16),
                        pltpu.VMEM((2 * ncb, PEER_N_KEYS, LANES), F32),
                        pltpu.VMEM((ncb, PEER_SEL, LANES), jnp.int32),
                        pltpu.VMEM((ncb, PEER_SEL, LANES), F32)],
        compiler_params=_cparams(("parallel", "arbitrary")),
        name="peer_route",
    )(h, nw.reshape(1, d), wq_bf16, subkeys_bf16)


PEER_TOK_BLOCK = 16
PEER_PARTS = 8
PEER_SLOTS = 3
FUSED_ROUTE_EVERY = 7
PEER_LOOKAHEAD = PEER_SLOTS - 1
PEER_ISSUE_UNROLL = 16
ROW_CHUNKS = D_MODEL // LANES
UV_CHUNKS = ROW_CHUNKS
PIECES = 2 * ROW_CHUNKS
ROW_PITCH = UV_CHUNKS + 1
U_MASK = 0xFFFF0000


def _start_row_copy(uv_hbm, expert, buf, slot, row, sem, priority):
    pltpu.make_async_copy(uv_hbm.at[expert], buf.at[slot, pl.ds(row * ROW_PITCH, UV_CHUNKS), :],
                          sem.at[slot]).start(priority=priority)


def _wait_slot(buf, slot, sem):
    done = buf.at[slot, pl.ds(0, PEER_TOK_BLOCK * PEER_SEL * UV_CHUNKS), :]
    pltpu.make_async_copy(done, done, sem.at[slot]).wait()


def _expert_prime(i, idx_head_ref, uv_hbm, buf, sem):
    rows_per_block = PEER_TOK_BLOCK * PEER_SEL

    @pl.when(i == 0)
    def _prime():
        for b in range(PEER_LOOKAHEAD):
            def rows(ro, carry):
                for ri in range(PEER_ISSUE_UNROLL):
                    r = ro * PEER_ISSUE_UNROLL + ri
                    _start_row_copy(uv_hbm, idx_head_ref[b * rows_per_block + r], buf, b, r, sem, ri % 2)
                return carry

            lax.fori_loop(0, rows_per_block // PEER_ISSUE_UNROLL, rows, 0)


def _expert_drain(i, n_blocks, buf, sem):
    @pl.when(i == n_blocks - 1)
    def _drain():
        for k in range(1, PEER_SLOTS):
            _wait_slot(buf, lax.rem(i + k, PEER_SLOTS), sem)


def _expert_main(i, idx_ahead_ref, x_ref, gate_ref, h_ref, uv_hbm, o_ref, buf, sem,
                 side_steps=None, side_every=1):
    tb = PEER_TOK_BLOCK
    slot = lax.rem(i, PEER_SLOTS)
    ahead_slot = lax.rem(i + PEER_LOOKAHEAD, PEER_SLOTS)
    _wait_slot(buf, slot, sem)
    gate_t = gate_ref[...].T
    group = PEER_SEL // PIECES
    pieces_done = 0

    for tk in range(tb):
        base = tk * PEER_SEL * ROW_PITCH

        def issue(piece):
            nonlocal pieces_done
            for k in range(group):
                r = tk * PEER_SEL + piece * group + k
                _start_row_copy(uv_hbm, idx_ahead_ref[r], buf, ahead_slot, r, sem, k % 2)
            pieces_done += 1
            if side_steps is not None and pieces_done % side_every == 0:
                next(side_steps, None)

        acc = None
        for c in range(ROW_CHUNKS):
            issue(c)
            uv_c = buf[slot, pl.ds(base + c, PEER_SEL, stride=ROW_PITCH), :]
            u_c = pltpu.bitcast(lax.bitwise_and(uv_c, jnp.uint32(U_MASK)), F32)
            term = u_c * x_ref[tk:tk + 1, c * LANES:(c + 1) * LANES]
            acc = term if acc is None else acc + term
        hsel = jnp.sum(acc, axis=1, keepdims=True)
        act = 0.5 * hsel * (1.0 + lax.erf(hsel * (2.0 ** -0.5)))
        w = gate_t[:, tk:tk + 1] * act
        outs = []
        for c in range(ROW_CHUNKS):
            issue(ROW_CHUNKS + c)
            uv_c = buf[slot, pl.ds(base + c, PEER_SEL, stride=ROW_PITCH), :]
            v_c = pltpu.bitcast(lax.shift_left(uv_c, jnp.uint32(16)), F32)
            outs.append(jnp.sum(v_c * w, axis=0, keepdims=True))
        o_ref[tk:tk + 1, :] = h_ref[tk:tk + 1, :] + jnp.concatenate(outs, axis=1)

    if side_steps is not None:
        for _ in side_steps:
            pass


def _peer_expert_body(idx_head_ref, idx_ahead_ref, x_ref, gate_ref, h_ref, uv_hbm, o_ref, buf, sem):
    i = pl.program_id(0)
    _expert_prime(i, idx_head_ref, uv_hbm, buf, sem)
    _expert_main(i, idx_ahead_ref, x_ref, gate_ref, h_ref, uv_hbm, o_ref, buf, sem)
    _expert_drain(i, pl.num_programs(0), buf, sem)


def _peer_fused_body(idx_head_ref, idx_ahead_ref, x_ref, gate_ref, h_ref, uv_hbm,
                     rh_ref, nw_ref, wq_ref, sk_ref,
                     o_ref, hn_ref, ridx_ref, rgate_ref,
                     buf, sem, hnb_ref, st_ref, oi_ref, og_ref):
    i = pl.program_id(0)
    head = lax.rem(i, PEER_HEADS)
    _expert_prime(i, idx_head_ref, uv_hbm, buf, sem)
    _route_norm(head, rh_ref, nw_ref, hn_ref, hnb_ref)
    route = _route_steps(head, wq_ref, sk_ref, hnb_ref, st_ref, oi_ref, og_ref, 1)
    _expert_main(i, idx_ahead_ref, x_ref, gate_ref, h_ref, uv_hbm, o_ref, buf, sem,
                 side_steps=route, side_every=FUSED_ROUTE_EVERY)
    _route_emit(head, ridx_ref, rgate_ref, oi_ref, og_ref, 1)
    _expert_drain(i, pl.num_programs(0), buf, sem)


def peer_expert_residual(idx_flat, hn, gate_tok, h, uv_tab):
    t, d = h.shape
    tb = PEER_TOK_BLOCK
    assert t % tb == 0
    n = t // tb
    assert n >= PEER_LOOKAHEAD
    tok_spec = pl.BlockSpec((tb, d), lambda i: (i, 0))
    return pl.pallas_call(
        _peer_expert_body,
        grid=(n,),
        in_specs=[pl.BlockSpec((PEER_LOOKAHEAD * tb * PEER_SEL,), lambda i: (0,), memory_space=pltpu.SMEM),
                  pl.BlockSpec((tb * PEER_SEL,), lambda i: (jnp.minimum(i + PEER_LOOKAHEAD, n - 1),),
                               memory_space=pltpu.SMEM),
                  tok_spec,
                  pl.BlockSpec((tb, PEER_SEL), lambda i: (i, 0)),
                  tok_spec,
                  pl.BlockSpec(memory_space=pl.ANY)],
        out_specs=tok_spec,
        out_shape=jax.ShapeDtypeStruct((t, d), F32),
        scratch_shapes=[pltpu.VMEM((PEER_SLOTS, tb * PEER_SEL * ROW_PITCH, LANES), jnp.uint32),
                        pltpu.SemaphoreType.DMA((PEER_SLOTS,))],
        compiler_params=_cparams(("arbitrary",)),
        name="peer_expert",
    )(idx_flat, idx_flat, hn, gate_tok, h, uv_tab)


def peer_expert_route_fused(idx_flat, hn, gate_tok, h_full, uv_tab, nw, wq_bf16, subkeys_bf16, part):
    tp, d = hn.shape
    tb = PEER_TOK_BLOCK
    n = tp // tb
    tiles = tp // LANES
    assert tp % LANES == 0 and tiles * PEER_HEADS == n and n >= PEER_LOOKAHEAD
    hq = 2 * PEER_HALF
    e_off = part * n
    r_off = (part + 1) * tiles
    tok_spec = pl.BlockSpec((tb, d), lambda i: (i, 0))
    tile_of = lambda i: i // PEER_HEADS
    head_of = lambda i: lax.rem(i, PEER_HEADS)
    return pl.pallas_call(
        _peer_fused_body,
        grid=(n,),
        in_specs=[pl.BlockSpec((PEER_LOOKAHEAD * tb * PEER_SEL,), lambda i: (0,), memory_space=pltpu.SMEM),
                  pl.BlockSpec((tb * PEER_SEL,), lambda i: (jnp.minimum(i + PEER_LOOKAHEAD, n - 1),),
                               memory_space=pltpu.SMEM),
                  tok_spec,
                  pl.BlockSpec((tb, PEER_SEL), lambda i: (i, 0)),
                  pl.BlockSpec((tb, d), lambda i: (e_off + i, 0)),
                  pl.BlockSpec(memory_space=pl.ANY),
                  pl.BlockSpec((LANES, d), lambda i: (r_off + tile_of(i), 0)),
                  pl.BlockSpec((1, d), lambda i: (0, 0)),
                  pl.BlockSpec((d, hq), lambda i: (0, head_of(i))),
                  pl.BlockSpec((1, 2, PEER_N_KEYS, PEER_HALF), lambda i: (head_of(i), 0, 0, 0))],
        out_specs=[tok_spec,
                   pl.BlockSpec((LANES, d), lambda i: (tile_of(i), 0)),
                   pl.BlockSpec((LANES, PEER_SEL), lambda i: (tile_of(i), 0)),
                   pl.BlockSpec((LANES, PEER_SEL), lambda i: (tile_of(i), 0))],
        out_shape=[jax.ShapeDtypeStruct((tp, d), F32),
                   jax.ShapeDtypeStruct((tp, d), F32),
                   jax.ShapeDtypeStruct((tp, PEER_SEL), jnp.int32),
                   jax.ShapeDtypeStruct((tp, PEER_SEL), F32)],
        scratch_shapes=[pltpu.VMEM((PEER_SLOTS, tb * PEER_SEL * ROW_PITCH, LANES), jnp.uint32),
                        pltpu.SemaphoreType.DMA((PEER_SLOTS,)),
                        pltpu.VMEM((LANES, d), BF16),
                        pltpu.VMEM((2, PEER_N_KEYS, LANES), F32),
                        pltpu.VMEM((1, PEER_SEL, LANES), jnp.int32),
                        pltpu.VMEM((1, PEER_SEL, LANES), F32)],
        compiler_params=_cparams(("arbitrary",)),
        name="peer_expert_route",
    )(idx_flat, idx_flat, hn, gate_tok, h_full, uv_tab, h_full, nw.reshape(1, d), wq_bf16, subkeys_bf16)


def _rmsnorm_body(x_ref, w_ref, o_ref):
    o_ref[...] = _rms(x_ref[...], w_ref[...])


def rmsnorm(x, w, tm=1024):
    t, d = x.shape
    tm = min(tm, t)
    return pl.pallas_call(
        _rmsnorm_body,
        grid=(t // tm,),
        in_specs=[pl.BlockSpec((tm, d), lambda i: (i, 0)), pl.BlockSpec((1, d), lambda i: (0, 0))],
        out_specs=pl.BlockSpec((tm, d), lambda i: (i, 0)),
        out_shape=jax.ShapeDtypeStruct((t, d), F32),
        compiler_params=_cparams(("parallel",)),
        name="rmsnorm",
    )(x, w.reshape(1, d))


def _pad_cols(v, start, width=SMALL_W):
    out = jnp.zeros((1, width), F32)
    return out.at[0, start:start + v.shape[0]].set(v.astype(F32))


def _even_mix(h, bsz, norm_w, w_in, w_out, ssd_conv_w, ssd_conv_b, ssd_dt_bias, ssd_a_log, ssd_d_skip,
              ssd_norm, ml_conv_w, ml_conv_b, ml_wq, ml_wk, ml_wv, ml_i_bias, ml_f_bias, ml_norm):
    d = D_MODEL
    o1 = d
    o2 = o1 + SSD_XBC
    o3 = o2 + SSD_HEADS
    o4 = o3 + d
    o5 = o4 + d
    o6 = o5 + MLSTM_HEADS
    o7 = o6 + MLSTM_HEADS
    w_small = jnp.concatenate([w_in[:, o2:o3], w_in[:, o5:o6], w_in[:, o6:o7],
                               jnp.zeros((d, SMALL_W - (o3 - o2) - (o7 - o5)), w_in.dtype)], axis=1)
    w_all = jnp.concatenate([w_in[:, :o1], w_in[:, o1:o2], w_in[:, o3:o4], w_in[:, o4:o5], w_small],
                            axis=1).astype(BF16)
    z, xbc, xm, og, small = norm_proj(h, norm_w, w_all, (d, SSD_XBC, d, d, SMALL_W), tm=256)
    bias_pad = (_pad_cols(ssd_dt_bias, SMALL_DT) + _pad_cols(ml_i_bias, SMALL_IG)
                + _pad_cols(ml_f_bias, SMALL_FG))
    y_a = ssd_mixer(z, xbc, small, ssd_conv_w, ssd_conv_b, bias_pad, _pad_cols(ssd_a_log, SMALL_DT),
                    _pad_cols(ssd_d_skip, SMALL_DT), ssd_norm, bsz)
    y_b = mlstm_mixer(xm, og, small, ml_conv_w, ml_conv_b, bias_pad, ml_wq.astype(BF16),
                      ml_wk.astype(BF16), ml_wv.astype(BF16), ml_norm, bsz)
    w_out_b = w_out.astype(BF16)
    return proj_residual([y_a, y_b], [w_out_b[:d], w_out_b[d:]], h)


def _odd_mix(h, bsz, norm_w, w_in, w_out, lb_logits, hg_norm):
    d = D_MODEL
    uq, uf, ui, ug = norm_proj(h, norm_w, w_in.astype(BF16), (d, d, d, d), tm=256)
    y = hgrn_mixer(uq, uf, ui, ug, lb_logits, hg_norm, bsz)
    return proj_residual([y], [w_out.astype(BF16)], h)


def _xattn(h, mem2d, bsz, norm_x, norm_m, wq, wkv, wo):
    k, v = norm_proj(mem2d, norm_m, wkv.astype(BF16), (D_MODEL, D_MODEL), out_dtype=BF16)
    return xattn_residual(h, norm_x, wq.astype(BF16), k, v, wo.astype(BF16), bsz)


def _peer(h, norm_w, wq, subkeys, u_tab, v_tab):
    t = h.shape[0]
    n_exp = u_tab.shape[0]
    wq_b = wq.astype(BF16)
    sk_b = subkeys.astype(BF16)
    u_bits = lax.bitcast_convert_type(u_tab.astype(BF16), jnp.uint16).astype(jnp.uint32)
    v_bits = lax.bitcast_convert_type(v_tab.astype(BF16), jnp.uint16).astype(jnp.uint32)
    uv_tab = ((u_bits << 16) | v_bits).reshape(n_exp, ROW_CHUNKS, LANES)
    parts = PEER_PARTS
    while t % (parts * LANES) != 0:
        parts //= 2
    tp = t // parts
    hn, idx, gate = peer_route(h[:tp], norm_w, wq_b, sk_b)
    outs = []
    for k in range(parts - 1):
        o, hn, idx, gate = peer_expert_route_fused(idx.reshape(tp * PEER_SEL), hn, gate, h, uv_tab,
                                                   norm_w, wq_b, sk_b, k)
        outs.append(o)
    outs.append(peer_expert_residual(idx.reshape(tp * PEER_SEL), hn, gate, h[(parts - 1) * tp:], uv_tab))
    return jnp.concatenate(outs, axis=0)


def kernel(x, mem, norm_mix, norm_xattn, norm_mem, norm_ffn, norm_final, ev_w_in, ev_w_out, ssd_conv_w, ssd_conv_b, ssd_dt_bias, ssd_a_log, ssd_d_skip, ssd_norm, ml_conv_w, ml_conv_b, ml_wq, ml_wk, ml_wv, ml_i_bias, ml_f_bias, ml_norm, od_w_in, od_w_out, hgrn_lb_logits, hgrn_norm, xa_wq, xa_wkv, xa_wo, peer_wq, peer_subkeys, peer_u, peer_v):
    bsz, seq, d = x.shape
    depth = norm_mix.shape[0]
    assert depth == 2 and hgrn_lb_logits.shape[0] == 2
    h = x.reshape(bsz * seq, d)
    mem2d = mem.reshape(bsz * MEM_LEN, d)
    for layer in range(depth):
        if layer % 2 == 0:
            e = layer // 2
            h = _even_mix(h, bsz, norm_mix[layer], ev_w_in[e], ev_w_out[e], ssd_conv_w[e], ssd_conv_b[e],
                          ssd_dt_bias[e], ssd_a_log[e], ssd_d_skip[e], ssd_norm[e], ml_conv_w[e],
                          ml_conv_b[e], ml_wq[e], ml_wk[e], ml_wv[e], ml_i_bias[e], ml_f_bias[e], ml_norm[e])
        else:
            o = layer // 2
            h = _odd_mix(h, bsz, norm_mix[layer], od_w_in[o], od_w_out[o], hgrn_lb_logits, hgrn_norm[o])
        h = _xattn(h, mem2d, bsz, norm_xattn[layer], norm_mem[layer], xa_wq[layer], xa_wkv[layer], xa_wo[layer])
        h = _peer(h, norm_ffn[layer], peer_wq[layer], peer_subkeys[layer], peer_u[layer], peer_v[layer])
    return rmsnorm(h, norm_final).reshape(bsz, seq, d)
```

```python
import functools

import jax
import jax.numpy as jnp
from jax import lax
from jax.experimental import pallas as pl
from jax.experimental.pallas import tpu as pltpu

F32 = jnp.float32
BF16 = jnp.bfloat16
HIGHEST = lax.Precision.HIGHEST

D_MODEL = 1024
EPS = 1e-6
CONV_WIDTH = 4
MEM_LEN = 256
SSD_HEAD_DIM = 64
SSD_HEADS = 16
SSD_GROUPS = 2
SSD_STATE = 128
SSD_XBC = D_MODEL + 2 * SSD_GROUPS * SSD_STATE
SSD_CHUNK = 128
MLSTM_HEADS = 4
MLSTM_HEAD_DIM = 256
MLSTM_CHUNK = 128
HGRN_HEADS = 8
HGRN_HEAD_DIM = 128
HGRN_CHUNK = 64
XATTN_HEADS = 4
XATTN_HEAD_DIM = 256
PEER_HEADS = 8
PEER_N_KEYS = 128
PEER_TOPK = 16
PEER_HALF = 128
PEER_SEL = PEER_HEADS * PEER_TOPK

SUBLANES = 8
LANES = 128
SMALL_W = LANES
SMALL_DT = 0
SMALL_IG = SSD_HEADS
SMALL_FG = SSD_HEADS + MLSTM_HEADS

VMEM_LIMIT = 52 * 1024 * 1024


def _cparams(sem):
    return pltpu.CompilerParams(dimension_semantics=sem, vmem_limit_bytes=VMEM_LIMIT)


def _sigmoid(x):
    return 1.0 / (1.0 + jnp.exp(-x))


def _softplus(x):
    return jnp.maximum(x, 0.0) + jnp.log(1.0 + jnp.exp(-jnp.abs(x)))


def _rms(x, w):
    return x * lax.rsqrt(jnp.mean(x * x, axis=-1, keepdims=True) + EPS) * w


def _dot(a, b):
    return jnp.dot(a.astype(BF16), b.astype(BF16), preferred_element_type=F32)


def _dot_nt(a, b):
    return lax.dot_general(a.astype(BF16), b.astype(BF16), (((1,), (1,)), ((), ())),
                           preferred_element_type=F32)


def _cumsum_rows(x):
    n = x.shape[0]
    r = lax.broadcasted_iota(jnp.int32, (n, n), 0)
    c = lax.broadcasted_iota(jnp.int32, (n, n), 1)
    tril = (r >= c).astype(F32)
    return jnp.dot(tril, x, precision=HIGHEST, preferred_element_type=F32)


def _causal(n):
    r = lax.broadcasted_iota(jnp.int32, (n, n), 0)
    c = lax.broadcasted_iota(jnp.int32, (n, n), 1)
    return r >= c


def _norm_proj_body(x_ref, nw_ref, w_ref, *out_refs, splits):
    xn = _rms(x_ref[...], nw_ref[...]).astype(BF16)
    off = 0
    for o_ref, n in zip(out_refs, splits):
        o_ref[...] = jnp.dot(xn, w_ref[:, off:off + n], preferred_element_type=F32).astype(o_ref.dtype)
        off += n


def norm_proj(x, nw, w_bf16, splits, tm=512, out_dtype=F32):
    t, d = x.shape
    n = w_bf16.shape[1]
    assert sum(splits) == n
    tm = min(tm, t)
    assert t % tm == 0
    return pl.pallas_call(
        functools.partial(_norm_proj_body, splits=tuple(splits)),
        grid=(t // tm,),
        in_specs=[pl.BlockSpec((tm, d), lambda i: (i, 0)),
                  pl.BlockSpec((1, d), lambda i: (0, 0)),
                  pl.BlockSpec((d, n), lambda i: (0, 0))],
        out_specs=[pl.BlockSpec((tm, s), lambda i: (i, 0)) for s in splits],
        out_shape=[jax.ShapeDtypeStruct((t, s), out_dtype) for s in splits],
        compiler_params=_cparams(("parallel",)),
        name="norm_proj",
    )(x, nw.reshape(1, d), w_bf16)


def _proj_residual_body(*refs, n_parts):
    a_refs = refs[:n_parts]
    w_refs = refs[n_parts:2 * n_parts]
    r_ref = refs[2 * n_parts]
    o_ref = refs[2 * n_parts + 1]
    acc = r_ref[...]
    for a_ref, w_ref in zip(a_refs, w_refs):
        acc = acc + jnp.dot(a_ref[...].astype(BF16), w_ref[...], preferred_element_type=F32)
    o_ref[...] = acc


def proj_residual(parts, ws_bf16, res, tm=512):
    t, d = res.shape
    tm = min(tm, t)
    n_parts = len(parts)
    in_specs = ([pl.BlockSpec((tm, p.shape[1]), lambda i: (i, 0)) for p in parts]
                + [pl.BlockSpec(w.shape, lambda i: (0, 0)) for w in ws_bf16]
                + [pl.BlockSpec((tm, d), lambda i: (i, 0))])
    return pl.pallas_call(
        functools.partial(_proj_residual_body, n_parts=n_parts),
        grid=(t // tm,),
        in_specs=in_specs,
        out_specs=pl.BlockSpec((tm, d), lambda i: (i, 0)),
        out_shape=jax.ShapeDtypeStruct((t, d), F32),
        compiler_params=_cparams(("parallel",)),
        name="proj_residual",
    )(*parts, *ws_bf16, res)


def _conv_silu(x_in, xpad_ref, cw_ref, cb_ref, n_rows):
    xpad_ref[SUBLANES:SUBLANES + n_rows, :] = x_in
    acc = cb_ref[...] + cw_ref[CONV_WIDTH - 1:CONV_WIDTH, :] * x_in
    for k in range(CONV_WIDTH - 1):
        start = SUBLANES - (CONV_WIDTH - 1) + k
        acc = acc + cw_ref[k:k + 1, :] * xpad_ref[start:start + n_rows, :]
    xpad_ref[0:SUBLANES, :] = x_in[n_rows - SUBLANES:n_rows, :]
    return acc * _sigmoid(acc)


def _ssd_body(z_ref, xbc_ref, sm_ref, cw_ref, cb_ref, bias_ref, alog_ref, dsk_ref, nw_ref,
              o_ref, xpad_ref, state_ref, ybuf_ref):
    L, P, N = SSD_CHUNK, SSD_HEAD_DIM, SSD_STATE
    heads_per_group = SSD_HEADS // SSD_GROUPS

    @pl.when(pl.program_id(1) == 0)
    def _init():
        xpad_ref[0:SUBLANES, :] = jnp.zeros((SUBLANES, SSD_XBC), F32)
        state_ref[...] = jnp.zeros_like(state_ref)

    xa = _conv_silu(xbc_ref[...], xpad_ref, cw_ref, cb_ref, L)
    dt = _softplus(sm_ref[...] + bias_ref[...])
    a = -jnp.exp(alog_ref[...])
    cs_col = _cumsum_rows(dt * a)
    cs_row = cs_col.T
    causal = _causal(L)

    for g in range(SSD_GROUPS):
        b_g = xa[:, D_MODEL + g * N:D_MODEL + (g + 1) * N]
        c_g = xa[:, D_MODEL + (SSD_GROUPS + g) * N:D_MODEL + (SSD_GROUPS + g + 1) * N]
        cb = _dot_nt(c_g, b_g)
        b_gt = b_g.T.astype(BF16)
        c_gb = c_g.astype(BF16)
        for e in range(heads_per_group):
            he = g * heads_per_group + e
            col = SMALL_DT + he
            cs_c = cs_col[:, col:col + 1]
            cs_r = cs_row[col:col + 1, :]
            cs_last = cs_col[L - 1:L, col:col + 1]
            decay = jnp.exp(jnp.where(causal, cs_c - cs_r, -jnp.inf))
            x_e = xa[:, he * P:(he + 1) * P]
            x_dt = x_e * dt[:, col:col + 1]
            y = _dot(cb * decay, x_dt)
            s_prev = state_ref[he]
            y = y + jnp.dot(c_gb, s_prev.astype(BF16), preferred_element_type=F32) * jnp.exp(cs_c)
            y = y + x_e * dsk_ref[:, col:col + 1]
            ybuf_ref[:, he * P:(he + 1) * P] = y
            x_end = (x_dt * jnp.exp(cs_last - cs_c)).astype(BF16)
            state_ref[he] = jnp.exp(cs_last) * s_prev + jnp.dot(b_gt, x_end, preferred_element_type=F32)

    zz = z_ref[...]
    y = ybuf_ref[...] * (zz * _sigmoid(zz))
    gw = D_MODEL // SSD_GROUPS
    for g in range(SSD_GROUPS):
        o_ref[:, g * gw:(g + 1) * gw] = _rms(y[:, g * gw:(g + 1) * gw], nw_ref[:, g * gw:(g + 1) * gw])


def ssd_mixer(z, xbc, small, conv_w, conv_b, bias_pad, alog_pad, dskip_pad, norm_w, bsz):
    t = z.shape[0]
    L = SSD_CHUNK
    nc = t // bsz // L
    row = lambda b, c: (b * nc + c, 0)
    const = lambda b, c: (0, 0)
    return pl.pallas_call(
        _ssd_body,
        grid=(bsz, nc),
        in_specs=[pl.BlockSpec((L, D_MODEL), row),
                  pl.BlockSpec((L, SSD_XBC), row),
                  pl.BlockSpec((L, SMALL_W), row),
                  pl.BlockSpec((CONV_WIDTH, SSD_XBC), const),
                  pl.BlockSpec((1, SSD_XBC), const),
                  pl.BlockSpec((1, SMALL_W), const),
                  pl.BlockSpec((1, SMALL_W), const),
                  pl.BlockSpec((1, SMALL_W), const),
                  pl.BlockSpec((1, D_MODEL), const)],
        out_specs=pl.BlockSpec((L, D_MODEL), row),
        out_shape=jax.ShapeDtypeStruct((t, D_MODEL), F32),
        scratch_shapes=[pltpu.VMEM((L + SUBLANES, SSD_XBC), F32),
                        pltpu.VMEM((SSD_HEADS, SSD_STATE, SSD_HEAD_DIM), F32),
                        pltpu.VMEM((L, D_MODEL), F32)],
        compiler_params=_cparams(("parallel", "arbitrary")),
        name="ssd_mixer",
    )(z, xbc, small, conv_w, conv_b.reshape(1, -1), bias_pad, alog_pad, dskip_pad, norm_w.reshape(1, -1))


def _mlstm_body(xm_ref, og_ref, sm_ref, cw_ref, cb_ref, bias_ref, wq_ref, wk_ref, wv_ref, nw_ref,
                o_ref, xpad_ref, c_ref, n_ref, m_ref):
    L, H, Dh = MLSTM_CHUNK, MLSTM_HEADS, MLSTM_HEAD_DIM

    @pl.when(pl.program_id(1) == 0)
    def _init():
        xpad_ref[0:SUBLANES, :] = jnp.zeros((SUBLANES, D_MODEL), F32)
        c_ref[...] = jnp.zeros_like(c_ref)
        n_ref[...] = jnp.zeros_like(n_ref)
        m_ref[...] = jnp.zeros_like(m_ref)

    xm = xm_ref[...]
    xc = _conv_silu(xm, xpad_ref, cw_ref, cb_ref, L)
    smb = sm_ref[...] + bias_ref[...]
    logf = jnp.minimum(smb, 0.0) - jnp.log(1.0 + jnp.exp(-jnp.abs(smb)))
    bcum_col = _cumsum_rows(logf)
    bcum_row = bcum_col.T
    smb_row = smb.T
    causal = _causal(L)

    for h in range(H):
        ci, cf = SMALL_IG + h, SMALL_FG + h
        bc = bcum_col[:, cf:cf + 1]
        br = bcum_row[cf:cf + 1, :]
        ic = smb[:, ci:ci + 1]
        ir = smb_row[ci:ci + 1, :]
        b_end = bcum_col[L - 1:L, cf:cf + 1]
        dlog = jnp.where(causal, bc - br + ir, -jnp.inf)
        m_intra = jnp.max(dlog, axis=1, keepdims=True)
        g_end = b_end - bc + ic
        m_chunk = jnp.max(g_end, axis=0, keepdims=True)
        w_end = jnp.exp(g_end - m_chunk)

        sl = slice(h * Dh, (h + 1) * Dh)
        q = _dot(xc[:, sl], wq_ref[h])
        k = _dot(xc[:, sl], wk_ref[h]) * (Dh ** -0.5)
        v = _dot(xm[:, sl], wv_ref[h])
        qb, kb, vb = q.astype(BF16), k.astype(BF16), v.astype(BF16)

        m_prev = m_ref[h:h + 1, 0:1]
        c_prev = c_ref[h]
        n_prev = n_ref[h:h + 1, :]
        m_t = jnp.maximum(bc + m_prev, m_intra)
        a_t = jnp.exp(bc + m_prev - m_t)
        qk = _dot_nt(qb, kb) * jnp.exp(dlog - m_t)
        num = a_t * jnp.dot(qb, c_prev.astype(BF16), preferred_element_type=F32) \
            + jnp.dot(qk.astype(BF16), vb, preferred_element_type=F32)
        den = a_t * jnp.sum(q * n_prev, axis=1, keepdims=True) + jnp.sum(qk, axis=1, keepdims=True)
        hid = num / jnp.maximum(jnp.abs(den), jnp.exp(-m_t))
        hid = _rms(hid, nw_ref[:, sl])
        og = og_ref[:, sl]
        o_ref[:, sl] = hid * _sigmoid(og)

        m_new = jnp.maximum(b_end + m_prev, m_chunk)
        a_old = jnp.exp(b_end + m_prev - m_new)
        a_new = jnp.exp(m_chunk - m_new)
        kw = k * w_end
        c_local = jnp.dot(kw.T.astype(BF16), vb, preferred_element_type=F32)
        c_ref[h] = a_old * c_prev + a_new * c_local
        n_ref[h:h + 1, :] = a_old * n_prev + a_new * jnp.sum(kw, axis=0, keepdims=True)
        m_ref[h:h + 1, :] = jnp.broadcast_to(m_new, (1, LANES))


def mlstm_mixer(xm, og, small, conv_w, conv_b, bias_pad, wq, wk, wv, norm_w, bsz):
    t = xm.shape[0]
    L = MLSTM_CHUNK
    nc = t // bsz // L
    row = lambda b, c: (b * nc + c, 0)
    const = lambda b, c: (0, 0)
    const3 = lambda b, c: (0, 0, 0)
    wspec = pl.BlockSpec((MLSTM_HEADS, MLSTM_HEAD_DIM, MLSTM_HEAD_DIM), const3)
    return pl.pallas_call(
        _mlstm_body,
        grid=(bsz, nc),
        in_specs=[pl.BlockSpec((L, D_MODEL), row),
                  pl.BlockSpec((L, D_MODEL), row),
                  pl.BlockSpec((L, SMALL_W), row),
                  pl.BlockSpec((CONV_WIDTH, D_MODEL), const),
                  pl.BlockSpec((1, D_MODEL), const),
                  pl.BlockSpec((1, SMALL_W), const),
                  wspec, wspec, wspec,
                  pl.BlockSpec((1, D_MODEL), const)],
        out_specs=pl.BlockSpec((L, D_MODEL), row),
        out_shape=jax.ShapeDtypeStruct((t, D_MODEL), F32),
        scratch_shapes=[pltpu.VMEM((L + SUBLANES, D_MODEL), F32),
                        pltpu.VMEM((MLSTM_HEADS, MLSTM_HEAD_DIM, MLSTM_HEAD_DIM), F32),
                        pltpu.VMEM((SUBLANES, MLSTM_HEAD_DIM), F32),
                        pltpu.VMEM((SUBLANES, LANES), F32)],
        compiler_params=_cparams(("parallel", "arbitrary")),
        name="mlstm_mixer",
    )(xm, og, small, conv_w, conv_b.reshape(1, -1), bias_pad, wq, wk, wv, norm_w.reshape(1, -1))


def _hgrn_body(uq_ref, uf_ref, ui_ref, ug_ref, lbl_ref, nw_ref, o_ref, s_ref):
    L, H, Dk = HGRN_CHUNK, HGRN_HEADS, HGRN_HEAD_DIM

    @pl.when(pl.program_id(1) == 0)
    def _init():
        s_ref[...] = jnp.zeros_like(s_ref)

    lg = lbl_ref[...]
    lmax = jnp.max(lg, axis=0, keepdims=True)
    le = jnp.exp(lg - lmax)
    lb = le[0:1, :] / jnp.sum(le, axis=0, keepdims=True)

    uf = uf_ref[...]
    uq = uq_ref[...]
    logf = jnp.log(lb + (1.0 - lb) * _sigmoid(uf))
    k = (1.0 - lb) * _sigmoid(-uf)
    q = uq * _sigmoid(uq)
    v = ui_ref[...]
    gc = _cumsum_rows(logf)
    g_ref = gc[L // 2:L // 2 + 1, :]
    g_last = gc[L - 1:L, :]
    q_in = (q * jnp.exp(gc - g_ref)).astype(BF16)
    k_in = (k * jnp.exp(g_ref - gc)).astype(BF16)
    q_off = (q * jnp.exp(gc)).astype(BF16)
    k_end = k * jnp.exp(g_last - gc)
    decay = jnp.exp(g_last)
    causal = _causal(L)
    vb = v.astype(BF16)
    ug = ug_ref[...]

    for h in range(H):
        sl = slice(h * Dk, (h + 1) * Dk)
        att = jnp.where(causal, _dot_nt(q_in[:, sl], k_in[:, sl]), 0.0)
        st_prev = s_ref[h]
        o = jnp.dot(att.astype(BF16), vb[:, sl], preferred_element_type=F32) \
            + _dot_nt(q_off[:, sl], st_prev)
        s_ref[h] = decay[:, sl] * st_prev + _dot(v[:, sl].T, k_end[:, sl])
        g = ug[:, sl]
        o_ref[:, sl] = _rms(o, nw_ref[:, sl]) * (g * _sigmoid(g))


def hgrn_mixer(uq, uf, ui, ug, lb_logits, norm_w, bsz):
    t = uq.shape[0]
    L = HGRN_CHUNK
    nc = t // bsz // L
    row = lambda b, c: (b * nc + c, 0)
    const = lambda b, c: (0, 0)
    rs = pl.BlockSpec((L, D_MODEL), row)
    return pl.pallas_call(
        _hgrn_body,
        grid=(bsz, nc),
        in_specs=[rs, rs, rs, rs,
                  pl.BlockSpec((2, D_MODEL), const),
                  pl.BlockSpec((1, D_MODEL), const)],
        out_specs=rs,
        out_shape=jax.ShapeDtypeStruct((t, D_MODEL), F32),
        scratch_shapes=[pltpu.VMEM((HGRN_HEADS, HGRN_HEAD_DIM, HGRN_HEAD_DIM), F32)],
        compiler_params=_cparams(("parallel", "arbitrary")),
        name="hgrn_mixer",
    )(uq, uf, ui, ug, lb_logits, norm_w.reshape(1, -1))


def _xattn_body(h_ref, nw_ref, wq_ref, k_ref, v_ref, wo_ref, o_ref, obuf_ref):
    hv = h_ref[...]
    hn = _rms(hv, nw_ref[...]).astype(BF16)
    q = jnp.dot(hn, wq_ref[...], preferred_element_type=F32)
    scale = XATTN_HEAD_DIM ** -0.5
    for a in range(XATTN_HEADS):
        sl = slice(a * XATTN_HEAD_DIM, (a + 1) * XATTN_HEAD_DIM)
        s = _dot_nt(q[:, sl], k_ref[:, sl]) * scale
        s = s - jnp.max(s, axis=1, keepdims=True)
        p = jnp.exp(s)
        p = p / jnp.sum(p, axis=1, keepdims=True)
        obuf_ref[:, sl] = jnp.dot(p.astype(BF16), v_ref[:, sl], preferred_element_type=F32).astype(BF16)
    o_ref[...] = hv + jnp.dot(obuf_ref[...], wo_ref[...], preferred_element_type=F32)


def xattn_residual(h, nw, wq_bf16, k_bf16, v_bf16, wo_bf16, bsz, tm=512):
    t, d = h.shape
    seq = t // bsz
    tm = min(tm, seq)
    nt = seq // tm
    row = lambda b, i: (b * nt + i, 0)
    const = lambda b, i: (0, 0)
    mem = lambda b, i: (b, 0)
    return pl.pallas_call(
        _xattn_body,
        grid=(bsz, nt),
        in_specs=[pl.BlockSpec((tm, d), row),
                  pl.BlockSpec((1, d), const),
                  pl.BlockSpec((d, d), const),
                  pl.BlockSpec((MEM_LEN, d), mem),
                  pl.BlockSpec((MEM_LEN, d), mem),
                  pl.BlockSpec((d, d), const)],
        out_specs=pl.BlockSpec((tm, d), row),
        out_shape=jax.ShapeDtypeStruct((t, d), F32),
        scratch_shapes=[pltpu.VMEM((tm, d), BF16)],
        compiler_params=_cparams(("parallel", "parallel")),
        name="xattn_residual",
    )(h, nw.reshape(1, d), wq_bf16, k_bf16, v_bf16, wo_bf16)


def _topk_rows_steps(arrays, k):
    arrays = list(arrays)
    iotas = [lax.broadcasted_iota(jnp.int32, s.shape, 0) for s in arrays]
    vals = [[] for _ in arrays]
    idxs = [[] for _ in arrays]
    for _ in range(k):
        for n, s in enumerate(arrays):
            m = jnp.max(s, axis=0, keepdims=True)
            i = jnp.min(jnp.where(s == m, iotas[n], s.shape[0]), axis=0, keepdims=True)
            vals[n].append(m)
            idxs[n].append(i)
            arrays[n] = jnp.where(iotas[n] == i, -jnp.inf, s)
        yield
    return [(jnp.concatenate(v, axis=0), jnp.concatenate(i, axis=0)) for v, i in zip(vals, idxs)]


def _run(steps):
    try:
        while True:
            next(steps)
    except StopIteration as done:
        return done.value


def _select_rows(table, sel, n):
    out = jnp.zeros(sel.shape, jnp.int32)
    for a in range(n):
        out = jnp.where(sel == a, table[a:a + 1, :], out)
    return out


def _route_norm(head, h_ref, nw_ref, hn_ref, hnb_ref):
    @pl.when(head == 0)
    def _norm():
        xn = _rms(h_ref[...], nw_ref[...])
        hn_ref[...] = xn
        hnb_ref[...] = xn.astype(BF16)


def _route_steps(head, wq_ref, sk_ref, hnb_ref, st_ref, oi_ref, og_ref, n_col_blocks):
    K = PEER_TOPK
    q = jnp.dot(hnb_ref[...], wq_ref[...], preferred_element_type=F32)
    for p in range(2):
        qp = q[:, p * PEER_HALF:(p + 1) * PEER_HALF]
        st = _dot_nt(sk_ref[0, p], qp)
        for cb in range(n_col_blocks):
            st_ref[p * n_col_blocks + cb] = st[:, cb * LANES:(cb + 1) * LANES]
    yield
    tops = yield from _topk_rows_steps([st_ref[j] for j in range(2 * n_col_blocks)], K)

    sub = lax.broadcasted_iota(jnp.int32, (SUBLANES, LANES), 0)
    cands = []
    for cb in range(n_col_blocks):
        v1, v2 = tops[cb][0], tops[n_col_blocks + cb][0]
        blocks = [v1[0:1, :] + v2[0:SUBLANES, :], v1[0:1, :] + v2[SUBLANES:K, :]]
        for a in range(1, SUBLANES):
            blocks.append(jnp.where(sub < K // (a + 1), v1[a:a + 1, :] + v2[0:SUBLANES, :], -jnp.inf))
        blocks.append(v1[SUBLANES:K, :] + v2[0:1, :])
        cands.append(jnp.concatenate(blocks, axis=0))
    yield
    cand_tops = yield from _topk_rows_steps(cands, K)

    row0 = pl.multiple_of(head * K, K)
    for cb in range(n_col_blocks):
        i1, i2 = tops[cb][1], tops[n_col_blocks + cb][1]
        top_s, pos = cand_tops[cb]
        blk = lax.shift_right_logical(pos, 3)
        within = lax.bitwise_and(pos, SUBLANES - 1)
        a_sel = jnp.where(blk <= 1, 0, jnp.where(blk <= SUBLANES, blk - 1, within + SUBLANES))
        b_sel = jnp.where(blk == 1, within + SUBLANES, jnp.where(blk <= SUBLANES, within, 0))
        e1 = _select_rows(i1, a_sel, K)
        e2 = _select_rows(i2, b_sel, K)
        oi_ref[cb, pl.ds(row0, K), :] = e1 * PEER_N_KEYS + e2
        ex = jnp.exp(top_s - jnp.max(top_s, axis=0, keepdims=True))
        og_ref[cb, pl.ds(row0, K), :] = ex / jnp.sum(ex, axis=0, keepdims=True)


def _route_emit(head, idx_ref, gate_ref, oi_ref, og_ref, n_col_blocks):
    @pl.when(head == PEER_HEADS - 1)
    def _emit():
        for cb in range(n_col_blocks):
            idx_ref[cb * LANES:(cb + 1) * LANES, :] = oi_ref[cb].T
            gate_ref[cb * LANES:(cb + 1) * LANES, :] = og_ref[cb].T


def _peer_route_body(h_ref, nw_ref, wq_ref, sk_ref, hn_ref, idx_ref, gate_ref,
                     hnb_ref, st_ref, oi_ref, og_ref, *, n_col_blocks):
    head = pl.program_id(1)
    _route_norm(head, h_ref, nw_ref, hn_ref, hnb_ref)
    _run(_route_steps(head, wq_ref, sk_ref, hnb_ref, st_ref, oi_ref, og_ref, n_col_blocks))
    _route_emit(head, idx_ref, gate_ref, oi_ref, og_ref, n_col_blocks)


def peer_route(h, nw, wq_bf16, subkeys_bf16, tm=512):
    t, d = h.shape
    tm = min(tm, t)
    ncb = tm // LANES
    hq = 2 * PEER_HALF
    return pl.pallas_call(
        functools.partial(_peer_route_body, n_col_blocks=ncb),
        grid=(t // tm, PEER_HEADS),
        in_specs=[pl.BlockSpec((tm, d), lambda i, hd: (i, 0)),
                  pl.BlockSpec((1, d), lambda i, hd: (0, 0)),
                  pl.BlockSpec((d, hq), lambda i, hd: (0, hd)),
                  pl.BlockSpec((1, 2, PEER_N_KEYS, PEER_HALF), lambda i, hd: (hd, 0, 0, 0))],
        out_specs=[pl.BlockSpec((tm, d), lambda i, hd: (i, 0)),
                   pl.BlockSpec((tm, PEER_SEL), lambda i, hd: (i, 0)),
                   pl.BlockSpec((tm, PEER_SEL), lambda i, hd: (i, 0))],
        out_shape=[jax.ShapeDtypeStruct((t, d), F32),
                   jax.ShapeDtypeStruct((t, PEER_SEL), jnp.int32),
                   jax.ShapeDtypeStruct((t, PEER_SEL), F32)],
        scratch_shapes=[pltpu.VMEM((tm, d), BF16),
                        pltpu.VMEM((2 * ncb, PEER_N_KEYS, LANES), F32),
                        pltpu.VMEM((ncb, PEER_SEL, LANES), jnp.int32),
                        pltpu.VMEM((ncb, PEER_SEL, LANES), F32)],
        compiler_params=_cparams(("parallel", "arbitrary")),
        name="peer_route",
    )(h, nw.reshape(1, d), wq_bf16, subkeys_bf16)


PEER_TOK_BLOCK = 16
PEER_PARTS = 8
PEER_SLOTS = 3
FUSED_ROUTE_EVERY = 7
PEER_LOOKAHEAD = PEER_SLOTS - 1
PEER_ISSUE_UNROLL = 16
ROW_CHUNKS = D_MODEL // LANES
UV_CHUNKS = ROW_CHUNKS
PIECES = 2 * ROW_CHUNKS
ROW_PITCH = UV_CHUNKS + 1
U_MASK = 0xFFFF0000


def _start_row_copy(uv_hbm, expert, buf, slot, row, sem, priority):
    pltpu.make_async_copy(uv_hbm.at[expert], buf.at[slot, pl.ds(row * ROW_PITCH, UV_CHUNKS), :],
                          sem.at[slot]).start(priority=priority)


def _wait_slot(buf, slot, sem):
    done = buf.at[slot, pl.ds(0, PEER_TOK_BLOCK * PEER_SEL * UV_CHUNKS), :]
    pltpu.make_async_copy(done, done, sem.at[slot]).wait()


def _expert_prime(i, idx_head_ref, uv_hbm, buf, sem):
    rows_per_block = PEER_TOK_BLOCK * PEER_SEL

    @pl.when(i == 0)
    def _prime():
        for b in range(PEER_LOOKAHEAD):
            def rows(ro, carry):
                for ri in range(PEER_ISSUE_UNROLL):
                    r = ro * PEER_ISSUE_UNROLL + ri
                    _start_row_copy(uv_hbm, idx_head_ref[b * rows_per_block + r], buf, b, r, sem, ri % 2)
                return carry

            lax.fori_loop(0, rows_per_block // PEER_ISSUE_UNROLL, rows, 0)


def _expert_drain(i, n_blocks, buf, sem):
    @pl.when(i == n_blocks - 1)
    def _drain():
        for k in range(1, PEER_SLOTS):
            _wait_slot(buf, lax.rem(i + k, PEER_SLOTS), sem)


def _expert_main(slot, ahead_slot, idx_ahead_ref, x_ref, gate_ref, h_ref, uv_hbm, o_ref, buf, sem,
                 side_steps=None, side_every=1):
    tb = PEER_TOK_BLOCK
    _wait_slot(buf, slot, sem)
    gate_t = gate_ref[...].T
    group = PEER_SEL // PIECES
    pieces_done = 0

    for tk in range(tb):
        base = tk * PEER_SEL * ROW_PITCH

        def issue(piece):
            nonlocal pieces_done
            for k in range(group):
                r = tk * PEER_SEL + piece * group + k
                _start_row_copy(uv_hbm, idx_ahead_ref[r], buf, ahead_slot, r, sem, k % 2)
            pieces_done += 1
            if side_steps is not None and pieces_done % side_every == 0:
                next(side_steps, None)

        acc = None
        for c in range(ROW_CHUNKS):
            issue(c)
            uv_c = buf[slot, pl.ds(base + c, PEER_SEL, stride=ROW_PITCH), :]
            u_c = pltpu.bitcast(lax.bitwise_and(uv_c, jnp.uint32(U_MASK)), F32)
            term = u_c * x_ref[tk:tk + 1, c * LANES:(c + 1) * LANES]
            acc = term if acc is None else acc + term
        hsel = jnp.sum(acc, axis=1, keepdims=True)
        act = 0.5 * hsel * (1.0 + lax.erf(hsel * (2.0 ** -0.5)))
        w = gate_t[:, tk:tk + 1] * act
        outs = []
        for c in range(ROW_CHUNKS):
            issue(ROW_CHUNKS + c)
            uv_c = buf[slot, pl.ds(base + c, PEER_SEL, stride=ROW_PITCH), :]
            v_c = pltpu.bitcast(lax.shift_left(uv_c, jnp.uint32(16)), F32)
            outs.append(jnp.sum(v_c * w, axis=0, keepdims=True))
        o_ref[tk:tk + 1, :] = h_ref[tk:tk + 1, :] + jnp.concatenate(outs, axis=1)

    if side_steps is not None:
        for _ in side_steps:
            pass


def _expert_step(i, make_side_steps, *refs):
    slot_now = lax.rem(i, PEER_SLOTS)
    for s in range(PEER_SLOTS):
        @pl.when(slot_now == s)
        def _specialised(s=s):
            side = make_side_steps() if make_side_steps is not None else None
            _expert_main(s, (s + PEER_LOOKAHEAD) % PEER_SLOTS, *refs,
                         side_steps=side, side_every=FUSED_ROUTE_EVERY)


def _peer_expert_body(idx_head_ref, idx_ahead_ref, x_ref, gate_ref, h_ref, uv_hbm, o_ref, buf, sem):
    i = pl.program_id(0)
    _expert_prime(i, idx_head_ref, uv_hbm, buf, sem)
    _expert_step(i, None, idx_ahead_ref, x_ref, gate_ref, h_ref, uv_hbm, o_ref, buf, sem)
    _expert_drain(i, pl.num_programs(0), buf, sem)


def _peer_fused_body(part_ref, idx_head_ref, idx_ahead_ref, x_ref, gate_ref, h_ref, uv_hbm,
                     rh_ref, nw_ref, wq_ref, sk_ref,
                     o_ref, hn_ref, ridx_ref, rgate_ref,
                     buf, sem, hnb_ref, st_ref, oi_ref, og_ref):
    del part_ref
    i = pl.program_id(0)
    head = lax.rem(i, PEER_HEADS)
    _expert_prime(i, idx_head_ref, uv_hbm, buf, sem)
    _route_norm(head, rh_ref, nw_ref, hn_ref, hnb_ref)
    route = functools.partial(_route_steps, head, wq_ref, sk_ref, hnb_ref, st_ref, oi_ref, og_ref, 1)
    _expert_step(i, route, idx_ahead_ref, x_ref, gate_ref, h_ref, uv_hbm, o_ref, buf, sem)
    _route_emit(head, ridx_ref, rgate_ref, oi_ref, og_ref, 1)
    _expert_drain(i, pl.num_programs(0), buf, sem)


def peer_expert_residual(idx_flat, hn, gate_tok, h, uv_tab):
    t, d = h.shape
    tb = PEER_TOK_BLOCK
    assert t % tb == 0
    n = t // tb
    assert n >= PEER_LOOKAHEAD
    tok_spec = pl.BlockSpec((tb, d), lambda i: (i, 0))
    return pl.pallas_call(
        _peer_expert_body,
        grid=(n,),
        in_specs=[pl.BlockSpec((PEER_LOOKAHEAD * tb * PEER_SEL,), lambda i: (0,), memory_space=pltpu.SMEM),
                  pl.BlockSpec((tb * PEER_SEL,), lambda i: (jnp.minimum(i + PEER_LOOKAHEAD, n - 1),),
                               memory_space=pltpu.SMEM),
                  tok_spec,
                  pl.BlockSpec((tb, PEER_SEL), lambda i: (i, 0)),
                  tok_spec,
                  pl.BlockSpec(memory_space=pl.ANY)],
        out_specs=tok_spec,
        out_shape=jax.ShapeDtypeStruct((t, d), F32),
        scratch_shapes=[pltpu.VMEM((PEER_SLOTS, tb * PEER_SEL * ROW_PITCH, LANES), jnp.uint32),
                        pltpu.SemaphoreType.DMA((PEER_SLOTS,))],
        compiler_params=_cparams(("arbitrary",)),
        name="peer_expert",
    )(idx_flat, idx_flat, hn, gate_tok, h, uv_tab)


def peer_expert_route_fused(idx_flat, hn, gate_tok, h_full, uv_tab, nw, wq_bf16, subkeys_bf16, part):
    tp, d = hn.shape
    tb = PEER_TOK_BLOCK
    n = tp // tb
    tiles = tp // LANES
    assert tp % LANES == 0 and tiles * PEER_HEADS == n and n >= PEER_LOOKAHEAD
    hq = 2 * PEER_HALF
    tok_spec = pl.BlockSpec((tb, d), lambda i, p: (i, 0))
    tile_of = lambda i: i // PEER_HEADS
    head_of = lambda i: lax.rem(i, PEER_HEADS)
    grid_spec = pltpu.PrefetchScalarGridSpec(
        num_scalar_prefetch=1,
        grid=(n,),
        in_specs=[pl.BlockSpec((PEER_LOOKAHEAD * tb * PEER_SEL,), lambda i, p: (0,), memory_space=pltpu.SMEM),
                  pl.BlockSpec((tb * PEER_SEL,), lambda i, p: (jnp.minimum(i + PEER_LOOKAHEAD, n - 1),),
                               memory_space=pltpu.SMEM),
                  tok_spec,
                  pl.BlockSpec((tb, PEER_SEL), lambda i, p: (i, 0)),
                  pl.BlockSpec((tb, d), lambda i, p: (p[0] * n + i, 0)),
                  pl.BlockSpec(memory_space=pl.ANY),
                  pl.BlockSpec((LANES, d), lambda i, p: ((p[0] + 1) * tiles + tile_of(i), 0)),
                  pl.BlockSpec((1, d), lambda i, p: (0, 0)),
                  pl.BlockSpec((d, hq), lambda i, p: (0, head_of(i))),
                  pl.BlockSpec((1, 2, PEER_N_KEYS, PEER_HALF), lambda i, p: (head_of(i), 0, 0, 0))],
        out_specs=[tok_spec,
                   pl.BlockSpec((LANES, d), lambda i, p: (tile_of(i), 0)),
                   pl.BlockSpec((LANES, PEER_SEL), lambda i, p: (tile_of(i), 0)),
                   pl.BlockSpec((LANES, PEER_SEL), lambda i, p: (tile_of(i), 0))],
        scratch_shapes=[pltpu.VMEM((PEER_SLOTS, tb * PEER_SEL * ROW_PITCH, LANES), jnp.uint32),
                        pltpu.SemaphoreType.DMA((PEER_SLOTS,)),
                        pltpu.VMEM((LANES, d), BF16),
                        pltpu.VMEM((2, PEER_N_KEYS, LANES), F32),
                        pltpu.VMEM((1, PEER_SEL, LANES), jnp.int32),
                        pltpu.VMEM((1, PEER_SEL, LANES), F32)])
    return pl.pallas_call(
        _peer_fused_body,
        grid_spec=grid_spec,
        out_shape=[jax.ShapeDtypeStruct((tp, d), F32),
                   jax.ShapeDtypeStruct((tp, d), F32),
                   jax.ShapeDtypeStruct((tp, PEER_SEL), jnp.int32),
                   jax.ShapeDtypeStruct((tp, PEER_SEL), F32)],
        compiler_params=_cparams(("arbitrary",)),
        name="peer_expert_route",
    )(jnp.reshape(part, (1,)).astype(jnp.int32), idx_flat, idx_flat, hn, gate_tok, h_full, uv_tab, h_full,
      nw.reshape(1, d), wq_bf16, subkeys_bf16)


def _rmsnorm_body(x_ref, w_ref, o_ref):
    o_ref[...] = _rms(x_ref[...], w_ref[...])


def rmsnorm(x, w, tm=1024):
    t, d = x.shape
    tm = min(tm, t)
    return pl.pallas_call(
        _rmsnorm_body,
        grid=(t // tm,),
        in_specs=[pl.BlockSpec((tm, d), lambda i: (i, 0)), pl.BlockSpec((1, d), lambda i: (0, 0))],
        out_specs=pl.BlockSpec((tm, d), lambda i: (i, 0)),
        out_shape=jax.ShapeDtypeStruct((t, d), F32),
        compiler_params=_cparams(("parallel",)),
        name="rmsnorm",
    )(x, w.reshape(1, d))


def _pad_cols(v, start, width=SMALL_W):
    out = jnp.zeros((1, width), F32)
    return out.at[0, start:start + v.shape[0]].set(v.astype(F32))


def _even_mix(h, bsz, norm_w, w_in, w_out, ssd_conv_w, ssd_conv_b, ssd_dt_bias, ssd_a_log, ssd_d_skip,
              ssd_norm, ml_conv_w, ml_conv_b, ml_wq, ml_wk, ml_wv, ml_i_bias, ml_f_bias, ml_norm):
    d = D_MODEL
    o1 = d
    o2 = o1 + SSD_XBC
    o3 = o2 + SSD_HEADS
    o4 = o3 + d
    o5 = o4 + d
    o6 = o5 + MLSTM_HEADS
    o7 = o6 + MLSTM_HEADS
    w_small = jnp.concatenate([w_in[:, o2:o3], w_in[:, o5:o6], w_in[:, o6:o7],
                               jnp.zeros((d, SMALL_W - (o3 - o2) - (o7 - o5)), w_in.dtype)], axis=1)
    w_all = jnp.concatenate([w_in[:, :o1], w_in[:, o1:o2], w_in[:, o3:o4], w_in[:, o4:o5], w_small],
                            axis=1).astype(BF16)
    z, xbc, xm, og, small = norm_proj(h, norm_w, w_all, (d, SSD_XBC, d, d, SMALL_W), tm=256)
    bias_pad = (_pad_cols(ssd_dt_bias, SMALL_DT) + _pad_cols(ml_i_bias, SMALL_IG)
                + _pad_cols(ml_f_bias, SMALL_FG))
    y_a = ssd_mixer(z, xbc, small, ssd_conv_w, ssd_conv_b, bias_pad, _pad_cols(ssd_a_log, SMALL_DT),
                    _pad_cols(ssd_d_skip, SMALL_DT), ssd_norm, bsz)
    y_b = mlstm_mixer(xm, og, small, ml_conv_w, ml_conv_b, bias_pad, ml_wq.astype(BF16),
                      ml_wk.astype(BF16), ml_wv.astype(BF16), ml_norm, bsz)
    w_out_b = w_out.astype(BF16)
    return proj_residual([y_a, y_b], [w_out_b[:d], w_out_b[d:]], h)


def _odd_mix(h, bsz, norm_w, w_in, w_out, lb_logits, hg_norm):
    d = D_MODEL
    uq, uf, ui, ug = norm_proj(h, norm_w, w_in.astype(BF16), (d, d, d, d), tm=256)
    y = hgrn_mixer(uq, uf, ui, ug, lb_logits, hg_norm, bsz)
    return proj_residual([y], [w_out.astype(BF16)], h)


def _xattn(h, mem2d, bsz, norm_x, norm_m, wq, wkv, wo):
    k, v = norm_proj(mem2d, norm_m, wkv.astype(BF16), (D_MODEL, D_MODEL), out_dtype=BF16)
    return xattn_residual(h, norm_x, wq.astype(BF16), k, v, wo.astype(BF16), bsz)


def _peer(h, norm_w, wq, subkeys, u_tab, v_tab):
    t = h.shape[0]
    n_exp = u_tab.shape[0]
    wq_b = wq.astype(BF16)
    sk_b = subkeys.astype(BF16)
    u_bits = lax.bitcast_convert_type(u_tab.astype(BF16), jnp.uint16).astype(jnp.uint32)
    v_bits = lax.bitcast_convert_type(v_tab.astype(BF16), jnp.uint16).astype(jnp.uint32)
    uv_tab = ((u_bits << 16) | v_bits).reshape(n_exp, ROW_CHUNKS, LANES)
    parts = PEER_PARTS
    while t % (parts * LANES) != 0:
        parts //= 2
    tp = t // parts
    hn, idx, gate = peer_route(h[:tp], norm_w, wq_b, sk_b)
    outs = []
    if parts > 1:
        def one_range(routed, k):
            hn_k, idx_k, gate_k = routed
            o, hn_n, idx_n, gate_n = peer_expert_route_fused(idx_k.reshape(tp * PEER_SEL), hn_k, gate_k, h, uv_tab,
                                                             norm_w, wq_b, sk_b, k)
            return (hn_n, idx_n, gate_n), o

        (hn, idx, gate), fused_out = lax.scan(one_range, (hn, idx, gate), jnp.arange(parts - 1, dtype=jnp.int32))
        outs.append(fused_out.reshape((parts - 1) * tp, D_MODEL))
    outs.append(peer_expert_residual(idx.reshape(tp * PEER_SEL), hn, gate, h[(parts - 1) * tp:], uv_tab))
    return jnp.concatenate(outs, axis=0)


def kernel(x, mem, norm_mix, norm_xattn, norm_mem, norm_ffn, norm_final, ev_w_in, ev_w_out, ssd_conv_w, ssd_conv_b, ssd_dt_bias, ssd_a_log, ssd_d_skip, ssd_norm, ml_conv_w, ml_conv_b, ml_wq, ml_wk, ml_wv, ml_i_bias, ml_f_bias, ml_norm, od_w_in, od_w_out, hgrn_lb_logits, hgrn_norm, xa_wq, xa_wkv, xa_wo, peer_wq, peer_subkeys, peer_u, peer_v):
    bsz, seq, d = x.shape
    depth = norm_mix.shape[0]
    assert depth == 2 and hgrn_lb_logits.shape[0] == 2
    h = x.reshape(bsz * seq, d)
    mem2d = mem.reshape(bsz * MEM_LEN, d)
    for layer in range(depth):
        if layer % 2 == 0:
            e = layer // 2
            h = _even_mix(h, bsz, norm_mix[layer], ev_w_in[e], ev_w_out[e], ssd_conv_w[e], ssd_conv_b[e],
                          ssd_dt_bias[e], ssd_a_log[e], ssd_d_skip[e], ssd_norm[e], ml_conv_w[e],
                          ml_conv_b[e], ml_wq[e], ml_wk[e], ml_wv[e], ml_i_bias[e], ml_f_bias[e], ml_norm[e])
        else:
            o = layer // 2
            h = _odd_mix(h, bsz, norm_mix[layer], od_w_in[o], od_w_out[o], hgrn_lb_logits, hgrn_norm[o])
        h = _xattn(h, mem2d, bsz, norm_xattn[layer], norm_mem[layer], xa_wq[layer], xa_wkv[layer], xa_wo[layer])
        h = _peer(h, norm_ffn[layer], peer_wq[layer], peer_subkeys[layer], peer_u[layer], peer_v[layer])
    return rmsnorm(h, norm_final).reshape(bsz, seq, d)
```

```python
import functools

import jax
import jax.numpy as jnp
from jax import lax
from jax.experimental import pallas as pl
from jax.experimental.pallas import tpu as pltpu

F32 = jnp.float32
BF16 = jnp.bfloat16
HIGHEST = lax.Precision.HIGHEST

D_MODEL = 1024
EPS = 1e-6
CONV_WIDTH = 4
MEM_LEN = 256
SSD_HEAD_DIM = 64
SSD_HEADS = 16
SSD_GROUPS = 2
SSD_STATE = 128
SSD_XBC = D_MODEL + 2 * SSD_GROUPS * SSD_STATE
SSD_CHUNK = 128
MLSTM_HEADS = 4
MLSTM_HEAD_DIM = 256
MLSTM_CHUNK = 128
HGRN_HEADS = 8
HGRN_HEAD_DIM = 128
HGRN_CHUNK = 64
XATTN_HEADS = 4
XATTN_HEAD_DIM = 256
PEER_HEADS = 8
PEER_N_KEYS = 128
PEER_TOPK = 16
PEER_HALF = 128
PEER_SEL = PEER_HEADS * PEER_TOPK

SUBLANES = 8
LANES = 128
SMALL_W = LANES
SMALL_DT = 0
SMALL_IG = SSD_HEADS
SMALL_FG = SSD_HEADS + MLSTM_HEADS

VMEM_LIMIT = 52 * 1024 * 1024


def _cparams(sem):
    return pltpu.CompilerParams(dimension_semantics=sem, vmem_limit_bytes=VMEM_LIMIT)


def _sigmoid(x):
    return 1.0 / (1.0 + jnp.exp(-x))


def _softplus(x):
    return jnp.maximum(x, 0.0) + jnp.log(1.0 + jnp.exp(-jnp.abs(x)))


def _rms(x, w):
    return x * lax.rsqrt(jnp.mean(x * x, axis=-1, keepdims=True) + EPS) * w


def _dot(a, b):
    return jnp.dot(a.astype(BF16), b.astype(BF16), preferred_element_type=F32)


def _dot_nt(a, b):
    return lax.dot_general(a.astype(BF16), b.astype(BF16), (((1,), (1,)), ((), ())),
                           preferred_element_type=F32)


def _cumsum_rows(x):
    n = x.shape[0]
    r = lax.broadcasted_iota(jnp.int32, (n, n), 0)
    c = lax.broadcasted_iota(jnp.int32, (n, n), 1)
    tril = (r >= c).astype(F32)
    return jnp.dot(tril, x, precision=HIGHEST, preferred_element_type=F32)


def _causal(n):
    r = lax.broadcasted_iota(jnp.int32, (n, n), 0)
    c = lax.broadcasted_iota(jnp.int32, (n, n), 1)
    return r >= c


def _norm_proj_body(x_ref, nw_ref, w_ref, *out_refs, splits):
    xn = _rms(x_ref[...], nw_ref[...]).astype(BF16)
    off = 0
    for o_ref, n in zip(out_refs, splits):
        o_ref[...] = jnp.dot(xn, w_ref[:, off:off + n], preferred_element_type=F32).astype(o_ref.dtype)
        off += n


def norm_proj(x, nw, w_bf16, splits, tm=512, out_dtype=F32):
    t, d = x.shape
    n = w_bf16.shape[1]
    assert sum(splits) == n
    tm = min(tm, t)
    assert t % tm == 0
    return pl.pallas_call(
        functools.partial(_norm_proj_body, splits=tuple(splits)),
        grid=(t // tm,),
        in_specs=[pl.BlockSpec((tm, d), lambda i: (i, 0)),
                  pl.BlockSpec((1, d), lambda i: (0, 0)),
                  pl.BlockSpec((d, n), lambda i: (0, 0))],
        out_specs=[pl.BlockSpec((tm, s), lambda i: (i, 0)) for s in splits],
        out_shape=[jax.ShapeDtypeStruct((t, s), out_dtype) for s in splits],
        compiler_params=_cparams(("parallel",)),
        name="norm_proj",
    )(x, nw.reshape(1, d), w_bf16)


def _proj_residual_body(*refs, n_parts):
    a_refs = refs[:n_parts]
    w_refs = refs[n_parts:2 * n_parts]
    r_ref = refs[2 * n_parts]
    o_ref = refs[2 * n_parts + 1]
    acc = r_ref[...]
    for a_ref, w_ref in zip(a_refs, w_refs):
        acc = acc + jnp.dot(a_ref[...].astype(BF16), w_ref[...], preferred_element_type=F32)
    o_ref[...] = acc


def proj_residual(parts, ws_bf16, res, tm=512):
    t, d = res.shape
    tm = min(tm, t)
    n_parts = len(parts)
    in_specs = ([pl.BlockSpec((tm, p.shape[1]), lambda i: (i, 0)) for p in parts]
                + [pl.BlockSpec(w.shape, lambda i: (0, 0)) for w in ws_bf16]
                + [pl.BlockSpec((tm, d), lambda i: (i, 0))])
    return pl.pallas_call(
        functools.partial(_proj_residual_body, n_parts=n_parts),
        grid=(t // tm,),
        in_specs=in_specs,
        out_specs=pl.BlockSpec((tm, d), lambda i: (i, 0)),
        out_shape=jax.ShapeDtypeStruct((t, d), F32),
        compiler_params=_cparams(("parallel",)),
        name="proj_residual",
    )(*parts, *ws_bf16, res)


def _conv_silu(x_in, xpad_ref, cw_ref, cb_ref, n_rows):
    xpad_ref[SUBLANES:SUBLANES + n_rows, :] = x_in
    acc = cb_ref[...] + cw_ref[CONV_WIDTH - 1:CONV_WIDTH, :] * x_in
    for k in range(CONV_WIDTH - 1):
        start = SUBLANES - (CONV_WIDTH - 1) + k
        acc = acc + cw_ref[k:k + 1, :] * xpad_ref[start:start + n_rows, :]
    xpad_ref[0:SUBLANES, :] = x_in[n_rows - SUBLANES:n_rows, :]
    return acc * _sigmoid(acc)


def _ssd_body(z_ref, xbc_ref, sm_ref, cw_ref, cb_ref, bias_ref, alog_ref, dsk_ref, nw_ref,
              o_ref, xpad_ref, state_ref, ybuf_ref):
    L, P, N = SSD_CHUNK, SSD_HEAD_DIM, SSD_STATE
    heads_per_group = SSD_HEADS // SSD_GROUPS

    @pl.when(pl.program_id(1) == 0)
    def _init():
        xpad_ref[0:SUBLANES, :] = jnp.zeros((SUBLANES, SSD_XBC), F32)
        state_ref[...] = jnp.zeros_like(state_ref)

    xa = _conv_silu(xbc_ref[...], xpad_ref, cw_ref, cb_ref, L)
    dt = _softplus(sm_ref[...] + bias_ref[...])
    a = -jnp.exp(alog_ref[...])
    cs_col = _cumsum_rows(dt * a)
    cs_row = cs_col.T
    causal = _causal(L)
    first_head = lax.broadcasted_iota(jnp.int32, (L, 2 * P), 1) < P

    for g in range(SSD_GROUPS):
        b_g = xa[:, D_MODEL + g * N:D_MODEL + (g + 1) * N]
        c_g = xa[:, D_MODEL + (SSD_GROUPS + g) * N:D_MODEL + (SSD_GROUPS + g + 1) * N]
        cb = _dot_nt(c_g, b_g)
        b_gt = b_g.T.astype(BF16)
        c_gb = c_g.astype(BF16)
        for e2 in range(heads_per_group // 2):
            pair = (g * heads_per_group) // 2 + e2
            col0 = SMALL_DT + 2 * pair
            col1 = col0 + 1

            def per_lane(arr, r0=None):
                a0 = arr[:, col0:col0 + 1] if r0 is None else arr[r0:r0 + 1, col0:col0 + 1]
                a1 = arr[:, col1:col1 + 1] if r0 is None else arr[r0:r0 + 1, col1:col1 + 1]
                return jnp.where(first_head[0:a0.shape[0], :], a0, a1)

            cs_c = per_lane(cs_col)
            cs_last = per_lane(cs_col, L - 1)
            x_p = xa[:, pair * 2 * P:(pair + 1) * 2 * P]
            x_dt = x_p * per_lane(dt)
            x_dtb = x_dt.astype(BF16)
            ys = []
            for col in (col0, col1):
                decay = jnp.exp(jnp.where(causal, cs_col[:, col:col + 1] - cs_row[col:col + 1, :], -jnp.inf))
                ys.append(jnp.dot((cb * decay).astype(BF16), x_dtb, preferred_element_type=F32))
            y = jnp.where(first_head, ys[0], ys[1])
            s_prev = state_ref[pair]
            y = y + jnp.dot(c_gb, s_prev.astype(BF16), preferred_element_type=F32) * jnp.exp(cs_c)
            y = y + x_p * per_lane(dsk_ref[...])
            ybuf_ref[:, pair * 2 * P:(pair + 1) * 2 * P] = y
            x_end = (x_dt * jnp.exp(cs_last - cs_c)).astype(BF16)
            state_ref[pair] = jnp.exp(cs_last) * s_prev + jnp.dot(b_gt, x_end, preferred_element_type=F32)

    zz = z_ref[...]
    y = ybuf_ref[...] * (zz * _sigmoid(zz))
    gw = D_MODEL // SSD_GROUPS
    for g in range(SSD_GROUPS):
        o_ref[:, g * gw:(g + 1) * gw] = _rms(y[:, g * gw:(g + 1) * gw], nw_ref[:, g * gw:(g + 1) * gw])


def ssd_mixer(z, xbc, small, conv_w, conv_b, bias_pad, alog_pad, dskip_pad, norm_w, bsz):
    t = z.shape[0]
    L = SSD_CHUNK
    nc = t // bsz // L
    row = lambda b, c: (b * nc + c, 0)
    const = lambda b, c: (0, 0)
    return pl.pallas_call(
        _ssd_body,
        grid=(bsz, nc),
        in_specs=[pl.BlockSpec((L, D_MODEL), row),
                  pl.BlockSpec((L, SSD_XBC), row),
                  pl.BlockSpec((L, SMALL_W), row),
                  pl.BlockSpec((CONV_WIDTH, SSD_XBC), const),
                  pl.BlockSpec((1, SSD_XBC), const),
                  pl.BlockSpec((1, SMALL_W), const),
                  pl.BlockSpec((1, SMALL_W), const),
                  pl.BlockSpec((1, SMALL_W), const),
                  pl.BlockSpec((1, D_MODEL), const)],
        out_specs=pl.BlockSpec((L, D_MODEL), row),
        out_shape=jax.ShapeDtypeStruct((t, D_MODEL), F32),
        scratch_shapes=[pltpu.VMEM((L + SUBLANES, SSD_XBC), F32),
                        pltpu.VMEM((SSD_HEADS // 2, SSD_STATE, 2 * SSD_HEAD_DIM), F32),
                        pltpu.VMEM((L, D_MODEL), F32)],
        compiler_params=_cparams(("parallel", "arbitrary")),
        name="ssd_mixer",
    )(z, xbc, small, conv_w, conv_b.reshape(1, -1), bias_pad, alog_pad, dskip_pad, norm_w.reshape(1, -1))


def _mlstm_body(xm_ref, og_ref, sm_ref, cw_ref, cb_ref, bias_ref, wq_ref, wk_ref, wv_ref, nw_ref,
                o_ref, xpad_ref, c_ref, n_ref, m_ref):
    L, H, Dh = MLSTM_CHUNK, MLSTM_HEADS, MLSTM_HEAD_DIM

    @pl.when(pl.program_id(1) == 0)
    def _init():
        xpad_ref[0:SUBLANES, :] = jnp.zeros((SUBLANES, D_MODEL), F32)
        c_ref[...] = jnp.zeros_like(c_ref)
        n_ref[...] = jnp.zeros_like(n_ref)
        m_ref[...] = jnp.zeros_like(m_ref)

    xm = xm_ref[...]
    xc = _conv_silu(xm, xpad_ref, cw_ref, cb_ref, L)
    smb = sm_ref[...] + bias_ref[...]
    logf = jnp.minimum(smb, 0.0) - jnp.log(1.0 + jnp.exp(-jnp.abs(smb)))
    bcum_col = _cumsum_rows(logf)
    bcum_row = bcum_col.T
    smb_row = smb.T
    causal = _causal(L)

    for h in range(H):
        ci, cf = SMALL_IG + h, SMALL_FG + h
        bc = bcum_col[:, cf:cf + 1]
        br = bcum_row[cf:cf + 1, :]
        ic = smb[:, ci:ci + 1]
        ir = smb_row[ci:ci + 1, :]
        b_end = bcum_col[L - 1:L, cf:cf + 1]
        dlog = jnp.where(causal, bc - br + ir, -jnp.inf)
        m_intra = jnp.max(dlog, axis=1, keepdims=True)
        g_end = b_end - bc + ic
        m_chunk = jnp.max(g_end, axis=0, keepdims=True)
        w_end = jnp.exp(g_end - m_chunk)

        sl = slice(h * Dh, (h + 1) * Dh)
        q = _dot(xc[:, sl], wq_ref[h])
        k = _dot(xc[:, sl], wk_ref[h]) * (Dh ** -0.5)
        v = _dot(xm[:, sl], wv_ref[h])
        qb, kb, vb = q.astype(BF16), k.astype(BF16), v.astype(BF16)

        m_prev = m_ref[h:h + 1, 0:1]
        c_prev = c_ref[h]
        n_prev = n_ref[h:h + 1, :]
        m_t = jnp.maximum(bc + m_prev, m_intra)
        a_t = jnp.exp(bc + m_prev - m_t)
        qk = _dot_nt(qb, kb) * jnp.exp(dlog - m_t)
        num = a_t * jnp.dot(qb, c_prev.astype(BF16), preferred_element_type=F32) \
            + jnp.dot(qk.astype(BF16), vb, preferred_element_type=F32)
        den = a_t * jnp.sum(q * n_prev, axis=1, keepdims=True) + jnp.sum(qk, axis=1, keepdims=True)
        hid = num / jnp.maximum(jnp.abs(den), jnp.exp(-m_t))
        hid = _rms(hid, nw_ref[:, sl])
        og = og_ref[:, sl]
        o_ref[:, sl] = hid * _sigmoid(og)

        m_new = jnp.maximum(b_end + m_prev, m_chunk)
        a_old = jnp.exp(b_end + m_prev - m_new)
        a_new = jnp.exp(m_chunk - m_new)
        kw = k * w_end
        c_local = jnp.dot(kw.T.astype(BF16), vb, preferred_element_type=F32)
        c_ref[h] = a_old * c_prev + a_new * c_local
        n_ref[h:h + 1, :] = a_old * n_prev + a_new * jnp.sum(kw, axis=0, keepdims=True)
        m_ref[h:h + 1, :] = jnp.broadcast_to(m_new, (1, LANES))


def mlstm_mixer(xm, og, small, conv_w, conv_b, bias_pad, wq, wk, wv, norm_w, bsz):
    t = xm.shape[0]
    L = MLSTM_CHUNK
    nc = t // bsz // L
    row = lambda b, c: (b * nc + c, 0)
    const = lambda b, c: (0, 0)
    const3 = lambda b, c: (0, 0, 0)
    wspec = pl.BlockSpec((MLSTM_HEADS, MLSTM_HEAD_DIM, MLSTM_HEAD_DIM), const3)
    return pl.pallas_call(
        _mlstm_body,
        grid=(bsz, nc),
        in_specs=[pl.BlockSpec((L, D_MODEL), row),
                  pl.BlockSpec((L, D_MODEL), row),
                  pl.BlockSpec((L, SMALL_W), row),
                  pl.BlockSpec((CONV_WIDTH, D_MODEL), const),
                  pl.BlockSpec((1, D_MODEL), const),
                  pl.BlockSpec((1, SMALL_W), const),
                  wspec, wspec, wspec,
                  pl.BlockSpec((1, D_MODEL), const)],
        out_specs=pl.BlockSpec((L, D_MODEL), row),
        out_shape=jax.ShapeDtypeStruct((t, D_MODEL), F32),
        scratch_shapes=[pltpu.VMEM((L + SUBLANES, D_MODEL), F32),
                        pltpu.VMEM((MLSTM_HEADS, MLSTM_HEAD_DIM, MLSTM_HEAD_DIM), F32),
                        pltpu.VMEM((SUBLANES, MLSTM_HEAD_DIM), F32),
                        pltpu.VMEM((SUBLANES, LANES), F32)],
        compiler_params=_cparams(("parallel", "arbitrary")),
        name="mlstm_mixer",
    )(xm, og, small, conv_w, conv_b.reshape(1, -1), bias_pad, wq, wk, wv, norm_w.reshape(1, -1))


def _hgrn_body(uq_ref, uf_ref, ui_ref, ug_ref, lbl_ref, nw_ref, o_ref, s_ref):
    L, H, Dk = HGRN_CHUNK, HGRN_HEADS, HGRN_HEAD_DIM

    @pl.when(pl.program_id(1) == 0)
    def _init():
        s_ref[...] = jnp.zeros_like(s_ref)

    lg = lbl_ref[...]
    lmax = jnp.max(lg, axis=0, keepdims=True)
    le = jnp.exp(lg - lmax)
    lb = le[0:1, :] / jnp.sum(le, axis=0, keepdims=True)

    uf = uf_ref[...]
    uq = uq_ref[...]
    logf = jnp.log(lb + (1.0 - lb) * _sigmoid(uf))
    k = (1.0 - lb) * _sigmoid(-uf)
    q = uq * _sigmoid(uq)
    v = ui_ref[...]
    gc = _cumsum_rows(logf)
    g_ref = gc[L // 2:L // 2 + 1, :]
    g_last = gc[L - 1:L, :]
    q_in = (q * jnp.exp(gc - g_ref)).astype(BF16)
    k_in = (k * jnp.exp(g_ref - gc)).astype(BF16)
    q_off = (q * jnp.exp(gc)).astype(BF16)
    k_end = k * jnp.exp(g_last - gc)
    decay = jnp.exp(g_last)
    causal = _causal(L)
    vb = v.astype(BF16)
    ug = ug_ref[...]

    for h in range(H):
        sl = slice(h * Dk, (h + 1) * Dk)
        att = jnp.where(causal, _dot_nt(q_in[:, sl], k_in[:, sl]), 0.0)
        st_prev = s_ref[h]
        o = jnp.dot(att.astype(BF16), vb[:, sl], preferred_element_type=F32) \
            + _dot_nt(q_off[:, sl], st_prev)
        s_ref[h] = decay[:, sl] * st_prev + _dot(v[:, sl].T, k_end[:, sl])
        g = ug[:, sl]
        o_ref[:, sl] = _rms(o, nw_ref[:, sl]) * (g * _sigmoid(g))


def hgrn_mixer(uq, uf, ui, ug, lb_logits, norm_w, bsz):
    t = uq.shape[0]
    L = HGRN_CHUNK
    nc = t // bsz // L
    row = lambda b, c: (b * nc + c, 0)
    const = lambda b, c: (0, 0)
    rs = pl.BlockSpec((L, D_MODEL), row)
    return pl.pallas_call(
        _hgrn_body,
        grid=(bsz, nc),
        in_specs=[rs, rs, rs, rs,
                  pl.BlockSpec((2, D_MODEL), const),
                  pl.BlockSpec((1, D_MODEL), const)],
        out_specs=rs,
        out_shape=jax.ShapeDtypeStruct((t, D_MODEL), F32),
        scratch_shapes=[pltpu.VMEM((HGRN_HEADS, HGRN_HEAD_DIM, HGRN_HEAD_DIM), F32)],
        compiler_params=_cparams(("parallel", "arbitrary")),
        name="hgrn_mixer",
    )(uq, uf, ui, ug, lb_logits, norm_w.reshape(1, -1))


def _xattn_body(h_ref, nw_ref, wq_ref, k_ref, v_ref, wo_ref, o_ref, obuf_ref):
    hv = h_ref[...]
    hn = _rms(hv, nw_ref[...]).astype(BF16)
    q = jnp.dot(hn, wq_ref[...], preferred_element_type=F32)
    scale = XATTN_HEAD_DIM ** -0.5
    for a in range(XATTN_HEADS):
        sl = slice(a * XATTN_HEAD_DIM, (a + 1) * XATTN_HEAD_DIM)
        s = _dot_nt(q[:, sl], k_ref[:, sl]) * scale
        s = s - jnp.max(s, axis=1, keepdims=True)
        p = jnp.exp(s)
        p = p / jnp.sum(p, axis=1, keepdims=True)
        obuf_ref[:, sl] = jnp.dot(p.astype(BF16), v_ref[:, sl], preferred_element_type=F32).astype(BF16)
    o_ref[...] = hv + jnp.dot(obuf_ref[...], wo_ref[...], preferred_element_type=F32)


def xattn_residual(h, nw, wq_bf16, k_bf16, v_bf16, wo_bf16, bsz, tm=512):
    t, d = h.shape
    seq = t // bsz
    tm = min(tm, seq)
    nt = seq // tm
    row = lambda b, i: (b * nt + i, 0)
    const = lambda b, i: (0, 0)
    mem = lambda b, i: (b, 0)
    return pl.pallas_call(
        _xattn_body,
        grid=(bsz, nt),
        in_specs=[pl.BlockSpec((tm, d), row),
                  pl.BlockSpec((1, d), const),
                  pl.BlockSpec((d, d), const),
                  pl.BlockSpec((MEM_LEN, d), mem),
                  pl.BlockSpec((MEM_LEN, d), mem),
                  pl.BlockSpec((d, d), const)],
        out_specs=pl.BlockSpec((tm, d), row),
        out_shape=jax.ShapeDtypeStruct((t, d), F32),
        scratch_shapes=[pltpu.VMEM((tm, d), BF16)],
        compiler_params=_cparams(("parallel", "parallel")),
        name="xattn_residual",
    )(h, nw.reshape(1, d), wq_bf16, k_bf16, v_bf16, wo_bf16)


def _topk_rows_steps(arrays, k):
    arrays = list(arrays)
    iotas = [lax.broadcasted_iota(jnp.int32, s.shape, 0) for s in arrays]
    vals = [[] for _ in arrays]
    idxs = [[] for _ in arrays]
    for _ in range(k):
        for n, s in enumerate(arrays):
            m = jnp.max(s, axis=0, keepdims=True)
            i = jnp.min(jnp.where(s == m, iotas[n], s.shape[0]), axis=0, keepdims=True)
            vals[n].append(m)
            idxs[n].append(i)
            arrays[n] = jnp.where(iotas[n] == i, -jnp.inf, s)
        yield
    return [(jnp.concatenate(v, axis=0), jnp.concatenate(i, axis=0)) for v, i in zip(vals, idxs)]


def _run(steps):
    try:
        while True:
            next(steps)
    except StopIteration as done:
        return done.value


def _select_rows(table, sel, n):
    out = jnp.zeros(sel.shape, jnp.int32)
    for a in range(n):
        out = jnp.where(sel == a, table[a:a + 1, :], out)
    return out


def _route_norm(head, h_ref, nw_ref, hn_ref, hnb_ref):
    @pl.when(head == 0)
    def _norm():
        xn = _rms(h_ref[...], nw_ref[...])
        hn_ref[...] = xn
        hnb_ref[...] = xn.astype(BF16)


def _route_steps(head, wq_ref, sk_ref, hnb_ref, st_ref, oi_ref, og_ref, n_col_blocks):
    K = PEER_TOPK
    q = jnp.dot(hnb_ref[...], wq_ref[...], preferred_element_type=F32)
    for p in range(2):
        qp = q[:, p * PEER_HALF:(p + 1) * PEER_HALF]
        st = _dot_nt(sk_ref[0, p], qp)
        for cb in range(n_col_blocks):
            st_ref[p * n_col_blocks + cb] = st[:, cb * LANES:(cb + 1) * LANES]
    yield
    tops = yield from _topk_rows_steps([st_ref[j] for j in range(2 * n_col_blocks)], K)

    sub = lax.broadcasted_iota(jnp.int32, (SUBLANES, LANES), 0)
    cands = []
    for cb in range(n_col_blocks):
        v1, v2 = tops[cb][0], tops[n_col_blocks + cb][0]
        blocks = [v1[0:1, :] + v2[0:SUBLANES, :], v1[0:1, :] + v2[SUBLANES:K, :]]
        for a in range(1, SUBLANES):
            blocks.append(jnp.where(sub < K // (a + 1), v1[a:a + 1, :] + v2[0:SUBLANES, :], -jnp.inf))
        blocks.append(v1[SUBLANES:K, :] + v2[0:1, :])
        cands.append(jnp.concatenate(blocks, axis=0))
    yield
    cand_tops = yield from _topk_rows_steps(cands, K)

    row0 = pl.multiple_of(head * K, K)
    for cb in range(n_col_blocks):
        i1, i2 = tops[cb][1], tops[n_col_blocks + cb][1]
        top_s, pos = cand_tops[cb]
        blk = lax.shift_right_logical(pos, 3)
        within = lax.bitwise_and(pos, SUBLANES - 1)
        a_sel = jnp.where(blk <= 1, 0, jnp.where(blk <= SUBLANES, blk - 1, within + SUBLANES))
        b_sel = jnp.where(blk == 1, within + SUBLANES, jnp.where(blk <= SUBLANES, within, 0))
        e1 = _select_rows(i1, a_sel, K)
        e2 = _select_rows(i2, b_sel, K)
        oi_ref[cb, pl.ds(row0, K), :] = e1 * PEER_N_KEYS + e2
        ex = jnp.exp(top_s - jnp.max(top_s, axis=0, keepdims=True))
        og_ref[cb, pl.ds(row0, K), :] = ex / jnp.sum(ex, axis=0, keepdims=True)


def _route_emit(head, idx_ref, gate_ref, oi_ref, og_ref, n_col_blocks):
    @pl.when(head == PEER_HEADS - 1)
    def _emit():
        for cb in range(n_col_blocks):
            idx_ref[cb * LANES:(cb + 1) * LANES, :] = oi_ref[cb].T
            gate_ref[cb * LANES:(cb + 1) * LANES, :] = og_ref[cb].T


def _peer_route_body(h_ref, nw_ref, wq_ref, sk_ref, hn_ref, idx_ref, gate_ref,
                     hnb_ref, st_ref, oi_ref, og_ref, *, n_col_blocks):
    head = pl.program_id(1)
    _route_norm(head, h_ref, nw_ref, hn_ref, hnb_ref)
    _run(_route_steps(head, wq_ref, sk_ref, hnb_ref, st_ref, oi_ref, og_ref, n_col_blocks))
    _route_emit(head, idx_ref, gate_ref, oi_ref, og_ref, n_col_blocks)


def peer_route(h, nw, wq_bf16, subkeys_bf16, tm=512):
    t, d = h.shape
    tm = min(tm, t)
    ncb = tm // LANES
    hq = 2 * PEER_HALF
    return pl.pallas_call(
        functools.partial(_peer_route_body, n_col_blocks=ncb),
        grid=(t // tm, PEER_HEADS),
        in_specs=[pl.BlockSpec((tm, d), lambda i, hd: (i, 0)),
                  pl.BlockSpec((1, d), lambda i, hd: (0, 0)),
                  pl.BlockSpec((d, hq), lambda i, hd: (0, hd)),
                  pl.BlockSpec((1, 2, PEER_N_KEYS, PEER_HALF), lambda i, hd: (hd, 0, 0, 0))],
        out_specs=[pl.BlockSpec((tm, d), lambda i, hd: (i, 0)),
                   pl.BlockSpec((tm, PEER_SEL), lambda i, hd: (i, 0)),
                   pl.BlockSpec((tm, PEER_SEL), lambda i, hd: (i, 0))],
        out_shape=[jax.ShapeDtypeStruct((t, d), F32),
                   jax.ShapeDtypeStruct((t, PEER_SEL), jnp.int32),
                   jax.ShapeDtypeStruct((t, PEER_SEL), F32)],
        scratch_shapes=[pltpu.VMEM((tm, d), BF16),
                        pltpu.VMEM((2 * ncb, PEER_N_KEYS, LANES), F32),
                        pltpu.VMEM((ncb, PEER_SEL, LANES), jnp.int32),
                        pltpu.VMEM((ncb, PEER_SEL, LANES), F32)],
        compiler_params=_cparams(("parallel", "arbitrary")),
        name="peer_route",
    )(h, nw.reshape(1, d), wq_bf16, subkeys_bf16)


PEER_TOK_BLOCK = 16
PEER_PARTS = 8
PEER_SLOTS = 3
FUSED_ROUTE_EVERY = 7
PEER_LOOKAHEAD = PEER_SLOTS - 1
PEER_ISSUE_UNROLL = 16
ROW_CHUNKS = D_MODEL // LANES
UV_CHUNKS = ROW_CHUNKS
PIECES = 2 * ROW_CHUNKS
ROW_PITCH = UV_CHUNKS + 1
U_MASK = 0xFFFF0000


def _start_row_copy(uv_hbm, expert, buf, slot, row, sem, priority):
    pltpu.make_async_copy(uv_hbm.at[expert], buf.at[slot, pl.ds(row * ROW_PITCH, UV_CHUNKS), :],
                          sem.at[slot]).start(priority=priority)


def _wait_slot(buf, slot, sem):
    done = buf.at[slot, pl.ds(0, PEER_TOK_BLOCK * PEER_SEL * UV_CHUNKS), :]
    pltpu.make_async_copy(done, done, sem.at[slot]).wait()


def _expert_prime(i, idx_head_ref, uv_hbm, buf, sem):
    rows_per_block = PEER_TOK_BLOCK * PEER_SEL

    @pl.when(i == 0)
    def _prime():
        for b in range(PEER_LOOKAHEAD):
            def rows(ro, carry):
                for ri in range(PEER_ISSUE_UNROLL):
                    r = ro * PEER_ISSUE_UNROLL + ri
                    _start_row_copy(uv_hbm, idx_head_ref[b * rows_per_block + r], buf, b, r, sem, ri % 2)
                return carry

            lax.fori_loop(0, rows_per_block // PEER_ISSUE_UNROLL, rows, 0)


def _expert_drain(i, n_blocks, buf, sem):
    @pl.when(i == n_blocks - 1)
    def _drain():
        for k in range(1, PEER_SLOTS):
            _wait_slot(buf, lax.rem(i + k, PEER_SLOTS), sem)


def _expert_main(i, idx_ahead_ref, x_ref, gate_ref, h_ref, uv_hbm, o_ref, buf, sem,
                 side_steps=None, side_every=1):
    tb = PEER_TOK_BLOCK
    slot = lax.rem(i, PEER_SLOTS)
    ahead_slot = lax.rem(i + PEER_LOOKAHEAD, PEER_SLOTS)
    _wait_slot(buf, slot, sem)
    gate_t = gate_ref[...].T
    group = PEER_SEL // PIECES
    pieces_done = 0

    for tk in range(tb):
        base = tk * PEER_SEL * ROW_PITCH

        def issue(piece):
            nonlocal pieces_done
            for k in range(group):
                r = tk * PEER_SEL + piece * group + k
                _start_row_copy(uv_hbm, idx_ahead_ref[r], buf, ahead_slot, r, sem, k % 2)
            pieces_done += 1
            if side_steps is not None and pieces_done % side_every == 0:
                next(side_steps, None)

        acc = None
        for c in range(ROW_CHUNKS):
            issue(c)
            uv_c = buf[slot, pl.ds(base + c, PEER_SEL, stride=ROW_PITCH), :]
            u_c = pltpu.bitcast(lax.bitwise_and(uv_c, jnp.uint32(U_MASK)), F32)
            term = u_c * x_ref[tk:tk + 1, c * LANES:(c + 1) * LANES]
            acc = term if acc is None else acc + term
        hsel = jnp.sum(acc, axis=1, keepdims=True)
        act = 0.5 * hsel * (1.0 + lax.erf(hsel * (2.0 ** -0.5)))
        w = gate_t[:, tk:tk + 1] * act
        outs = []
        for c in range(ROW_CHUNKS):
            issue(ROW_CHUNKS + c)
            uv_c = buf[slot, pl.ds(base + c, PEER_SEL, stride=ROW_PITCH), :]
            v_c = pltpu.bitcast(lax.shift_left(uv_c, jnp.uint32(16)), F32)
            outs.append(jnp.sum(v_c * w, axis=0, keepdims=True))
        o_ref[tk:tk + 1, :] = h_ref[tk:tk + 1, :] + jnp.concatenate(outs, axis=1)

    if side_steps is not None:
        for _ in side_steps:
            pass


def _peer_expert_body(idx_head_ref, idx_ahead_ref, x_ref, gate_ref, h_ref, uv_hbm, o_ref, buf, sem):
    i = pl.program_id(0)
    _expert_prime(i, idx_head_ref, uv_hbm, buf, sem)
    _expert_main(i, idx_ahead_ref, x_ref, gate_ref, h_ref, uv_hbm, o_ref, buf, sem)
    _expert_drain(i, pl.num_programs(0), buf, sem)


def _peer_fused_body(idx_head_ref, idx_ahead_ref, x_ref, gate_ref, h_ref, uv_hbm,
                     rh_ref, nw_ref, wq_ref, sk_ref,
                     o_ref, hn_ref, ridx_ref, rgate_ref,
                     buf, sem, hnb_ref, st_ref, oi_ref, og_ref):
    i = pl.program_id(0)
    head = lax.rem(i, PEER_HEADS)
    _expert_prime(i, idx_head_ref, uv_hbm, buf, sem)
    _route_norm(head, rh_ref, nw_ref, hn_ref, hnb_ref)
    route = _route_steps(head, wq_ref, sk_ref, hnb_ref, st_ref, oi_ref, og_ref, 1)
    _expert_main(i, idx_ahead_ref, x_ref, gate_ref, h_ref, uv_hbm, o_ref, buf, sem,
                 side_steps=route, side_every=FUSED_ROUTE_EVERY)
    _route_emit(head, ridx_ref, rgate_ref, oi_ref, og_ref, 1)
    _expert_drain(i, pl.num_programs(0), buf, sem)


def peer_expert_residual(idx_flat, hn, gate_tok, h, uv_tab):
    t, d = h.shape
    tb = PEER_TOK_BLOCK
    assert t % tb == 0
    n = t // tb
    assert n >= PEER_LOOKAHEAD
    tok_spec = pl.BlockSpec((tb, d), lambda i: (i, 0))
    return pl.pallas_call(
        _peer_expert_body,
        grid=(n,),
        in_specs=[pl.BlockSpec((PEER_LOOKAHEAD * tb * PEER_SEL,), lambda i: (0,), memory_space=pltpu.SMEM),
                  pl.BlockSpec((tb * PEER_SEL,), lambda i: (jnp.minimum(i + PEER_LOOKAHEAD, n - 1),),
                               memory_space=pltpu.SMEM),
                  tok_spec,
                  pl.BlockSpec((tb, PEER_SEL), lambda i: (i, 0)),
                  tok_spec,
                  pl.BlockSpec(memory_space=pl.ANY)],
        out_specs=tok_spec,
        out_shape=jax.ShapeDtypeStruct((t, d), F32),
        scratch_shapes=[pltpu.VMEM((PEER_SLOTS, tb * PEER_SEL * ROW_PITCH, LANES), jnp.uint32),
                        pltpu.SemaphoreType.DMA((PEER_SLOTS,))],
        compiler_params=_cparams(("arbitrary",)),
        name="peer_expert",
    )(idx_flat, idx_flat, hn, gate_tok, h, uv_tab)


def peer_expert_route_fused(idx_flat, hn, gate_tok, h_full, uv_tab, nw, wq_bf16, subkeys_bf16, part):
    tp, d = hn.shape
    tb = PEER_TOK_BLOCK
    n = tp // tb
    tiles = tp // LANES
    assert tp % LANES == 0 and tiles * PEER_HEADS == n and n >= PEER_LOOKAHEAD
    hq = 2 * PEER_HALF
    e_off = part * n
    r_off = (part + 1) * tiles
    tok_spec = pl.BlockSpec((tb, d), lambda i: (i, 0))
    tile_of = lambda i: i // PEER_HEADS
    head_of = lambda i: lax.rem(i, PEER_HEADS)
    return pl.pallas_call(
        _peer_fused_body,
        grid=(n,),
        in_specs=[pl.BlockSpec((PEER_LOOKAHEAD * tb * PEER_SEL,), lambda i: (0,), memory_space=pltpu.SMEM),
                  pl.BlockSpec((tb * PEER_SEL,), lambda i: (jnp.minimum(i + PEER_LOOKAHEAD, n - 1),),
                               memory_space=pltpu.SMEM),
                  tok_spec,
                  pl.BlockSpec((tb, PEER_SEL), lambda i: (i, 0)),
                  pl.BlockSpec((tb, d), lambda i: (e_off + i, 0)),
                  pl.BlockSpec(memory_space=pl.ANY),
                  pl.BlockSpec((LANES, d), lambda i: (r_off + tile_of(i), 0)),
                  pl.BlockSpec((1, d), lambda i: (0, 0)),
                  pl.BlockSpec((d, hq), lambda i: (0, head_of(i))),
                  pl.BlockSpec((1, 2, PEER_N_KEYS, PEER_HALF), lambda i: (head_of(i), 0, 0, 0))],
        out_specs=[tok_spec,
                   pl.BlockSpec((LANES, d), lambda i: (tile_of(i), 0)),
                   pl.BlockSpec((LANES, PEER_SEL), lambda i: (tile_of(i), 0)),
                   pl.BlockSpec((LANES, PEER_SEL), lambda i: (tile_of(i), 0))],
        out_shape=[jax.ShapeDtypeStruct((tp, d), F32),
                   jax.ShapeDtypeStruct((tp, d), F32),
                   jax.ShapeDtypeStruct((tp, PEER_SEL), jnp.int32),
                   jax.ShapeDtypeStruct((tp, PEER_SEL), F32)],
        scratch_shapes=[pltpu.VMEM((PEER_SLOTS, tb * PEER_SEL * ROW_PITCH, LANES), jnp.uint32),
                        pltpu.SemaphoreType.DMA((PEER_SLOTS,)),
                        pltpu.VMEM((LANES, d), BF16),
                        pltpu.VMEM((2, PEER_N_KEYS, LANES), F32),
                        pltpu.VMEM((1, PEER_SEL, LANES), jnp.int32),
                        pltpu.VMEM((1, PEER_SEL, LANES), F32)],
        compiler_params=_cparams(("arbitrary",)),
        name="peer_expert_route",
    )(idx_flat, idx_flat, hn, gate_tok, h_full, uv_tab, h_full, nw.reshape(1, d), wq_bf16, subkeys_bf16)


def _rmsnorm_body(x_ref, w_ref, o_ref):
    o_ref[...] = _rms(x_ref[...], w_ref[...])


def rmsnorm(x, w, tm=1024):
    t, d = x.shape
    tm = min(tm, t)
    return pl.pallas_call(
        _rmsnorm_body,
        grid=(t // tm,),
        in_specs=[pl.BlockSpec((tm, d), lambda i: (i, 0)), pl.BlockSpec((1, d), lambda i: (0, 0))],
        out_specs=pl.BlockSpec((tm, d), lambda i: (i, 0)),
        out_shape=jax.ShapeDtypeStruct((t, d), F32),
        compiler_params=_cparams(("parallel",)),
        name="rmsnorm",
    )(x, w.reshape(1, d))


def _pad_cols(v, start, width=SMALL_W):
    out = jnp.zeros((1, width), F32)
    return out.at[0, start:start + v.shape[0]].set(v.astype(F32))


def _even_mix(h, bsz, norm_w, w_in, w_out, ssd_conv_w, ssd_conv_b, ssd_dt_bias, ssd_a_log, ssd_d_skip,
              ssd_norm, ml_conv_w, ml_conv_b, ml_wq, ml_wk, ml_wv, ml_i_bias, ml_f_bias, ml_norm):
    d = D_MODEL
    o1 = d
    o2 = o1 + SSD_XBC
    o3 = o2 + SSD_HEADS
    o4 = o3 + d
    o5 = o4 + d
    o6 = o5 + MLSTM_HEADS
    o7 = o6 + MLSTM_HEADS
    w_small = jnp.concatenate([w_in[:, o2:o3], w_in[:, o5:o6], w_in[:, o6:o7],
                               jnp.zeros((d, SMALL_W - (o3 - o2) - (o7 - o5)), w_in.dtype)], axis=1)
    w_all = jnp.concatenate([w_in[:, :o1], w_in[:, o1:o2], w_in[:, o3:o4], w_in[:, o4:o5], w_small],
                            axis=1).astype(BF16)
    z, xbc, xm, og, small = norm_proj(h, norm_w, w_all, (d, SSD_XBC, d, d, SMALL_W), tm=256)
    bias_pad = (_pad_cols(ssd_dt_bias, SMALL_DT) + _pad_cols(ml_i_bias, SMALL_IG)
                + _pad_cols(ml_f_bias, SMALL_FG))
    y_a = ssd_mixer(z, xbc, small, ssd_conv_w, ssd_conv_b, bias_pad, _pad_cols(ssd_a_log, SMALL_DT),
                    _pad_cols(ssd_d_skip, SMALL_DT), ssd_norm, bsz)
    y_b = mlstm_mixer(xm, og, small, ml_conv_w, ml_conv_b, bias_pad, ml_wq.astype(BF16),
                      ml_wk.astype(BF16), ml_wv.astype(BF16), ml_norm, bsz)
    w_out_b = w_out.astype(BF16)
    return proj_residual([y_a, y_b], [w_out_b[:d], w_out_b[d:]], h)


def _odd_mix(h, bsz, norm_w, w_in, w_out, lb_logits, hg_norm):
    d = D_MODEL
    uq, uf, ui, ug = norm_proj(h, norm_w, w_in.astype(BF16), (d, d, d, d), tm=256)
    y = hgrn_mixer(uq, uf, ui, ug, lb_logits, hg_norm, bsz)
    return proj_residual([y], [w_out.astype(BF16)], h)


def _xattn(h, mem2d, bsz, norm_x, norm_m, wq, wkv, wo):
    k, v = norm_proj(mem2d, norm_m, wkv.astype(BF16), (D_MODEL, D_MODEL), out_dtype=BF16)
    return xattn_residual(h, norm_x, wq.astype(BF16), k, v, wo.astype(BF16), bsz)


def _peer(h, norm_w, wq, subkeys, u_tab, v_tab):
    t = h.shape[0]
    n_exp = u_tab.shape[0]
    wq_b = wq.astype(BF16)
    sk_b = subkeys.astype(BF16)
    u_bits = lax.bitcast_convert_type(u_tab.astype(BF16), jnp.uint16).astype(jnp.uint32)
    v_bits = lax.bitcast_convert_type(v_tab.astype(BF16), jnp.uint16).astype(jnp.uint32)
    uv_tab = ((u_bits << 16) | v_bits).reshape(n_exp, ROW_CHUNKS, LANES)
    parts = PEER_PARTS
    while t % (parts * LANES) != 0:
        parts //= 2
    tp = t // parts
    hn, idx, gate = peer_route(h[:tp], norm_w, wq_b, sk_b)
    outs = []
    for k in range(parts - 1):
        o, hn, idx, gate = peer_expert_route_fused(idx.reshape(tp * PEER_SEL), hn, gate, h, uv_tab,
                                                   norm_w, wq_b, sk_b, k)
        outs.append(o)
    outs.append(peer_expert_residual(idx.reshape(tp * PEER_SEL), hn, gate, h[(parts - 1) * tp:], uv_tab))
    return jnp.concatenate(outs, axis=0)


def kernel(x, mem, norm_mix, norm_xattn, norm_mem, norm_ffn, norm_final, ev_w_in, ev_w_out, ssd_conv_w, ssd_conv_b, ssd_dt_bias, ssd_a_log, ssd_d_skip, ssd_norm, ml_conv_w, ml_conv_b, ml_wq, ml_wk, ml_wv, ml_i_bias, ml_f_bias, ml_norm, od_w_in, od_w_out, hgrn_lb_logits, hgrn_norm, xa_wq, xa_wkv, xa_wo, peer_wq, peer_subkeys, peer_u, peer_v):
    bsz, seq, d = x.shape
    depth = norm_mix.shape[0]
    assert depth == 2 and hgrn_lb_logits.shape[0] == 2
    h = x.reshape(bsz * seq, d)
    mem2d = mem.reshape(bsz * MEM_LEN, d)
    for layer in range(depth):
        if layer % 2 == 0:
            e = layer // 2
            h = _even_mix(h, bsz, norm_mix[layer], ev_w_in[e], ev_w_out[e], ssd_conv_w[e], ssd_conv_b[e],
                          ssd_dt_bias[e], ssd_a_log[e], ssd_d_skip[e], ssd_norm[e], ml_conv_w[e],
                          ml_conv_b[e], ml_wq[e], ml_wk[e], ml_wv[e], ml_i_bias[e], ml_f_bias[e], ml_norm[e])
        else:
            o = layer // 2
            h = _odd_mix(h, bsz, norm_mix[layer], od_w_in[o], od_w_out[o], hgrn_lb_logits, hgrn_norm[o])
        h = _xattn(h, mem2d, bsz, norm_xattn[layer], norm_mem[layer], xa_wq[layer], xa_wkv[layer], xa_wo[layer])
        h = _peer(h, norm_ffn[layer], peer_wq[layer], peer_subkeys[layer], peer_u[layer], peer_v[layer])
    return rmsnorm(h, norm_final).reshape(bsz, seq, d)
```

```python
import functools

import jax
import jax.numpy as jnp
from jax import lax
from jax.experimental import pallas as pl
from jax.experimental.pallas import tpu as pltpu

F32 = jnp.float32
BF16 = jnp.bfloat16
HIGHEST = lax.Precision.HIGHEST

D_MODEL = 1024
EPS = 1e-6
CONV_WIDTH = 4
MEM_LEN = 256
SSD_HEAD_DIM = 64
SSD_HEADS = 16
SSD_GROUPS = 2
SSD_STATE = 128
SSD_XBC = D_MODEL + 2 * SSD_GROUPS * SSD_STATE
SSD_CHUNK = 128
MLSTM_HEADS = 4
MLSTM_HEAD_DIM = 256
MLSTM_CHUNK = 128
HGRN_HEADS = 8
HGRN_HEAD_DIM = 128
HGRN_CHUNK = 64
XATTN_HEADS = 4
XATTN_HEAD_DIM = 256
PEER_HEADS = 8
PEER_N_KEYS = 128
PEER_TOPK = 16
PEER_HALF = 128
PEER_SEL = PEER_HEADS * PEER_TOPK

SUBLANES = 8
LANES = 128
SMALL_W = LANES
SMALL_DT = 0
SMALL_IG = SSD_HEADS
SMALL_FG = SSD_HEADS + MLSTM_HEADS

VMEM_LIMIT = 52 * 1024 * 1024


def _cparams(sem):
    return pltpu.CompilerParams(dimension_semantics=sem, vmem_limit_bytes=VMEM_LIMIT)


def _sigmoid(x):
    return 1.0 / (1.0 + jnp.exp(-x))


def _softplus(x):
    return jnp.maximum(x, 0.0) + jnp.log(1.0 + jnp.exp(-jnp.abs(x)))


def _rms(x, w):
    return x * lax.rsqrt(jnp.mean(x * x, axis=-1, keepdims=True) + EPS) * w


def _dot(a, b):
    return jnp.dot(a.astype(BF16), b.astype(BF16), preferred_element_type=F32)


def _dot_nt(a, b):
    return lax.dot_general(a.astype(BF16), b.astype(BF16), (((1,), (1,)), ((), ())),
                           preferred_element_type=F32)


def _cumsum_rows(x):
    n = x.shape[0]
    r = lax.broadcasted_iota(jnp.int32, (n, n), 0)
    c = lax.broadcasted_iota(jnp.int32, (n, n), 1)
    tril = (r >= c).astype(F32)
    return jnp.dot(tril, x, precision=HIGHEST, preferred_element_type=F32)


def _causal(n):
    r = lax.broadcasted_iota(jnp.int32, (n, n), 0)
    c = lax.broadcasted_iota(jnp.int32, (n, n), 1)
    return r >= c


def _norm_proj_body(x_ref, nw_ref, w_ref, *out_refs, splits):
    xn = _rms(x_ref[...], nw_ref[...]).astype(BF16)
    off = 0
    for o_ref, n in zip(out_refs, splits):
        o_ref[...] = jnp.dot(xn, w_ref[:, off:off + n], preferred_element_type=F32).astype(o_ref.dtype)
        off += n


def norm_proj(x, nw, w_bf16, splits, tm=512, out_dtype=F32):
    t, d = x.shape
    n = w_bf16.shape[1]
    assert sum(splits) == n
    tm = min(tm, t)
    assert t % tm == 0
    return pl.pallas_call(
        functools.partial(_norm_proj_body, splits=tuple(splits)),
        grid=(t // tm,),
        in_specs=[pl.BlockSpec((tm, d), lambda i: (i, 0)),
                  pl.BlockSpec((1, d), lambda i: (0, 0)),
                  pl.BlockSpec((d, n), lambda i: (0, 0))],
        out_specs=[pl.BlockSpec((tm, s), lambda i: (i, 0)) for s in splits],
        out_shape=[jax.ShapeDtypeStruct((t, s), out_dtype) for s in splits],
        compiler_params=_cparams(("parallel",)),
        name="norm_proj",
    )(x, nw.reshape(1, d), w_bf16)


def _proj_residual_body(*refs, n_parts):
    a_refs = refs[:n_parts]
    w_refs = refs[n_parts:2 * n_parts]
    r_ref = refs[2 * n_parts]
    o_ref = refs[2 * n_parts + 1]
    acc = r_ref[...]
    for a_ref, w_ref in zip(a_refs, w_refs):
        acc = acc + jnp.dot(a_ref[...].astype(BF16), w_ref[...], preferred_element_type=F32)
    o_ref[...] = acc


def proj_residual(parts, ws_bf16, res, tm=512):
    t, d = res.shape
    tm = min(tm, t)
    n_parts = len(parts)
    in_specs = ([pl.BlockSpec((tm, p.shape[1]), lambda i: (i, 0)) for p in parts]
                + [pl.BlockSpec(w.shape, lambda i: (0, 0)) for w in ws_bf16]
                + [pl.BlockSpec((tm, d), lambda i: (i, 0))])
    return pl.pallas_call(
        functools.partial(_proj_residual_body, n_parts=n_parts),
        grid=(t // tm,),
        in_specs=in_specs,
        out_specs=pl.BlockSpec((tm, d), lambda i: (i, 0)),
        out_shape=jax.ShapeDtypeStruct((t, d), F32),
        compiler_params=_cparams(("parallel",)),
        name="proj_residual",
    )(*parts, *ws_bf16, res)


def _conv_silu(x_in, xpad_ref, cw_ref, cb_ref, n_rows):
    xpad_ref[SUBLANES:SUBLANES + n_rows, :] = x_in
    acc = cb_ref[...] + cw_ref[CONV_WIDTH - 1:CONV_WIDTH, :] * x_in
    for k in range(CONV_WIDTH - 1):
        start = SUBLANES - (CONV_WIDTH - 1) + k
        acc = acc + cw_ref[k:k + 1, :] * xpad_ref[start:start + n_rows, :]
    xpad_ref[0:SUBLANES, :] = x_in[n_rows - SUBLANES:n_rows, :]
    return acc * _sigmoid(acc)


def _ssd_body(z_ref, xbc_ref, sm_ref, cw_ref, cb_ref, bias_ref, alog_ref, dsk_ref, nw_ref,
              o_ref, xpad_ref, state_ref, ybuf_ref):
    L, P, N = SSD_CHUNK, SSD_HEAD_DIM, SSD_STATE
    heads_per_group = SSD_HEADS // SSD_GROUPS

    @pl.when(pl.program_id(1) == 0)
    def _init():
        xpad_ref[0:SUBLANES, :] = jnp.zeros((SUBLANES, SSD_XBC), F32)
        state_ref[...] = jnp.zeros_like(state_ref)

    xa = _conv_silu(xbc_ref[...], xpad_ref, cw_ref, cb_ref, L)
    dt = _softplus(sm_ref[...] + bias_ref[...])
    a = -jnp.exp(alog_ref[...])
    cs_col = _cumsum_rows(dt * a)
    cs_row = cs_col.T
    causal = _causal(L)
    first_head = lax.broadcasted_iota(jnp.int32, (L, 2 * P), 1) < P

    for g in range(SSD_GROUPS):
        b_g = xa[:, D_MODEL + g * N:D_MODEL + (g + 1) * N]
        c_g = xa[:, D_MODEL + (SSD_GROUPS + g) * N:D_MODEL + (SSD_GROUPS + g + 1) * N]
        cb = _dot_nt(c_g, b_g)
        b_gt = b_g.T.astype(BF16)
        c_gb = c_g.astype(BF16)
        for e2 in range(heads_per_group // 2):
            pair = (g * heads_per_group) // 2 + e2
            col0 = SMALL_DT + 2 * pair
            col1 = col0 + 1

            def per_lane(arr, r0=None):
                a0 = arr[:, col0:col0 + 1] if r0 is None else arr[r0:r0 + 1, col0:col0 + 1]
                a1 = arr[:, col1:col1 + 1] if r0 is None else arr[r0:r0 + 1, col1:col1 + 1]
                return jnp.where(first_head[0:a0.shape[0], :], a0, a1)

            cs_c = per_lane(cs_col)
            cs_last = per_lane(cs_col, L - 1)
            x_p = xa[:, pair * 2 * P:(pair + 1) * 2 * P]
            x_dt = x_p * per_lane(dt)
            x_dtb = x_dt.astype(BF16)
            ys = []
            for col in (col0, col1):
                decay = jnp.exp(jnp.where(causal, cs_col[:, col:col + 1] - cs_row[col:col + 1, :], -jnp.inf))
                ys.append(jnp.dot((cb * decay).astype(BF16), x_dtb, preferred_element_type=F32))
            y = jnp.where(first_head, ys[0], ys[1])
            s_prev = state_ref[pair]
            y = y + jnp.dot(c_gb, s_prev.astype(BF16), preferred_element_type=F32) * jnp.exp(cs_c)
            y = y + x_p * per_lane(dsk_ref[...])
            ybuf_ref[:, pair * 2 * P:(pair + 1) * 2 * P] = y
            x_end = (x_dt * jnp.exp(cs_last - cs_c)).astype(BF16)
            state_ref[pair] = jnp.exp(cs_last) * s_prev + jnp.dot(b_gt, x_end, preferred_element_type=F32)

    zz = z_ref[...]
    y = ybuf_ref[...] * (zz * _sigmoid(zz))
    gw = D_MODEL // SSD_GROUPS
    for g in range(SSD_GROUPS):
        o_ref[:, g * gw:(g + 1) * gw] = _rms(y[:, g * gw:(g + 1) * gw], nw_ref[:, g * gw:(g + 1) * gw])


def ssd_mixer(z, xbc, small, conv_w, conv_b, bias_pad, alog_pad, dskip_pad, norm_w, bsz):
    t = z.shape[0]
    L = SSD_CHUNK
    nc = t // bsz // L
    row = lambda b, c: (b * nc + c, 0)
    const = lambda b, c: (0, 0)
    return pl.pallas_call(
        _ssd_body,
        grid=(bsz, nc),
        in_specs=[pl.BlockSpec((L, D_MODEL), row),
                  pl.BlockSpec((L, SSD_XBC), row),
                  pl.BlockSpec((L, SMALL_W), row),
                  pl.BlockSpec((CONV_WIDTH, SSD_XBC), const),
                  pl.BlockSpec((1, SSD_XBC), const),
                  pl.BlockSpec((1, SMALL_W), const),
                  pl.BlockSpec((1, SMALL_W), const),
                  pl.BlockSpec((1, SMALL_W), const),
                  pl.BlockSpec((1, D_MODEL), const)],
        out_specs=pl.BlockSpec((L, D_MODEL), row),
        out_shape=jax.ShapeDtypeStruct((t, D_MODEL), F32),
        scratch_shapes=[pltpu.VMEM((L + SUBLANES, SSD_XBC), F32),
                        pltpu.VMEM((SSD_HEADS // 2, SSD_STATE, 2 * SSD_HEAD_DIM), F32),
                        pltpu.VMEM((L, D_MODEL), F32)],
        compiler_params=_cparams(("parallel", "arbitrary")),
        name="ssd_mixer",
    )(z, xbc, small, conv_w, conv_b.reshape(1, -1), bias_pad, alog_pad, dskip_pad, norm_w.reshape(1, -1))


def _mlstm_body(xm_ref, og_ref, sm_ref, cw_ref, cb_ref, bias_ref, wq_ref, wk_ref, wv_ref, nw_ref,
                o_ref, xpad_ref, c_ref, n_ref, m_ref):
    L, H, Dh = MLSTM_CHUNK, MLSTM_HEADS, MLSTM_HEAD_DIM

    @pl.when(pl.program_id(1) == 0)
    def _init():
        xpad_ref[0:SUBLANES, :] = jnp.zeros((SUBLANES, D_MODEL), F32)
        c_ref[...] = jnp.zeros_like(c_ref)
        n_ref[...] = jnp.zeros_like(n_ref)
        m_ref[...] = jnp.zeros_like(m_ref)

    xm = xm_ref[...]
    xc = _conv_silu(xm, xpad_ref, cw_ref, cb_ref, L)
    smb = sm_ref[...] + bias_ref[...]
    logf = jnp.minimum(smb, 0.0) - jnp.log(1.0 + jnp.exp(-jnp.abs(smb)))
    bcum_col = _cumsum_rows(logf)
    bcum_row = bcum_col.T
    smb_row = smb.T
    causal = _causal(L)

    for h in range(H):
        ci, cf = SMALL_IG + h, SMALL_FG + h
        bc = bcum_col[:, cf:cf + 1]
        br = bcum_row[cf:cf + 1, :]
        ic = smb[:, ci:ci + 1]
        ir = smb_row[ci:ci + 1, :]
        b_end = bcum_col[L - 1:L, cf:cf + 1]
        dlog = jnp.where(causal, bc - br + ir, -jnp.inf)
        m_intra = jnp.max(dlog, axis=1, keepdims=True)
        g_end = b_end - bc + ic
        m_chunk = jnp.max(g_end, axis=0, keepdims=True)
        w_end = jnp.exp(g_end - m_chunk)

        sl = slice(h * Dh, (h + 1) * Dh)
        q = _dot(xc[:, sl], wq_ref[h])
        k = _dot(xc[:, sl], wk_ref[h]) * (Dh ** -0.5)
        v = _dot(xm[:, sl], wv_ref[h])
        qb, kb, vb = q.astype(BF16), k.astype(BF16), v.astype(BF16)

        m_prev = m_ref[h:h + 1, 0:1]
        c_prev = c_ref[h]
        n_prev = n_ref[h:h + 1, :]
        m_t = jnp.maximum(bc + m_prev, m_intra)
        a_t = jnp.exp(bc + m_prev - m_t)
        qk = _dot_nt(qb, kb) * jnp.exp(dlog - m_t)
        num = a_t * jnp.dot(qb, c_prev.astype(BF16), preferred_element_type=F32) \
            + jnp.dot(qk.astype(BF16), vb, preferred_element_type=F32)
        den = a_t * jnp.sum(q * n_prev, axis=1, keepdims=True) + jnp.sum(qk, axis=1, keepdims=True)
        hid = num / jnp.maximum(jnp.abs(den), jnp.exp(-m_t))
        hid = _rms(hid, nw_ref[:, sl])
        og = og_ref[:, sl]
        o_ref[:, sl] = hid * _sigmoid(og)

        m_new = jnp.maximum(b_end + m_prev, m_chunk)
        a_old = jnp.exp(b_end + m_prev - m_new)
        a_new = jnp.exp(m_chunk - m_new)
        kw = k * w_end
        c_local = jnp.dot(kw.T.astype(BF16), vb, preferred_element_type=F32)
        c_ref[h] = a_old * c_prev + a_new * c_local
        n_ref[h:h + 1, :] = a_old * n_prev + a_new * jnp.sum(kw, axis=0, keepdims=True)
        m_ref[h:h + 1, :] = jnp.broadcast_to(m_new, (1, LANES))


def mlstm_mixer(xm, og, small, conv_w, conv_b, bias_pad, wq, wk, wv, norm_w, bsz):
    t = xm.shape[0]
    L = MLSTM_CHUNK
    nc = t // bsz // L
    row = lambda b, c: (b * nc + c, 0)
    const = lambda b, c: (0, 0)
    const3 = lambda b, c: (0, 0, 0)
    wspec = pl.BlockSpec((MLSTM_HEADS, MLSTM_HEAD_DIM, MLSTM_HEAD_DIM), const3)
    return pl.pallas_call(
        _mlstm_body,
        grid=(bsz, nc),
        in_specs=[pl.BlockSpec((L, D_MODEL), row),
                  pl.BlockSpec((L, D_MODEL), row),
                  pl.BlockSpec((L, SMALL_W), row),
                  pl.BlockSpec((CONV_WIDTH, D_MODEL), const),
                  pl.BlockSpec((1, D_MODEL), const),
                  pl.BlockSpec((1, SMALL_W), const),
                  wspec, wspec, wspec,
                  pl.BlockSpec((1, D_MODEL), const)],
        out_specs=pl.BlockSpec((L, D_MODEL), row),
        out_shape=jax.ShapeDtypeStruct((t, D_MODEL), F32),
        scratch_shapes=[pltpu.VMEM((L + SUBLANES, D_MODEL), F32),
                        pltpu.VMEM((MLSTM_HEADS, MLSTM_HEAD_DIM, MLSTM_HEAD_DIM), F32),
                        pltpu.VMEM((SUBLANES, MLSTM_HEAD_DIM), F32),
                        pltpu.VMEM((SUBLANES, LANES), F32)],
        compiler_params=_cparams(("parallel", "arbitrary")),
        name="mlstm_mixer",
    )(xm, og, small, conv_w, conv_b.reshape(1, -1), bias_pad, wq, wk, wv, norm_w.reshape(1, -1))


def _hgrn_body(uq_ref, uf_ref, ui_ref, ug_ref, lbl_ref, nw_ref, o_ref, s_ref):
    L, H, Dk = HGRN_CHUNK, HGRN_HEADS, HGRN_HEAD_DIM

    @pl.when(pl.program_id(1) == 0)
    def _init():
        s_ref[...] = jnp.zeros_like(s_ref)

    lg = lbl_ref[...]
    lmax = jnp.max(lg, axis=0, keepdims=True)
    le = jnp.exp(lg - lmax)
    lb = le[0:1, :] / jnp.sum(le, axis=0, keepdims=True)

    uf = uf_ref[...]
    uq = uq_ref[...]
    logf = jnp.log(lb + (1.0 - lb) * _sigmoid(uf))
    k = (1.0 - lb) * _sigmoid(-uf)
    q = uq * _sigmoid(uq)
    v = ui_ref[...]
    gc = _cumsum_rows(logf)
    g_ref = gc[L // 2:L // 2 + 1, :]
    g_last = gc[L - 1:L, :]
    q_in = (q * jnp.exp(gc - g_ref)).astype(BF16)
    k_in = (k * jnp.exp(g_ref - gc)).astype(BF16)
    q_off = (q * jnp.exp(gc)).astype(BF16)
    k_end = k * jnp.exp(g_last - gc)
    decay = jnp.exp(g_last)
    causal = _causal(L)
    vb = v.astype(BF16)
    ug = ug_ref[...]

    for h in range(H):
        sl = slice(h * Dk, (h + 1) * Dk)
        att = jnp.where(causal, _dot_nt(q_in[:, sl], k_in[:, sl]), 0.0)
        st_prev = s_ref[h]
        o = jnp.dot(att.astype(BF16), vb[:, sl], preferred_element_type=F32) \
            + _dot_nt(q_off[:, sl], st_prev)
        s_ref[h] = decay[:, sl] * st_prev + _dot(v[:, sl].T, k_end[:, sl])
        g = ug[:, sl]
        o_ref[:, sl] = _rms(o, nw_ref[:, sl]) * (g * _sigmoid(g))


def hgrn_mixer(uq, uf, ui, ug, lb_logits, norm_w, bsz):
    t = uq.shape[0]
    L = HGRN_CHUNK
    nc = t // bsz // L
    row = lambda b, c: (b * nc + c, 0)
    const = lambda b, c: (0, 0)
    rs = pl.BlockSpec((L, D_MODEL), row)
    return pl.pallas_call(
        _hgrn_body,
        grid=(bsz, nc),
        in_specs=[rs, rs, rs, rs,
                  pl.BlockSpec((2, D_MODEL), const),
                  pl.BlockSpec((1, D_MODEL), const)],
        out_specs=rs,
        out_shape=jax.ShapeDtypeStruct((t, D_MODEL), F32),
        scratch_shapes=[pltpu.VMEM((HGRN_HEADS, HGRN_HEAD_DIM, HGRN_HEAD_DIM), F32)],
        compiler_params=_cparams(("parallel", "arbitrary")),
        name="hgrn_mixer",
    )(uq, uf, ui, ug, lb_logits, norm_w.reshape(1, -1))


def _xattn_body(h_ref, nw_ref, wq_ref, k_ref, v_ref, wo_ref, o_ref, obuf_ref):
    hv = h_ref[...]
    hn = _rms(hv, nw_ref[...]).astype(BF16)
    q = jnp.dot(hn, wq_ref[...], preferred_element_type=F32)
    scale = XATTN_HEAD_DIM ** -0.5
    for a in range(XATTN_HEADS):
        sl = slice(a * XATTN_HEAD_DIM, (a + 1) * XATTN_HEAD_DIM)
        s = _dot_nt(q[:, sl], k_ref[:, sl]) * scale
        s = s - jnp.max(s, axis=1, keepdims=True)
        p = jnp.exp(s)
        p = p / jnp.sum(p, axis=1, keepdims=True)
        obuf_ref[:, sl] = jnp.dot(p.astype(BF16), v_ref[:, sl], preferred_element_type=F32).astype(BF16)
    o_ref[...] = hv + jnp.dot(obuf_ref[...], wo_ref[...], preferred_element_type=F32)


def xattn_residual(h, nw, wq_bf16, k_bf16, v_bf16, wo_bf16, bsz, tm=512):
    t, d = h.shape
    seq = t // bsz
    tm = min(tm, seq)
    nt = seq // tm
    row = lambda b, i: (b * nt + i, 0)
    const = lambda b, i: (0, 0)
    mem = lambda b, i: (b, 0)
    return pl.pallas_call(
        _xattn_body,
        grid=(bsz, nt),
        in_specs=[pl.BlockSpec((tm, d), row),
                  pl.BlockSpec((1, d), const),
                  pl.BlockSpec((d, d), const),
                  pl.BlockSpec((MEM_LEN, d), mem),
                  pl.BlockSpec((MEM_LEN, d), mem),
                  pl.BlockSpec((d, d), const)],
        out_specs=pl.BlockSpec((tm, d), row),
        out_shape=jax.ShapeDtypeStruct((t, d), F32),
        scratch_shapes=[pltpu.VMEM((tm, d), BF16)],
        compiler_params=_cparams(("parallel", "parallel")),
        name="xattn_residual",
    )(h, nw.reshape(1, d), wq_bf16, k_bf16, v_bf16, wo_bf16)


def _topk_rows_steps(arrays, k):
    arrays = list(arrays)
    iotas = [lax.broadcasted_iota(jnp.int32, s.shape, 0) for s in arrays]
    vals = [[] for _ in arrays]
    idxs = [[] for _ in arrays]
    for _ in range(k):
        for n, s in enumerate(arrays):
            m = jnp.max(s, axis=0, keepdims=True)
            i = jnp.min(jnp.where(s == m, iotas[n], s.shape[0]), axis=0, keepdims=True)
            vals[n].append(m)
            idxs[n].append(i)
            arrays[n] = jnp.where(iotas[n] == i, -jnp.inf, s)
        yield
    return [(jnp.concatenate(v, axis=0), jnp.concatenate(i, axis=0)) for v, i in zip(vals, idxs)]


def _run(steps):
    try:
        while True:
            next(steps)
    except StopIteration as done:
        return done.value


def _select_rows(table, sel, n):
    out = jnp.zeros(sel.shape, jnp.int32)
    for a in range(n):
        out = jnp.where(sel == a, table[a:a + 1, :], out)
    return out


def _route_norm(head, h_ref, nw_ref, hn_ref, hnb_ref):
    @pl.when(head == 0)
    def _norm():
        xn = _rms(h_ref[...], nw_ref[...])
        hn_ref[...] = xn
        hnb_ref[...] = xn.astype(BF16)


def _route_steps(head, wq_ref, sk_ref, hnb_ref, st_ref, oi_ref, og_ref, n_col_blocks):
    K = PEER_TOPK
    q = jnp.dot(hnb_ref[...], wq_ref[...], preferred_element_type=F32)
    for p in range(2):
        qp = q[:, p * PEER_HALF:(p + 1) * PEER_HALF]
        st = _dot_nt(sk_ref[0, p], qp)
        for cb in range(n_col_blocks):
            st_ref[p * n_col_blocks + cb] = st[:, cb * LANES:(cb + 1) * LANES]
    yield
    tops = yield from _topk_rows_steps([st_ref[j] for j in range(2 * n_col_blocks)], K)

    sub = lax.broadcasted_iota(jnp.int32, (SUBLANES, LANES), 0)
    cands = []
    for cb in range(n_col_blocks):
        v1, v2 = tops[cb][0], tops[n_col_blocks + cb][0]
        blocks = [v1[0:1, :] + v2[0:SUBLANES, :], v1[0:1, :] + v2[SUBLANES:K, :]]
        for a in range(1, SUBLANES):
            blocks.append(jnp.where(sub < K // (a + 1), v1[a:a + 1, :] + v2[0:SUBLANES, :], -jnp.inf))
        blocks.append(v1[SUBLANES:K, :] + v2[0:1, :])
        cands.append(jnp.concatenate(blocks, axis=0))
    yield
    cand_tops = yield from _topk_rows_steps(cands, K)

    row0 = pl.multiple_of(head * K, K)
    for cb in range(n_col_blocks):
        i1, i2 = tops[cb][1], tops[n_col_blocks + cb][1]
        top_s, pos = cand_tops[cb]
        blk = lax.shift_right_logical(pos, 3)
        within = lax.bitwise_and(pos, SUBLANES - 1)
        a_sel = jnp.where(blk <= 1, 0, jnp.where(blk <= SUBLANES, blk - 1, within + SUBLANES))
        b_sel = jnp.where(blk == 1, within + SUBLANES, jnp.where(blk <= SUBLANES, within, 0))
        e1 = _select_rows(i1, a_sel, K)
        e2 = _select_rows(i2, b_sel, K)
        oi_ref[cb, pl.ds(row0, K), :] = e1 * PEER_N_KEYS + e2
        ex = jnp.exp(top_s - jnp.max(top_s, axis=0, keepdims=True))
        og_ref[cb, pl.ds(row0, K), :] = ex / jnp.sum(ex, axis=0, keepdims=True)


def _route_emit(head, idx_ref, gate_ref, oi_ref, og_ref, n_col_blocks):
    @pl.when(head == PEER_HEADS - 1)
    def _emit():
        for cb in range(n_col_blocks):
            idx_ref[cb * LANES:(cb + 1) * LANES, :] = oi_ref[cb].T
            gate_ref[cb * LANES:(cb + 1) * LANES, :] = og_ref[cb].T


def _peer_route_body(h_ref, nw_ref, wq_ref, sk_ref, hn_ref, idx_ref, gate_ref,
                     hnb_ref, st_ref, oi_ref, og_ref, *, n_col_blocks):
    head = pl.program_id(1)
    _route_norm(head, h_ref, nw_ref, hn_ref, hnb_ref)
    _run(_route_steps(head, wq_ref, sk_ref, hnb_ref, st_ref, oi_ref, og_ref, n_col_blocks))
    _route_emit(head, idx_ref, gate_ref, oi_ref, og_ref, n_col_blocks)


def peer_route(h, nw, wq_bf16, subkeys_bf16, tm=512):
    t, d = h.shape
    tm = min(tm, t)
    ncb = tm // LANES
    hq = 2 * PEER_HALF
    return pl.pallas_call(
        functools.partial(_peer_route_body, n_col_blocks=ncb),
        grid=(t // tm, PEER_HEADS),
        in_specs=[pl.BlockSpec((tm, d), lambda i, hd: (i, 0)),
                  pl.BlockSpec((1, d), lambda i, hd: (0, 0)),
                  pl.BlockSpec((d, hq), lambda i, hd: (0, hd)),
                  pl.BlockSpec((1, 2, PEER_N_KEYS, PEER_HALF), lambda i, hd: (hd, 0, 0, 0))],
        out_specs=[pl.BlockSpec((tm, d), lambda i, hd: (i, 0)),
                   pl.BlockSpec((tm, PEER_SEL), lambda i, hd: (i, 0)),
                   pl.BlockSpec((tm, PEER_SEL), lambda i, hd: (i, 0))],
        out_shape=[jax.ShapeDtypeStruct((t, d), F32),
                   jax.ShapeDtypeStruct((t, PEER_SEL), jnp.int32),
                   jax.ShapeDtypeStruct((t, PEER_SEL), F32)],
        scratch_shapes=[pltpu.VMEM((tm, d), BF16),
                        pltpu.VMEM((2 * ncb, PEER_N_KEYS, LANES), F32),
                        pltpu.VMEM((ncb, PEER_SEL, LANES), jnp.int32),
                        pltpu.VMEM((ncb, PEER_SEL, LANES), F32)],
        compiler_params=_cparams(("parallel", "arbitrary")),
        name="peer_route",
    )(h, nw.reshape(1, d), wq_bf16, subkeys_bf16)


PEER_TOK_BLOCK = 16
PEER_PARTS = 8
PEER_SLOTS = 3
FUSED_ROUTE_EVERY = 7
PEER_LOOKAHEAD = PEER_SLOTS - 1
PEER_ISSUE_UNROLL = 16
ROW_CHUNKS = D_MODEL // LANES
UV_CHUNKS = ROW_CHUNKS
PIECES = 2 * ROW_CHUNKS
ROW_PITCH = UV_CHUNKS + 1
U_MASK = 0xFFFF0000


def _pack_tables_body(u_ref, v_ref, o_ref):
    rows = u_ref.shape[0]
    u_hi = pltpu.bitcast(u_ref[...].astype(BF16).astype(F32), jnp.uint32)
    v_lo = lax.shift_right_logical(pltpu.bitcast(v_ref[...].astype(BF16).astype(F32), jnp.uint32), jnp.uint32(16))
    packed = lax.bitwise_or(u_hi, v_lo)
    for c in range(ROW_CHUNKS):
        o_ref[pl.ds(c, rows, stride=ROW_CHUNKS), :] = packed[:, c * LANES:(c + 1) * LANES]


def pack_expert_tables(u_tab, v_tab, rows=512):
    n_exp, d = u_tab.shape
    rows = min(rows, n_exp)
    assert n_exp % rows == 0 and d == D_MODEL
    packed = pl.pallas_call(
        _pack_tables_body,
        grid=(n_exp // rows,),
        in_specs=[pl.BlockSpec((rows, d), lambda i: (i, 0)), pl.BlockSpec((rows, d), lambda i: (i, 0))],
        out_specs=pl.BlockSpec((rows * ROW_CHUNKS, LANES), lambda i: (i, 0)),
        out_shape=jax.ShapeDtypeStruct((n_exp * ROW_CHUNKS, LANES), jnp.uint32),
        compiler_params=_cparams(("parallel",)),
        name="pack_expert_tables",
    )(u_tab, v_tab)
    return packed.reshape(n_exp, ROW_CHUNKS, LANES)


def _start_row_copy(uv_hbm, expert, buf, slot, row, sem, priority):
    pltpu.make_async_copy(uv_hbm.at[expert], buf.at[slot, pl.ds(row * ROW_PITCH, UV_CHUNKS), :],
                          sem.at[slot]).start(priority=priority)


def _wait_slot(buf, slot, sem):
    done = buf.at[slot, pl.ds(0, PEER_TOK_BLOCK * PEER_SEL * UV_CHUNKS), :]
    pltpu.make_async_copy(done, done, sem.at[slot]).wait()


def _expert_prime(i, idx_head_ref, uv_hbm, buf, sem):
    rows_per_block = PEER_TOK_BLOCK * PEER_SEL

    @pl.when(i == 0)
    def _prime():
        for b in range(PEER_LOOKAHEAD):
            def rows(ro, carry):
                for ri in range(PEER_ISSUE_UNROLL):
                    r = ro * PEER_ISSUE_UNROLL + ri
                    _start_row_copy(uv_hbm, idx_head_ref[b * rows_per_block + r], buf, b, r, sem, ri % 2)
                return carry

            lax.fori_loop(0, rows_per_block // PEER_ISSUE_UNROLL, rows, 0)


def _expert_drain(i, n_blocks, buf, sem):
    @pl.when(i == n_blocks - 1)
    def _drain():
        for k in range(1, PEER_SLOTS):
            _wait_slot(buf, lax.rem(i + k, PEER_SLOTS), sem)


def _expert_main(i, idx_ahead_ref, x_ref, gate_ref, h_ref, uv_hbm, o_ref, buf, sem,
                 side_steps=None, side_every=1):
    tb = PEER_TOK_BLOCK
    slot = lax.rem(i, PEER_SLOTS)
    ahead_slot = lax.rem(i + PEER_LOOKAHEAD, PEER_SLOTS)
    _wait_slot(buf, slot, sem)
    gate_t = gate_ref[...].T
    group = PEER_SEL // PIECES
    pieces_done = 0

    for tk in range(tb):
        base = tk * PEER_SEL * ROW_PITCH

        def issue(piece):
            nonlocal pieces_done
            for k in range(group):
                r = tk * PEER_SEL + piece * group + k
                _start_row_copy(uv_hbm, idx_ahead_ref[r], buf, ahead_slot, r, sem, k % 2)
            pieces_done += 1
            if side_steps is not None and pieces_done % side_every == 0:
                next(side_steps, None)

        acc = None
        for c in range(ROW_CHUNKS):
            issue(c)
            uv_c = buf[slot, pl.ds(base + c, PEER_SEL, stride=ROW_PITCH), :]
            u_c = pltpu.bitcast(lax.bitwise_and(uv_c, jnp.uint32(U_MASK)), F32)
            term = u_c * x_ref[tk:tk + 1, c * LANES:(c + 1) * LANES]
            acc = term if acc is None else acc + term
        hsel = jnp.sum(acc, axis=1, keepdims=True)
        act = 0.5 * hsel * (1.0 + lax.erf(hsel * (2.0 ** -0.5)))
        w = gate_t[:, tk:tk + 1] * act
        outs = []
        for c in range(ROW_CHUNKS):
            issue(ROW_CHUNKS + c)
            uv_c = buf[slot, pl.ds(base + c, PEER_SEL, stride=ROW_PITCH), :]
            v_c = pltpu.bitcast(lax.shift_left(uv_c, jnp.uint32(16)), F32)
            outs.append(jnp.sum(v_c * w, axis=0, keepdims=True))
        o_ref[tk:tk + 1, :] = h_ref[tk:tk + 1, :] + jnp.concatenate(outs, axis=1)

    if side_steps is not None:
        for _ in side_steps:
            pass


def _peer_expert_body(idx_head_ref, idx_ahead_ref, x_ref, gate_ref, h_ref, uv_hbm, o_ref, buf, sem):
    i = pl.program_id(0)
    _expert_prime(i, idx_head_ref, uv_hbm, buf, sem)
    _expert_main(i, idx_ahead_ref, x_ref, gate_ref, h_ref, uv_hbm, o_ref, buf, sem)
    _expert_drain(i, pl.num_programs(0), buf, sem)


def _peer_fused_body(idx_head_ref, idx_ahead_ref, x_ref, gate_ref, h_ref, uv_hbm,
                     rh_ref, nw_ref, wq_ref, sk_ref,
                     o_ref, hn_ref, ridx_ref, rgate_ref,
                     buf, sem, hnb_ref, st_ref, oi_ref, og_ref):
    i = pl.program_id(0)
    head = lax.rem(i, PEER_HEADS)
    _expert_prime(i, idx_head_ref, uv_hbm, buf, sem)
    _route_norm(head, rh_ref, nw_ref, hn_ref, hnb_ref)
    route = _route_steps(head, wq_ref, sk_ref, hnb_ref, st_ref, oi_ref, og_ref, 1)
    _expert_main(i, idx_ahead_ref, x_ref, gate_ref, h_ref, uv_hbm, o_ref, buf, sem,
                 side_steps=route, side_every=FUSED_ROUTE_EVERY)
    _route_emit(head, ridx_ref, rgate_ref, oi_ref, og_ref, 1)
    _expert_drain(i, pl.num_programs(0), buf, sem)


def peer_expert_residual(idx_flat, hn, gate_tok, h, uv_tab):
    t, d = h.shape
    tb = PEER_TOK_BLOCK
    assert t % tb == 0
    n = t // tb
    assert n >= PEER_LOOKAHEAD
    tok_spec = pl.BlockSpec((tb, d), lambda i: (i, 0))
    return pl.pallas_call(
        _peer_expert_body,
        grid=(n,),
        in_specs=[pl.BlockSpec((PEER_LOOKAHEAD * tb * PEER_SEL,), lambda i: (0,), memory_space=pltpu.SMEM),
                  pl.BlockSpec((tb * PEER_SEL,), lambda i: (jnp.minimum(i + PEER_LOOKAHEAD, n - 1),),
                               memory_space=pltpu.SMEM),
                  tok_spec,
                  pl.BlockSpec((tb, PEER_SEL), lambda i: (i, 0)),
                  tok_spec,
                  pl.BlockSpec(memory_space=pl.ANY)],
        out_specs=tok_spec,
        out_shape=jax.ShapeDtypeStruct((t, d), F32),
        scratch_shapes=[pltpu.VMEM((PEER_SLOTS, tb * PEER_SEL * ROW_PITCH, LANES), jnp.uint32),
                        pltpu.SemaphoreType.DMA((PEER_SLOTS,))],
        compiler_params=_cparams(("arbitrary",)),
        name="peer_expert",
    )(idx_flat, idx_flat, hn, gate_tok, h, uv_tab)


def peer_expert_route_fused(idx_flat, hn, gate_tok, h_full, uv_tab, nw, wq_bf16, subkeys_bf16, part):
    tp, d = hn.shape
    tb = PEER_TOK_BLOCK
    n = tp // tb
    tiles = tp // LANES
    assert tp % LANES == 0 and tiles * PEER_HEADS == n and n >= PEER_LOOKAHEAD
    hq = 2 * PEER_HALF
    e_off = part * n
    r_off = (part + 1) * tiles
    tok_spec = pl.BlockSpec((tb, d), lambda i: (i, 0))
    tile_of = lambda i: i // PEER_HEADS
    head_of = lambda i: lax.rem(i, PEER_HEADS)
    return pl.pallas_call(
        _peer_fused_body,
        grid=(n,),
        in_specs=[pl.BlockSpec((PEER_LOOKAHEAD * tb * PEER_SEL,), lambda i: (0,), memory_space=pltpu.SMEM),
                  pl.BlockSpec((tb * PEER_SEL,), lambda i: (jnp.minimum(i + PEER_LOOKAHEAD, n - 1),),
                               memory_space=pltpu.SMEM),
                  tok_spec,
                  pl.BlockSpec((tb, PEER_SEL), lambda i: (i, 0)),
                  pl.BlockSpec((tb, d), lambda i: (e_off + i, 0)),
                  pl.BlockSpec(memory_space=pl.ANY),
                  pl.BlockSpec((LANES, d), lambda i: (r_off + tile_of(i), 0)),
                  pl.BlockSpec((1, d), lambda i: (0, 0)),
                  pl.BlockSpec((d, hq), lambda i: (0, head_of(i))),
                  pl.BlockSpec((1, 2, PEER_N_KEYS, PEER_HALF), lambda i: (head_of(i), 0, 0, 0))],
        out_specs=[tok_spec,
                   pl.BlockSpec((LANES, d), lambda i: (tile_of(i), 0)),
                   pl.BlockSpec((LANES, PEER_SEL), lambda i: (tile_of(i), 0)),
                   pl.BlockSpec((LANES, PEER_SEL), lambda i: (tile_of(i), 0))],
        out_shape=[jax.ShapeDtypeStruct((tp, d), F32),
                   jax.ShapeDtypeStruct((tp, d), F32),
                   jax.ShapeDtypeStruct((tp, PEER_SEL), jnp.int32),
                   jax.ShapeDtypeStruct((tp, PEER_SEL), F32)],
        scratch_shapes=[pltpu.VMEM((PEER_SLOTS, tb * PEER_SEL * ROW_PITCH, LANES), jnp.uint32),
                        pltpu.SemaphoreType.DMA((PEER_SLOTS,)),
                        pltpu.VMEM((LANES, d), BF16),
                        pltpu.VMEM((2, PEER_N_KEYS, LANES), F32),
                        pltpu.VMEM((1, PEER_SEL, LANES), jnp.int32),
                        pltpu.VMEM((1, PEER_SEL, LANES), F32)],
        compiler_params=_cparams(("arbitrary",)),
        name="peer_expert_route",
    )(idx_flat, idx_flat, hn, gate_tok, h_full, uv_tab, h_full, nw.reshape(1, d), wq_bf16, subkeys_bf16)


def _rmsnorm_body(x_ref, w_ref, o_ref):
    o_ref[...] = _rms(x_ref[...], w_ref[...])


def rmsnorm(x, w, tm=1024):
    t, d = x.shape
    tm = min(tm, t)
    return pl.pallas_call(
        _rmsnorm_body,
        grid=(t // tm,),
        in_specs=[pl.BlockSpec((tm, d), lambda i: (i, 0)), pl.BlockSpec((1, d), lambda i: (0, 0))],
        out_specs=pl.BlockSpec((tm, d), lambda i: (i, 0)),
        out_shape=jax.ShapeDtypeStruct((t, d), F32),
        compiler_params=_cparams(("parallel",)),
        name="rmsnorm",
    )(x, w.reshape(1, d))


def _pad_cols(v, start, width=SMALL_W):
    out = jnp.zeros((1, width), F32)
    return out.at[0, start:start + v.shape[0]].set(v.astype(F32))


def _even_mix(h, bsz, norm_w, w_in, w_out, ssd_conv_w, ssd_conv_b, ssd_dt_bias, ssd_a_log, ssd_d_skip,
              ssd_norm, ml_conv_w, ml_conv_b, ml_wq, ml_wk, ml_wv, ml_i_bias, ml_f_bias, ml_norm):
    d = D_MODEL
    o1 = d
    o2 = o1 + SSD_XBC
    o3 = o2 + SSD_HEADS
    o4 = o3 + d
    o5 = o4 + d
    o6 = o5 + MLSTM_HEADS
    o7 = o6 + MLSTM_HEADS
    w_small = jnp.concatenate([w_in[:, o2:o3], w_in[:, o5:o6], w_in[:, o6:o7],
                               jnp.zeros((d, SMALL_W - (o3 - o2) - (o7 - o5)), w_in.dtype)], axis=1)
    w_all = jnp.concatenate([w_in[:, :o1], w_in[:, o1:o2], w_in[:, o3:o4], w_in[:, o4:o5], w_small],
                            axis=1).astype(BF16)
    z, xbc, xm, og, small = norm_proj(h, norm_w, w_all, (d, SSD_XBC, d, d, SMALL_W), tm=256)
    bias_pad = (_pad_cols(ssd_dt_bias, SMALL_DT) + _pad_cols(ml_i_bias, SMALL_IG)
                + _pad_cols(ml_f_bias, SMALL_FG))
    y_a = ssd_mixer(z, xbc, small, ssd_conv_w, ssd_conv_b, bias_pad, _pad_cols(ssd_a_log, SMALL_DT),
                    _pad_cols(ssd_d_skip, SMALL_DT), ssd_norm, bsz)
    y_b = mlstm_mixer(xm, og, small, ml_conv_w, ml_conv_b, bias_pad, ml_wq.astype(BF16),
                      ml_wk.astype(BF16), ml_wv.astype(BF16), ml_norm, bsz)
    w_out_b = w_out.astype(BF16)
    return proj_residual([y_a, y_b], [w_out_b[:d], w_out_b[d:]], h)


def _odd_mix(h, bsz, norm_w, w_in, w_out, lb_logits, hg_norm):
    d = D_MODEL
    uq, uf, ui, ug = norm_proj(h, norm_w, w_in.astype(BF16), (d, d, d, d), tm=256)
    y = hgrn_mixer(uq, uf, ui, ug, lb_logits, hg_norm, bsz)
    return proj_residual([y], [w_out.astype(BF16)], h)


def _xattn(h, mem2d, bsz, norm_x, norm_m, wq, wkv, wo):
    k, v = norm_proj(mem2d, norm_m, wkv.astype(BF16), (D_MODEL, D_MODEL), out_dtype=BF16)
    return xattn_residual(h, norm_x, wq.astype(BF16), k, v, wo.astype(BF16), bsz)


def _peer(h, norm_w, wq, subkeys, u_tab, v_tab):
    t = h.shape[0]
    wq_b = wq.astype(BF16)
    sk_b = subkeys.astype(BF16)
    uv_tab = pack_expert_tables(u_tab, v_tab)
    parts = PEER_PARTS
    while t % (parts * LANES) != 0:
        parts //= 2
    tp = t // parts
    hn, idx, gate = peer_route(h[:tp], norm_w, wq_b, sk_b)
    outs = []
    for k in range(parts - 1):
        o, hn, idx, gate = peer_expert_route_fused(idx.reshape(tp * PEER_SEL), hn, gate, h, uv_tab,
                                                   norm_w, wq_b, sk_b, k)
        outs.append(o)
    outs.append(peer_expert_residual(idx.reshape(tp * PEER_SEL), hn, gate, h[(parts - 1) * tp:], uv_tab))
    return jnp.concatenate(outs, axis=0)


def kernel(x, mem, norm_mix, norm_xattn, norm_mem, norm_ffn, norm_final, ev_w_in, ev_w_out, ssd_conv_w, ssd_conv_b, ssd_dt_bias, ssd_a_log, ssd_d_skip, ssd_norm, ml_conv_w, ml_conv_b, ml_wq, ml_wk, ml_wv, ml_i_bias, ml_f_bias, ml_norm, od_w_in, od_w_out, hgrn_lb_logits, hgrn_norm, xa_wq, xa_wkv, xa_wo, peer_wq, peer_subkeys, peer_u, peer_v):
    bsz, seq, d = x.shape
    depth = norm_mix.shape[0]
    assert depth == 2 and hgrn_lb_logits.shape[0] == 2
    h = x.reshape(bsz * seq, d)
    mem2d = mem.reshape(bsz * MEM_LEN, d)
    for layer in range(depth):
        if layer % 2 == 0:
            e = layer // 2
            h = _even_mix(h, bsz, norm_mix[layer], ev_w_in[e], ev_w_out[e], ssd_conv_w[e], ssd_conv_b[e],
                          ssd_dt_bias[e], ssd_a_log[e], ssd_d_skip[e], ssd_norm[e], ml_conv_w[e],
                          ml_conv_b[e], ml_wq[e], ml_wk[e], ml_wv[e], ml_i_bias[e], ml_f_bias[e], ml_norm[e])
        else:
            o = layer // 2
            h = _odd_mix(h, bsz, norm_mix[layer], od_w_in[o], od_w_out[o], hgrn_lb_logits, hgrn_norm[o])
        h = _xattn(h, mem2d, bsz, norm_xattn[layer], norm_mem[layer], xa_wq[layer], xa_wkv[layer], xa_wo[layer])
        h = _peer(h, norm_ffn[layer], peer_wq[layer], peer_subkeys[layer], peer_u[layer], peer_v[layer])
    return rmsnorm(h, norm_final).reshape(bsz, seq, d)
```

```python
import functools

import jax
import jax.numpy as jnp
from jax import lax
from jax.experimental import pallas as pl
from jax.experimental.pallas import tpu as pltpu

F32 = jnp.float32
BF16 = jnp.bfloat16
HIGHEST = lax.Precision.HIGHEST

D_MODEL = 1024
EPS = 1e-6
CONV_WIDTH = 4
MEM_LEN = 256
SSD_HEAD_DIM = 64
SSD_HEADS = 16
SSD_GROUPS = 2
SSD_STATE = 128
SSD_XBC = D_MODEL + 2 * SSD_GROUPS * SSD_STATE
SSD_CHUNK = 128
MLSTM_HEADS = 4
MLSTM_HEAD_DIM = 256
MLSTM_CHUNK = 128
HGRN_HEADS = 8
HGRN_HEAD_DIM = 128
HGRN_CHUNK = 64
HGRN_CHUNKS_PER_STEP = 2
XATTN_HEADS = 4
XATTN_HEAD_DIM = 256
PEER_HEADS = 8
PEER_N_KEYS = 128
PEER_TOPK = 16
PEER_HALF = 128
PEER_SEL = PEER_HEADS * PEER_TOPK

SUBLANES = 8
LANES = 128
SMALL_W = LANES
SMALL_DT = 0
SMALL_IG = SSD_HEADS
SMALL_FG = SSD_HEADS + MLSTM_HEADS

VMEM_LIMIT = 52 * 1024 * 1024


def _cparams(sem):
    return pltpu.CompilerParams(dimension_semantics=sem, vmem_limit_bytes=VMEM_LIMIT)


def _sigmoid(x):
    return 1.0 / (1.0 + jnp.exp(-x))


def _softplus(x):
    return jnp.maximum(x, 0.0) + jnp.log(1.0 + jnp.exp(-jnp.abs(x)))


def _rms(x, w):
    return x * lax.rsqrt(jnp.mean(x * x, axis=-1, keepdims=True) + EPS) * w


def _dot(a, b):
    return jnp.dot(a.astype(BF16), b.astype(BF16), preferred_element_type=F32)


def _dot_nt(a, b):
    return lax.dot_general(a.astype(BF16), b.astype(BF16), (((1,), (1,)), ((), ())),
                           preferred_element_type=F32)


def _cumsum_rows(x):
    n = x.shape[0]
    r = lax.broadcasted_iota(jnp.int32, (n, n), 0)
    c = lax.broadcasted_iota(jnp.int32, (n, n), 1)
    tril = (r >= c).astype(F32)
    return jnp.dot(tril, x, precision=HIGHEST, preferred_element_type=F32)


def _causal(n):
    r = lax.broadcasted_iota(jnp.int32, (n, n), 0)
    c = lax.broadcasted_iota(jnp.int32, (n, n), 1)
    return r >= c


def _norm_proj_body(x_ref, nw_ref, w_ref, *out_refs, splits):
    xn = _rms(x_ref[...], nw_ref[...]).astype(BF16)
    off = 0
    for o_ref, n in zip(out_refs, splits):
        o_ref[...] = jnp.dot(xn, w_ref[:, off:off + n], preferred_element_type=F32).astype(o_ref.dtype)
        off += n


def norm_proj(x, nw, w_bf16, splits, tm=512, out_dtype=F32):
    t, d = x.shape
    n = w_bf16.shape[1]
    assert sum(splits) == n
    tm = min(tm, t)
    assert t % tm == 0
    return pl.pallas_call(
        functools.partial(_norm_proj_body, splits=tuple(splits)),
        grid=(t // tm,),
        in_specs=[pl.BlockSpec((tm, d), lambda i: (i, 0)),
                  pl.BlockSpec((1, d), lambda i: (0, 0)),
                  pl.BlockSpec((d, n), lambda i: (0, 0))],
        out_specs=[pl.BlockSpec((tm, s), lambda i: (i, 0)) for s in splits],
        out_shape=[jax.ShapeDtypeStruct((t, s), out_dtype) for s in splits],
        compiler_params=_cparams(("parallel",)),
        name="norm_proj",
    )(x, nw.reshape(1, d), w_bf16)


def _proj_residual_body(*refs, n_parts):
    a_refs = refs[:n_parts]
    w_refs = refs[n_parts:2 * n_parts]
    r_ref = refs[2 * n_parts]
    o_ref = refs[2 * n_parts + 1]
    acc = r_ref[...]
    for a_ref, w_ref in zip(a_refs, w_refs):
        acc = acc + jnp.dot(a_ref[...].astype(BF16), w_ref[...], preferred_element_type=F32)
    o_ref[...] = acc


def proj_residual(parts, ws_bf16, res, tm=512):
    t, d = res.shape
    tm = min(tm, t)
    n_parts = len(parts)
    in_specs = ([pl.BlockSpec((tm, p.shape[1]), lambda i: (i, 0)) for p in parts]
                + [pl.BlockSpec(w.shape, lambda i: (0, 0)) for w in ws_bf16]
                + [pl.BlockSpec((tm, d), lambda i: (i, 0))])
    return pl.pallas_call(
        functools.partial(_proj_residual_body, n_parts=n_parts),
        grid=(t // tm,),
        in_specs=in_specs,
        out_specs=pl.BlockSpec((tm, d), lambda i: (i, 0)),
        out_shape=jax.ShapeDtypeStruct((t, d), F32),
        compiler_params=_cparams(("parallel",)),
        name="proj_residual",
    )(*parts, *ws_bf16, res)


def _conv_silu(x_in, xpad_ref, cw_ref, cb_ref, n_rows):
    xpad_ref[SUBLANES:SUBLANES + n_rows, :] = x_in
    acc = cb_ref[...] + cw_ref[CONV_WIDTH - 1:CONV_WIDTH, :] * x_in
    for k in range(CONV_WIDTH - 1):
        start = SUBLANES - (CONV_WIDTH - 1) + k
        acc = acc + cw_ref[k:k + 1, :] * xpad_ref[start:start + n_rows, :]
    xpad_ref[0:SUBLANES, :] = x_in[n_rows - SUBLANES:n_rows, :]
    return acc * _sigmoid(acc)


def _ssd_body(z_ref, xbc_ref, sm_ref, cw_ref, cb_ref, bias_ref, alog_ref, dsk_ref, nw_ref,
              o_ref, xpad_ref, state_ref, ybuf_ref):
    L, P, N = SSD_CHUNK, SSD_HEAD_DIM, SSD_STATE
    heads_per_group = SSD_HEADS // SSD_GROUPS

    @pl.when(pl.program_id(1) == 0)
    def _init():
        xpad_ref[0:SUBLANES, :] = jnp.zeros((SUBLANES, SSD_XBC), F32)
        state_ref[...] = jnp.zeros_like(state_ref)

    xa = _conv_silu(xbc_ref[...], xpad_ref, cw_ref, cb_ref, L)
    dt = _softplus(sm_ref[...] + bias_ref[...])
    a = -jnp.exp(alog_ref[...])
    cs_col = _cumsum_rows(dt * a)
    cs_row = cs_col.T
    causal = _causal(L)
    first_head = lax.broadcasted_iota(jnp.int32, (L, 2 * P), 1) < P

    for g in range(SSD_GROUPS):
        b_g = xa[:, D_MODEL + g * N:D_MODEL + (g + 1) * N]
        c_g = xa[:, D_MODEL + (SSD_GROUPS + g) * N:D_MODEL + (SSD_GROUPS + g + 1) * N]
        cb = _dot_nt(c_g, b_g)
        b_gt = b_g.T.astype(BF16)
        c_gb = c_g.astype(BF16)
        for e2 in range(heads_per_group // 2):
            pair = (g * heads_per_group) // 2 + e2
            col0 = SMALL_DT + 2 * pair
            col1 = col0 + 1

            def per_lane(arr, r0=None):
                a0 = arr[:, col0:col0 + 1] if r0 is None else arr[r0:r0 + 1, col0:col0 + 1]
                a1 = arr[:, col1:col1 + 1] if r0 is None else arr[r0:r0 + 1, col1:col1 + 1]
                return jnp.where(first_head[0:a0.shape[0], :], a0, a1)

            cs_c = per_lane(cs_col)
            cs_last = per_lane(cs_col, L - 1)
            x_p = xa[:, pair * 2 * P:(pair + 1) * 2 * P]
            x_dt = x_p * per_lane(dt)
            x_dtb = x_dt.astype(BF16)
            ys = []
            for col in (col0, col1):
                decay = jnp.exp(jnp.where(causal, cs_col[:, col:col + 1] - cs_row[col:col + 1, :], -jnp.inf))
                ys.append(jnp.dot((cb * decay).astype(BF16), x_dtb, preferred_element_type=F32))
            y = jnp.where(first_head, ys[0], ys[1])
            s_prev = state_ref[pair]
            y = y + jnp.dot(c_gb, s_prev.astype(BF16), preferred_element_type=F32) * jnp.exp(cs_c)
            y = y + x_p * per_lane(dsk_ref[...])
            ybuf_ref[:, pair * 2 * P:(pair + 1) * 2 * P] = y
            x_end = (x_dt * jnp.exp(cs_last - cs_c)).astype(BF16)
            state_ref[pair] = jnp.exp(cs_last) * s_prev + jnp.dot(b_gt, x_end, preferred_element_type=F32)

    zz = z_ref[...]
    y = ybuf_ref[...] * (zz * _sigmoid(zz))
    gw = D_MODEL // SSD_GROUPS
    for g in range(SSD_GROUPS):
        o_ref[:, g * gw:(g + 1) * gw] = _rms(y[:, g * gw:(g + 1) * gw], nw_ref[:, g * gw:(g + 1) * gw])


def ssd_mixer(z, xbc, small, conv_w, conv_b, bias_pad, alog_pad, dskip_pad, norm_w, bsz):
    t = z.shape[0]
    L = SSD_CHUNK
    nc = t // bsz // L
    row = lambda b, c: (b * nc + c, 0)
    const = lambda b, c: (0, 0)
    return pl.pallas_call(
        _ssd_body,
        grid=(bsz, nc),
        in_specs=[pl.BlockSpec((L, D_MODEL), row),
                  pl.BlockSpec((L, SSD_XBC), row),
                  pl.BlockSpec((L, SMALL_W), row),
                  pl.BlockSpec((CONV_WIDTH, SSD_XBC), const),
                  pl.BlockSpec((1, SSD_XBC), const),
                  pl.BlockSpec((1, SMALL_W), const),
                  pl.BlockSpec((1, SMALL_W), const),
                  pl.BlockSpec((1, SMALL_W), const),
                  pl.BlockSpec((1, D_MODEL), const)],
        out_specs=pl.BlockSpec((L, D_MODEL), row),
        out_shape=jax.ShapeDtypeStruct((t, D_MODEL), F32),
        scratch_shapes=[pltpu.VMEM((L + SUBLANES, SSD_XBC), F32),
                        pltpu.VMEM((SSD_HEADS // 2, SSD_STATE, 2 * SSD_HEAD_DIM), F32),
                        pltpu.VMEM((L, D_MODEL), F32)],
        compiler_params=_cparams(("parallel", "arbitrary")),
        name="ssd_mixer",
    )(z, xbc, small, conv_w, conv_b.reshape(1, -1), bias_pad, alog_pad, dskip_pad, norm_w.reshape(1, -1))


def _mlstm_body(xm_ref, og_ref, sm_ref, cw_ref, cb_ref, bias_ref, wq_ref, wk_ref, wv_ref, nw_ref,
                o_ref, xpad_ref, c_ref, n_ref, m_ref):
    L, H, Dh = MLSTM_CHUNK, MLSTM_HEADS, MLSTM_HEAD_DIM

    @pl.when(pl.program_id(1) == 0)
    def _init():
        xpad_ref[0:SUBLANES, :] = jnp.zeros((SUBLANES, D_MODEL), F32)
        c_ref[...] = jnp.zeros_like(c_ref)
        n_ref[...] = jnp.zeros_like(n_ref)
        m_ref[...] = jnp.zeros_like(m_ref)

    xm = xm_ref[...]
    xc = _conv_silu(xm, xpad_ref, cw_ref, cb_ref, L)
    smb = sm_ref[...] + bias_ref[...]
    logf = jnp.minimum(smb, 0.0) - jnp.log(1.0 + jnp.exp(-jnp.abs(smb)))
    bcum_col = _cumsum_rows(logf)
    bcum_row = bcum_col.T
    smb_row = smb.T
    causal = _causal(L)

    for h in range(H):
        ci, cf = SMALL_IG + h, SMALL_FG + h
        bc = bcum_col[:, cf:cf + 1]
        br = bcum_row[cf:cf + 1, :]
        ic = smb[:, ci:ci + 1]
        ir = smb_row[ci:ci + 1, :]
        b_end = bcum_col[L - 1:L, cf:cf + 1]
        dlog = jnp.where(causal, bc - br + ir, -jnp.inf)
        m_intra = jnp.max(dlog, axis=1, keepdims=True)
        g_end = b_end - bc + ic
        m_chunk = jnp.max(g_end, axis=0, keepdims=True)
        w_end = jnp.exp(g_end - m_chunk)

        sl = slice(h * Dh, (h + 1) * Dh)
        q = _dot(xc[:, sl], wq_ref[h])
        k = _dot(xc[:, sl], wk_ref[h]) * (Dh ** -0.5)
        v = _dot(xm[:, sl], wv_ref[h])
        qb, kb, vb = q.astype(BF16), k.astype(BF16), v.astype(BF16)

        m_prev = m_ref[h:h + 1, 0:1]
        c_prev = c_ref[h]
        n_prev = n_ref[h:h + 1, :]
        m_t = jnp.maximum(bc + m_prev, m_intra)
        a_t = jnp.exp(bc + m_prev - m_t)
        qk = _dot_nt(qb, kb) * jnp.exp(dlog - m_t)
        num = a_t * jnp.dot(qb, c_prev.astype(BF16), preferred_element_type=F32) \
            + jnp.dot(qk.astype(BF16), vb, preferred_element_type=F32)
        den = a_t * jnp.sum(q * n_prev, axis=1, keepdims=True) + jnp.sum(qk, axis=1, keepdims=True)
        hid = num / jnp.maximum(jnp.abs(den), jnp.exp(-m_t))
        hid = _rms(hid, nw_ref[:, sl])
        og = og_ref[:, sl]
        o_ref[:, sl] = hid * _sigmoid(og)

        m_new = jnp.maximum(b_end + m_prev, m_chunk)
        a_old = jnp.exp(b_end + m_prev - m_new)
        a_new = jnp.exp(m_chunk - m_new)
        kw = k * w_end
        c_local = jnp.dot(kw.T.astype(BF16), vb, preferred_element_type=F32)
        c_ref[h] = a_old * c_prev + a_new * c_local
        n_ref[h:h + 1, :] = a_old * n_prev + a_new * jnp.sum(kw, axis=0, keepdims=True)
        m_ref[h:h + 1, :] = jnp.broadcast_to(m_new, (1, LANES))


def mlstm_mixer(xm, og, small, conv_w, conv_b, bias_pad, wq, wk, wv, norm_w, bsz):
    t = xm.shape[0]
    L = MLSTM_CHUNK
    nc = t // bsz // L
    row = lambda b, c: (b * nc + c, 0)
    const = lambda b, c: (0, 0)
    const3 = lambda b, c: (0, 0, 0)
    wspec = pl.BlockSpec((MLSTM_HEADS, MLSTM_HEAD_DIM, MLSTM_HEAD_DIM), const3)
    return pl.pallas_call(
        _mlstm_body,
        grid=(bsz, nc),
        in_specs=[pl.BlockSpec((L, D_MODEL), row),
                  pl.BlockSpec((L, D_MODEL), row),
                  pl.BlockSpec((L, SMALL_W), row),
                  pl.BlockSpec((CONV_WIDTH, D_MODEL), const),
                  pl.BlockSpec((1, D_MODEL), const),
                  pl.BlockSpec((1, SMALL_W), const),
                  wspec, wspec, wspec,
                  pl.BlockSpec((1, D_MODEL), const)],
        out_specs=pl.BlockSpec((L, D_MODEL), row),
        out_shape=jax.ShapeDtypeStruct((t, D_MODEL), F32),
        scratch_shapes=[pltpu.VMEM((L + SUBLANES, D_MODEL), F32),
                        pltpu.VMEM((MLSTM_HEADS, MLSTM_HEAD_DIM, MLSTM_HEAD_DIM), F32),
                        pltpu.VMEM((SUBLANES, MLSTM_HEAD_DIM), F32),
                        pltpu.VMEM((SUBLANES, LANES), F32)],
        compiler_params=_cparams(("parallel", "arbitrary")),
        name="mlstm_mixer",
    )(xm, og, small, conv_w, conv_b.reshape(1, -1), bias_pad, wq, wk, wv, norm_w.reshape(1, -1))


def _hgrn_body(uq_ref, uf_ref, ui_ref, ug_ref, lbl_ref, nw_ref, o_ref, s_ref):
    L, H, Dk = HGRN_CHUNK, HGRN_HEADS, HGRN_HEAD_DIM

    @pl.when(pl.program_id(1) == 0)
    def _init():
        s_ref[...] = jnp.zeros_like(s_ref)

    lg = lbl_ref[...]
    lmax = jnp.max(lg, axis=0, keepdims=True)
    le = jnp.exp(lg - lmax)
    lb = le[0:1, :] / jnp.sum(le, axis=0, keepdims=True)

    causal = _causal(L)
    for sub in range(uq_ref.shape[0] // L):
        rows = slice(sub * L, (sub + 1) * L)
        uf = uf_ref[rows, :]
        uq = uq_ref[rows, :]
        logf = jnp.log(lb + (1.0 - lb) * _sigmoid(uf))
        k = (1.0 - lb) * _sigmoid(-uf)
        q = uq * _sigmoid(uq)
        v = ui_ref[rows, :]
        gc = _cumsum_rows(logf)
        g_ref = gc[L // 2:L // 2 + 1, :]
        g_last = gc[L - 1:L, :]
        q_in = (q * jnp.exp(gc - g_ref)).astype(BF16)
        k_in = (k * jnp.exp(g_ref - gc)).astype(BF16)
        q_off = (q * jnp.exp(gc)).astype(BF16)
        k_end = k * jnp.exp(g_last - gc)
        decay = jnp.exp(g_last)
        vb = v.astype(BF16)
        ug = ug_ref[rows, :]

        for h in range(H):
            sl = slice(h * Dk, (h + 1) * Dk)
            att = jnp.where(causal, _dot_nt(q_in[:, sl], k_in[:, sl]), 0.0)
            st_prev = s_ref[h]
            o = jnp.dot(att.astype(BF16), vb[:, sl], preferred_element_type=F32) \
                + _dot_nt(q_off[:, sl], st_prev)
            s_ref[h] = decay[:, sl] * st_prev + _dot(v[:, sl].T, k_end[:, sl])
            g = ug[:, sl]
            o_ref[rows, sl] = _rms(o, nw_ref[:, sl]) * (g * _sigmoid(g))


def hgrn_mixer(uq, uf, ui, ug, lb_logits, norm_w, bsz):
    t = uq.shape[0]
    seq = t // bsz
    rows = HGRN_CHUNK * HGRN_CHUNKS_PER_STEP
    if seq % rows != 0:
        rows = HGRN_CHUNK
    nc = seq // rows
    row = lambda b, c: (b * nc + c, 0)
    const = lambda b, c: (0, 0)
    rs = pl.BlockSpec((rows, D_MODEL), row)
    return pl.pallas_call(
        _hgrn_body,
        grid=(bsz, nc),
        in_specs=[rs, rs, rs, rs,
                  pl.BlockSpec((2, D_MODEL), const),
                  pl.BlockSpec((1, D_MODEL), const)],
        out_specs=rs,
        out_shape=jax.ShapeDtypeStruct((t, D_MODEL), F32),
        scratch_shapes=[pltpu.VMEM((HGRN_HEADS, HGRN_HEAD_DIM, HGRN_HEAD_DIM), F32)],
        compiler_params=_cparams(("parallel", "arbitrary")),
        name="hgrn_mixer",
    )(uq, uf, ui, ug, lb_logits, norm_w.reshape(1, -1))


def _xattn_body(h_ref, nw_ref, wq_ref, k_ref, v_ref, wo_ref, o_ref, obuf_ref):
    hv = h_ref[...]
    hn = _rms(hv, nw_ref[...]).astype(BF16)
    q = jnp.dot(hn, wq_ref[...], preferred_element_type=F32)
    scale = XATTN_HEAD_DIM ** -0.5
    for a in range(XATTN_HEADS):
        sl = slice(a * XATTN_HEAD_DIM, (a + 1) * XATTN_HEAD_DIM)
        s = _dot_nt(q[:, sl], k_ref[:, sl]) * scale
        s = s - jnp.max(s, axis=1, keepdims=True)
        p = jnp.exp(s)
        p = p / jnp.sum(p, axis=1, keepdims=True)
        obuf_ref[:, sl] = jnp.dot(p.astype(BF16), v_ref[:, sl], preferred_element_type=F32).astype(BF16)
    o_ref[...] = hv + jnp.dot(obuf_ref[...], wo_ref[...], preferred_element_type=F32)


def xattn_residual(h, nw, wq_bf16, k_bf16, v_bf16, wo_bf16, bsz, tm=512):
    t, d = h.shape
    seq = t // bsz
    tm = min(tm, seq)
    nt = seq // tm
    row = lambda b, i: (b * nt + i, 0)
    const = lambda b, i: (0, 0)
    mem = lambda b, i: (b, 0)
    return pl.pallas_call(
        _xattn_body,
        grid=(bsz, nt),
        in_specs=[pl.BlockSpec((tm, d), row),
                  pl.BlockSpec((1, d), const),
                  pl.BlockSpec((d, d), const),
                  pl.BlockSpec((MEM_LEN, d), mem),
                  pl.BlockSpec((MEM_LEN, d), mem),
                  pl.BlockSpec((d, d), const)],
        out_specs=pl.BlockSpec((tm, d), row),
        out_shape=jax.ShapeDtypeStruct((t, d), F32),
        scratch_shapes=[pltpu.VMEM((tm, d), BF16)],
        compiler_params=_cparams(("parallel", "parallel")),
        name="xattn_residual",
    )(h, nw.reshape(1, d), wq_bf16, k_bf16, v_bf16, wo_bf16)


def _topk_rows_steps(arrays, k):
    arrays = list(arrays)
    iotas = [lax.broadcasted_iota(jnp.int32, s.shape, 0) for s in arrays]
    vals = [[] for _ in arrays]
    idxs = [[] for _ in arrays]
    for _ in range(k):
        for n, s in enumerate(arrays):
            m = jnp.max(s, axis=0, keepdims=True)
            i = jnp.min(jnp.where(s == m, iotas[n], s.shape[0]), axis=0, keepdims=True)
            vals[n].append(m)
            idxs[n].append(i)
            arrays[n] = jnp.where(iotas[n] == i, -jnp.inf, s)
        yield
    return [(jnp.concatenate(v, axis=0), jnp.concatenate(i, axis=0)) for v, i in zip(vals, idxs)]


def _run(steps):
    try:
        while True:
            next(steps)
    except StopIteration as done:
        return done.value


def _select_rows(table, sel, n):
    out = jnp.zeros(sel.shape, jnp.int32)
    for a in range(n):
        out = jnp.where(sel == a, table[a:a + 1, :], out)
    return out


def _route_norm(head, h_ref, nw_ref, hn_ref, hnb_ref):
    @pl.when(head == 0)
    def _norm():
        xn = _rms(h_ref[...], nw_ref[...])
        hn_ref[...] = xn
        hnb_ref[...] = xn.astype(BF16)


def _route_steps(head, wq_ref, sk_ref, hnb_ref, st_ref, oi_ref, og_ref, n_col_blocks):
    K = PEER_TOPK
    q = jnp.dot(hnb_ref[...], wq_ref[...], preferred_element_type=F32)
    for p in range(2):
        qp = q[:, p * PEER_HALF:(p + 1) * PEER_HALF]
        st = _dot_nt(sk_ref[0, p], qp)
        for cb in range(n_col_blocks):
            st_ref[p * n_col_blocks + cb] = st[:, cb * LANES:(cb + 1) * LANES]
    yield
    tops = yield from _topk_rows_steps([st_ref[j] for j in range(2 * n_col_blocks)], K)

    sub = lax.broadcasted_iota(jnp.int32, (SUBLANES, LANES), 0)
    cands = []
    for cb in range(n_col_blocks):
        v1, v2 = tops[cb][0], tops[n_col_blocks + cb][0]
        blocks = [v1[0:1, :] + v2[0:SUBLANES, :], v1[0:1, :] + v2[SUBLANES:K, :]]
        for a in range(1, SUBLANES):
            blocks.append(jnp.where(sub < K // (a + 1), v1[a:a + 1, :] + v2[0:SUBLANES, :], -jnp.inf))
        blocks.append(v1[SUBLANES:K, :] + v2[0:1, :])
        cands.append(jnp.concatenate(blocks, axis=0))
    yield
    cand_tops = yield from _topk_rows_steps(cands, K)

    row0 = pl.multiple_of(head * K, K)
    for cb in range(n_col_blocks):
        i1, i2 = tops[cb][1], tops[n_col_blocks + cb][1]
        top_s, pos = cand_tops[cb]
        blk = lax.shift_right_logical(pos, 3)
        within = lax.bitwise_and(pos, SUBLANES - 1)
        a_sel = jnp.where(blk <= 1, 0, jnp.where(blk <= SUBLANES, blk - 1, within + SUBLANES))
        b_sel = jnp.where(blk == 1, within + SUBLANES, jnp.where(blk <= SUBLANES, within, 0))
        e1 = _select_rows(i1, a_sel, K)
        e2 = _select_rows(i2, b_sel, K)
        oi_ref[cb, pl.ds(row0, K), :] = e1 * PEER_N_KEYS + e2
        ex = jnp.exp(top_s - jnp.max(top_s, axis=0, keepdims=True))
        og_ref[cb, pl.ds(row0, K), :] = ex / jnp.sum(ex, axis=0, keepdims=True)


def _route_emit(head, idx_ref, gate_ref, oi_ref, og_ref, n_col_blocks):
    @pl.when(head == PEER_HEADS - 1)
    def _emit():
        for cb in range(n_col_blocks):
            idx_ref[cb * LANES:(cb + 1) * LANES, :] = oi_ref[cb].T
            gate_ref[cb * LANES:(cb + 1) * LANES, :] = og_ref[cb].T


def _peer_route_body(h_ref, nw_ref, wq_ref, sk_ref, hn_ref, idx_ref, gate_ref,
                     hnb_ref, st_ref, oi_ref, og_ref, *, n_col_blocks):
    head = pl.program_id(1)
    _route_norm(head, h_ref, nw_ref, hn_ref, hnb_ref)
    _run(_route_steps(head, wq_ref, sk_ref, hnb_ref, st_ref, oi_ref, og_ref, n_col_blocks))
    _route_emit(head, idx_ref, gate_ref, oi_ref, og_ref, n_col_blocks)


def peer_route(h, nw, wq_bf16, subkeys_bf16, tm=512):
    t, d = h.shape
    tm = min(tm, t)
    ncb = tm // LANES
    hq = 2 * PEER_HALF
    return pl.pallas_call(
        functools.partial(_peer_route_body, n_col_blocks=ncb),
        grid=(t // tm, PEER_HEADS),
        in_specs=[pl.BlockSpec((tm, d), lambda i, hd: (i, 0)),
                  pl.BlockSpec((1, d), lambda i, hd: (0, 0)),
                  pl.BlockSpec((d, hq), lambda i, hd: (0, hd)),
                  pl.BlockSpec((1, 2, PEER_N_KEYS, PEER_HALF), lambda i, hd: (hd, 0, 0, 0))],
        out_specs=[pl.BlockSpec((tm, d), lambda i, hd: (i, 0)),
                   pl.BlockSpec((tm, PEER_SEL), lambda i, hd: (i, 0)),
                   pl.BlockSpec((tm, PEER_SEL), lambda i, hd: (i, 0))],
        out_shape=[jax.ShapeDtypeStruct((t, d), F32),
                   jax.ShapeDtypeStruct((t, PEER_SEL), jnp.int32),
                   jax.ShapeDtypeStruct((t, PEER_SEL), F32)],
        scratch_shapes=[pltpu.VMEM((tm, d), BF16),
                        pltpu.VMEM((2 * ncb, PEER_N_KEYS, LANES), F32),
                        pltpu.VMEM((ncb, PEER_SEL, LANES), jnp.int32),
                        pltpu.VMEM((ncb, PEER_SEL, LANES), F32)],
        compiler_params=_cparams(("parallel", "arbitrary")),
        name="peer_route",
    )(h, nw.reshape(1, d), wq_bf16, subkeys_bf16)


PEER_TOK_BLOCK = 16
PEER_PARTS = 8
PEER_SLOTS = 3
FUSED_ROUTE_EVERY = 7
PEER_LOOKAHEAD = PEER_SLOTS - 1
PEER_ISSUE_UNROLL = 16
ROW_CHUNKS = D_MODEL // LANES
UV_CHUNKS = ROW_CHUNKS
PIECES = 2 * ROW_CHUNKS
ROW_PITCH = UV_CHUNKS + 1
U_MASK = 0xFFFF0000


def _start_row_copy(uv_hbm, expert, buf, slot, row, sem, priority):
    pltpu.make_async_copy(uv_hbm.at[expert], buf.at[slot, pl.ds(row * ROW_PITCH, UV_CHUNKS), :],
                          sem.at[slot]).start(priority=priority)


def _wait_slot(buf, slot, sem):
    done = buf.at[slot, pl.ds(0, PEER_TOK_BLOCK * PEER_SEL * UV_CHUNKS), :]
    pltpu.make_async_copy(done, done, sem.at[slot]).wait()


def _expert_prime(i, idx_head_ref, uv_hbm, buf, sem):
    rows_per_block = PEER_TOK_BLOCK * PEER_SEL

    @pl.when(i == 0)
    def _prime():
        for b in range(PEER_LOOKAHEAD):
            def rows(ro, carry):
                for ri in range(PEER_ISSUE_UNROLL):
                    r = ro * PEER_ISSUE_UNROLL + ri
                    _start_row_copy(uv_hbm, idx_head_ref[b * rows_per_block + r], buf, b, r, sem, ri % 2)
                return carry

            lax.fori_loop(0, rows_per_block // PEER_ISSUE_UNROLL, rows, 0)


def _expert_drain(i, n_blocks, buf, sem):
    @pl.when(i == n_blocks - 1)
    def _drain():
        for k in range(1, PEER_SLOTS):
            _wait_slot(buf, lax.rem(i + k, PEER_SLOTS), sem)


def _expert_main(i, idx_ahead_ref, x_ref, gate_ref, h_ref, uv_hbm, o_ref, buf, sem,
                 side_steps=None, side_every=1):
    tb = PEER_TOK_BLOCK
    slot = lax.rem(i, PEER_SLOTS)
    ahead_slot = lax.rem(i + PEER_LOOKAHEAD, PEER_SLOTS)
    _wait_slot(buf, slot, sem)
    gate_t = gate_ref[...].T
    group = PEER_SEL // PIECES
    pieces_done = 0

    for tk in range(tb):
        base = tk * PEER_SEL * ROW_PITCH

        def issue(piece):
            nonlocal pieces_done
            for k in range(group):
                r = tk * PEER_SEL + piece * group + k
                _start_row_copy(uv_hbm, idx_ahead_ref[r], buf, ahead_slot, r, sem, k % 2)
            pieces_done += 1
            if side_steps is not None and pieces_done % side_every == 0:
                next(side_steps, None)

        acc = None
        for c in range(ROW_CHUNKS):
            issue(c)
            uv_c = buf[slot, pl.ds(base + c, PEER_SEL, stride=ROW_PITCH), :]
            u_c = pltpu.bitcast(lax.bitwise_and(uv_c, jnp.uint32(U_MASK)), F32)
            term = u_c * x_ref[tk:tk + 1, c * LANES:(c + 1) * LANES]
            acc = term if acc is None else acc + term
        hsel = jnp.sum(acc, axis=1, keepdims=True)
        act = 0.5 * hsel * (1.0 + lax.erf(hsel * (2.0 ** -0.5)))
        w = gate_t[:, tk:tk + 1] * act
        outs = []
        for c in range(ROW_CHUNKS):
            issue(ROW_CHUNKS + c)
            uv_c = buf[slot, pl.ds(base + c, PEER_SEL, stride=ROW_PITCH), :]
            v_c = pltpu.bitcast(lax.shift_left(uv_c, jnp.uint32(16)), F32)
            outs.append(jnp.sum(v_c * w, axis=0, keepdims=True))
        o_ref[tk:tk + 1, :] = h_ref[tk:tk + 1, :] + jnp.concatenate(outs, axis=1)

    if side_steps is not None:
        for _ in side_steps:
            pass


def _peer_expert_body(idx_head_ref, idx_ahead_ref, x_ref, gate_ref, h_ref, uv_hbm, o_ref, buf, sem):
    i = pl.program_id(0)
    _expert_prime(i, idx_head_ref, uv_hbm, buf, sem)
    _expert_main(i, idx_ahead_ref, x_ref, gate_ref, h_ref, uv_hbm, o_ref, buf, sem)
    _expert_drain(i, pl.num_programs(0), buf, sem)


def _peer_fused_body(idx_head_ref, idx_ahead_ref, x_ref, gate_ref, h_ref, uv_hbm,
                     rh_ref, nw_ref, wq_ref, sk_ref,
                     o_ref, hn_ref, ridx_ref, rgate_ref,
                     buf, sem, hnb_ref, st_ref, oi_ref, og_ref):
    i = pl.program_id(0)
    head = lax.rem(i, PEER_HEADS)
    _expert_prime(i, idx_head_ref, uv_hbm, buf, sem)
    _route_norm(head, rh_ref, nw_ref, hn_ref, hnb_ref)
    route = _route_steps(head, wq_ref, sk_ref, hnb_ref, st_ref, oi_ref, og_ref, 1)
    _expert_main(i, idx_ahead_ref, x_ref, gate_ref, h_ref, uv_hbm, o_ref, buf, sem,
                 side_steps=route, side_every=FUSED_ROUTE_EVERY)
    _route_emit(head, ridx_ref, rgate_ref, oi_ref, og_ref, 1)
    _expert_drain(i, pl.num_programs(0), buf, sem)


def peer_expert_residual(idx_flat, hn, gate_tok, h, uv_tab):
    t, d = h.shape
    tb = PEER_TOK_BLOCK
    assert t % tb == 0
    n = t // tb
    assert n >= PEER_LOOKAHEAD
    tok_spec = pl.BlockSpec((tb, d), lambda i: (i, 0))
    return pl.pallas_call(
        _peer_expert_body,
        grid=(n,),
        in_specs=[pl.BlockSpec((PEER_LOOKAHEAD * tb * PEER_SEL,), lambda i: (0,), memory_space=pltpu.SMEM),
                  pl.BlockSpec((tb * PEER_SEL,), lambda i: (jnp.minimum(i + PEER_LOOKAHEAD, n - 1),),
                               memory_space=pltpu.SMEM),
                  tok_spec,
                  pl.BlockSpec((tb, PEER_SEL), lambda i: (i, 0)),
                  tok_spec,
                  pl.BlockSpec(memory_space=pl.ANY)],
        out_specs=tok_spec,
        out_shape=jax.ShapeDtypeStruct((t, d), F32),
        scratch_shapes=[pltpu.VMEM((PEER_SLOTS, tb * PEER_SEL * ROW_PITCH, LANES), jnp.uint32),
                        pltpu.SemaphoreType.DMA((PEER_SLOTS,))],
        compiler_params=_cparams(("arbitrary",)),
        name="peer_expert",
    )(idx_flat, idx_flat, hn, gate_tok, h, uv_tab)


def peer_expert_route_fused(idx_flat, hn, gate_tok, h_full, uv_tab, nw, wq_bf16, subkeys_bf16, part):
    tp, d = hn.shape
    tb = PEER_TOK_BLOCK
    n = tp // tb
    tiles = tp // LANES
    assert tp % LANES == 0 and tiles * PEER_HEADS == n and n >= PEER_LOOKAHEAD
    hq = 2 * PEER_HALF
    e_off = part * n
    r_off = (part + 1) * tiles
    tok_spec = pl.BlockSpec((tb, d), lambda i: (i, 0))
    tile_of = lambda i: i // PEER_HEADS
    head_of = lambda i: lax.rem(i, PEER_HEADS)
    return pl.pallas_call(
        _peer_fused_body,
        grid=(n,),
        in_specs=[pl.BlockSpec((PEER_LOOKAHEAD * tb * PEER_SEL,), lambda i: (0,), memory_space=pltpu.SMEM),
                  pl.BlockSpec((tb * PEER_SEL,), lambda i: (jnp.minimum(i + PEER_LOOKAHEAD, n - 1),),
                               memory_space=pltpu.SMEM),
                  tok_spec,
                  pl.BlockSpec((tb, PEER_SEL), lambda i: (i, 0)),
                  pl.BlockSpec((tb, d), lambda i: (e_off + i, 0)),
                  pl.BlockSpec(memory_space=pl.ANY),
                  pl.BlockSpec((LANES, d), lambda i: (r_off + tile_of(i), 0)),
                  pl.BlockSpec((1, d), lambda i: (0, 0)),
                  pl.BlockSpec((d, hq), lambda i: (0, head_of(i))),
                  pl.BlockSpec((1, 2, PEER_N_KEYS, PEER_HALF), lambda i: (head_of(i), 0, 0, 0))],
        out_specs=[tok_spec,
                   pl.BlockSpec((LANES, d), lambda i: (tile_of(i), 0)),
                   pl.BlockSpec((LANES, PEER_SEL), lambda i: (tile_of(i), 0)),
                   pl.BlockSpec((LANES, PEER_SEL), lambda i: (tile_of(i), 0))],
        out_shape=[jax.ShapeDtypeStruct((tp, d), F32),
                   jax.ShapeDtypeStruct((tp, d), F32),
                   jax.ShapeDtypeStruct((tp, PEER_SEL), jnp.int32),
                   jax.ShapeDtypeStruct((tp, PEER_SEL), F32)],
        scratch_shapes=[pltpu.VMEM((PEER_SLOTS, tb * PEER_SEL * ROW_PITCH, LANES), jnp.uint32),
                        pltpu.SemaphoreType.DMA((PEER_SLOTS,)),
                        pltpu.VMEM((LANES, d), BF16),
                        pltpu.VMEM((2, PEER_N_KEYS, LANES), F32),
                        pltpu.VMEM((1, PEER_SEL, LANES), jnp.int32),
                        pltpu.VMEM((1, PEER_SEL, LANES), F32)],
        compiler_params=_cparams(("arbitrary",)),
        name="peer_expert_route",
    )(idx_flat, idx_flat, hn, gate_tok, h_full, uv_tab, h_full, nw.reshape(1, d), wq_bf16, subkeys_bf16)


def _rmsnorm_body(x_ref, w_ref, o_ref):
    o_ref[...] = _rms(x_ref[...], w_ref[...])


def rmsnorm(x, w, tm=1024):
    t, d = x.shape
    tm = min(tm, t)
    return pl.pallas_call(
        _rmsnorm_body,
        grid=(t // tm,),
        in_specs=[pl.BlockSpec((tm, d), lambda i: (i, 0)), pl.BlockSpec((1, d), lambda i: (0, 0))],
        out_specs=pl.BlockSpec((tm, d), lambda i: (i, 0)),
        out_shape=jax.ShapeDtypeStruct((t, d), F32),
        compiler_params=_cparams(("parallel",)),
        name="rmsnorm",
    )(x, w.reshape(1, d))


def _pad_cols(v, start, width=SMALL_W):
    out = jnp.zeros((1, width), F32)
    return out.at[0, start:start + v.shape[0]].set(v.astype(F32))


def _even_mix(h, bsz, norm_w, w_in, w_out, ssd_conv_w, ssd_conv_b, ssd_dt_bias, ssd_a_log, ssd_d_skip,
              ssd_norm, ml_conv_w, ml_conv_b, ml_wq, ml_wk, ml_wv, ml_i_bias, ml_f_bias, ml_norm):
    d = D_MODEL
    o1 = d
    o2 = o1 + SSD_XBC
    o3 = o2 + SSD_HEADS
    o4 = o3 + d
    o5 = o4 + d
    o6 = o5 + MLSTM_HEADS
    o7 = o6 + MLSTM_HEADS
    w_small = jnp.concatenate([w_in[:, o2:o3], w_in[:, o5:o6], w_in[:, o6:o7],
                               jnp.zeros((d, SMALL_W - (o3 - o2) - (o7 - o5)), w_in.dtype)], axis=1)
    w_all = jnp.concatenate([w_in[:, :o1], w_in[:, o1:o2], w_in[:, o3:o4], w_in[:, o4:o5], w_small],
                            axis=1).astype(BF16)
    z, xbc, xm, og, small = norm_proj(h, norm_w, w_all, (d, SSD_XBC, d, d, SMALL_W), tm=256)
    bias_pad = (_pad_cols(ssd_dt_bias, SMALL_DT) + _pad_cols(ml_i_bias, SMALL_IG)
                + _pad_cols(ml_f_bias, SMALL_FG))
    y_a = ssd_mixer(z, xbc, small, ssd_conv_w, ssd_conv_b, bias_pad, _pad_cols(ssd_a_log, SMALL_DT),
                    _pad_cols(ssd_d_skip, SMALL_DT), ssd_norm, bsz)
    y_b = mlstm_mixer(xm, og, small, ml_conv_w, ml_conv_b, bias_pad, ml_wq.astype(BF16),
                      ml_wk.astype(BF16), ml_wv.astype(BF16), ml_norm, bsz)
    w_out_b = w_out.astype(BF16)
    return proj_residual([y_a, y_b], [w_out_b[:d], w_out_b[d:]], h)


def _odd_mix(h, bsz, norm_w, w_in, w_out, lb_logits, hg_norm):
    d = D_MODEL
    uq, uf, ui, ug = norm_proj(h, norm_w, w_in.astype(BF16), (d, d, d, d), tm=256)
    y = hgrn_mixer(uq, uf, ui, ug, lb_logits, hg_norm, bsz)
    return proj_residual([y], [w_out.astype(BF16)], h)


def _xattn(h, mem2d, bsz, norm_x, norm_m, wq, wkv, wo):
    k, v = norm_proj(mem2d, norm_m, wkv.astype(BF16), (D_MODEL, D_MODEL), out_dtype=BF16)
    return xattn_residual(h, norm_x, wq.astype(BF16), k, v, wo.astype(BF16), bsz)


def _peer(h, norm_w, wq, subkeys, u_tab, v_tab):
    t = h.shape[0]
    n_exp = u_tab.shape[0]
    wq_b = wq.astype(BF16)
    sk_b = subkeys.astype(BF16)
    u_bits = lax.bitcast_convert_type(u_tab.astype(BF16), jnp.uint16).astype(jnp.uint32)
    v_bits = lax.bitcast_convert_type(v_tab.astype(BF16), jnp.uint16).astype(jnp.uint32)
    uv_tab = ((u_bits << 16) | v_bits).reshape(n_exp, ROW_CHUNKS, LANES)
    parts = PEER_PARTS
    while t % (parts * LANES) != 0:
        parts //= 2
    tp = t // parts
    hn, idx, gate = peer_route(h[:tp], norm_w, wq_b, sk_b)
    outs = []
    for k in range(parts - 1):
        o, hn, idx, gate = peer_expert_route_fused(idx.reshape(tp * PEER_SEL), hn, gate, h, uv_tab,
                                                   norm_w, wq_b, sk_b, k)
        outs.append(o)
    outs.append(peer_expert_residual(idx.reshape(tp * PEER_SEL), hn, gate, h[(parts - 1) * tp:], uv_tab))
    return jnp.concatenate(outs, axis=0)


def kernel(x, mem, norm_mix, norm_xattn, norm_mem, norm_ffn, norm_final, ev_w_in, ev_w_out, ssd_conv_w, ssd_conv_b, ssd_dt_bias, ssd_a_log, ssd_d_skip, ssd_norm, ml_conv_w, ml_conv_b, ml_wq, ml_wk, ml_wv, ml_i_bias, ml_f_bias, ml_norm, od_w_in, od_w_out, hgrn_lb_logits, hgrn_norm, xa_wq, xa_wkv, xa_wo, peer_wq, peer_subkeys, peer_u, peer_v):
    bsz, seq, d = x.shape
    depth = norm_mix.shape[0]
    assert depth == 2 and hgrn_lb_logits.shape[0] == 2
    h = x.reshape(bsz * seq, d)
    mem2d = mem.reshape(bsz * MEM_LEN, d)
    for layer in range(depth):
        if layer % 2 == 0:
            e = layer // 2
            h = _even_mix(h, bsz, norm_mix[layer], ev_w_in[e], ev_w_out[e], ssd_conv_w[e], ssd_conv_b[e],
                          ssd_dt_bias[e], ssd_a_log[e], ssd_d_skip[e], ssd_norm[e], ml_conv_w[e],
                          ml_conv_b[e], ml_wq[e], ml_wk[e], ml_wv[e], ml_i_bias[e], ml_f_bias[e], ml_norm[e])
        else:
            o = layer // 2
            h = _odd_mix(h, bsz, norm_mix[layer], od_w_in[o], od_w_out[o], hgrn_lb_logits, hgrn_norm[o])
        h = _xattn(h, mem2d, bsz, norm_xattn[layer], norm_mem[layer], xa_wq[layer], xa_wkv[layer], xa_wo[layer])
        h = _peer(h, norm_ffn[layer], peer_wq[layer], peer_subkeys[layer], peer_u[layer], peer_v[layer])
    return rmsnorm(h, norm_final).reshape(bsz, seq, d)
```
